```python
import math
import jax
import jax.numpy as jnp
from jax import lax
import numpy as np

D_MODEL = 2048
BATCH = 4
SEQ = 2048
DEPTH = 4
DEC_BATCH = 128
DEC_SEQ = 8
PAST_LEN = 16384
PAGE_SIZE = 128

RET_HEADS = 8
RET_DH = 128
RET_WIDTH = RET_HEADS * RET_DH
ROPE_BASE = 10000.0
M_HEADS = 8
M_DH = 128
M_WIDTH = M_HEADS * M_DH
CONV_W = 4
S5_WIDTH = 1024
S5_GROUP = 16
S5_GROUPS = S5_WIDTH // S5_GROUP
S5_STATE = 64
N_BRANCH = 3
CHUNK = 128
PEER_HEADS = 8
PEER_DQ = 256
N_KEYS = 128
N_EXPERTS = N_KEYS * N_KEYS
PEER_TOPK = 16
PEER_BLOCK = 128
EPS = 1e-6
IN_WIDTH = 4 * RET_WIDTH + 3 * M_WIDTH + 2 * M_HEADS + S5_WIDTH + N_BRANCH * D_MODEL

kernel_name = 'hybrid_retention_mlstm_s5_peer_step'


def rms_norm(x, w):
    x32 = x.astype(jnp.float32)
    y = x32 * lax.rsqrt(jnp.mean(x32 * x32, axis=-1, keepdims=True) + EPS)
    return (y * w.astype(jnp.float32)).astype(x.dtype)


def head_norm(x):
    x32 = x.astype(jnp.float32)
    xc = x32 - jnp.mean(x32, axis=-1, keepdims=True)
    return xc * lax.rsqrt(jnp.mean(xc * xc, axis=-1, keepdims=True) + EPS)


def split_in(z):
    sizes = [RET_WIDTH] * 4 + [M_WIDTH] * 3 + [M_HEADS, M_HEADS, S5_WIDTH] + [D_MODEL] * N_BRANCH
    return jnp.split(z, np.cumsum(sizes)[:-1].tolist(), axis=-1)


def chunk_len(t):
    return CHUNK if t % CHUNK == 0 else t


def to_chunks(x, L):
    b, t, h = x.shape[:3]
    rest = x.shape[3:]
    x = x.reshape((b, t // L, L, h) + rest)
    return x.transpose((1, 0, 3, 2) + tuple(range(4, x.ndim)))


def from_chunks(o):
    nc, b, h, L, d = o.shape
    return o.transpose(1, 0, 3, 2, 4).reshape(b, nc * L, h, d)


def rotary(x, pos):
    half = x.shape[-1] // 2
    inv = jnp.exp(-math.log(ROPE_BASE) * jnp.arange(half, dtype=jnp.float32) / half)
    ang = pos[:, None] * inv[None, :]
    cos = jnp.cos(ang)[None, :, None, :]
    sin = jnp.sin(ang)[None, :, None, :]
    x1, x2 = x[..., :half], x[..., half:]
    return jnp.concatenate([x1 * cos - x2 * sin, x1 * sin + x2 * cos], axis=-1)


def retention(q, k, v, s0):
    t = q.shape[1]
    L = chunk_len(t)
    lg = jnp.log1p(-jnp.exp2(-5.0 - jnp.arange(RET_HEADS, dtype=jnp.float32)))
    j = jnp.arange(L, dtype=jnp.float32)
    diff = j[:, None] - j[None, :]
    intra = jnp.where(diff >= 0, jnp.exp(lg[:, None, None] * jnp.maximum(diff, 0.0)), 0.0)
    cross = jnp.exp(lg[:, None] * (j + 1.0))
    kdec = jnp.exp(lg[:, None] * (L - 1.0 - j))
    cdec = jnp.exp(lg * L)

    def step(s, inp):
        qc, kc, vc = inp
        att = jnp.einsum('bhid,bhjd->bhij', qc, kc) * intra
        o = jnp.einsum('bhij,bhjv->bhiv', att, vc) + jnp.einsum('bhid,bhdv->bhiv', qc, s) * cross[..., None]
        s = cdec[:, None, None] * s + jnp.einsum('bhjd,bhjv->bhdv', kc * kdec[..., None], vc)
        return s, o

    s, o = lax.scan(step, s0, (to_chunks(q, L), to_chunks(k, L), to_chunks(v, L)))
    return from_chunks(o), s


def mlstm(q, k, v, ig, lf, c0, n0, m0):
    t = q.shape[1]
    L = chunk_len(t)
    causal = jnp.tril(jnp.ones((L, L), dtype=bool))

    def step(carry, inp):
        c, n, m = carry
        qc, kc, vc, ic, fc = inp
        b = jnp.cumsum(fc, axis=-1)
        dlog = jnp.where(causal, b[..., :, None] - b[..., None, :] + ic[..., None, :], -jnp.inf)
        inter = b + m[..., None]
        mt = jnp.maximum(inter, jnp.max(dlog, axis=-1))
        s = jnp.einsum('bhid,bhjd->bhij', qc, kc) * jnp.exp(dlog - mt[..., None])
        w_int = jnp.exp(inter - mt)
        num = jnp.einsum('bhij,bhjv->bhiv', s, vc) + w_int[..., None] * jnp.einsum('bhid,bhdv->bhiv', qc, c)
        den = jnp.sum(s, axis=-1) + w_int * jnp.einsum('bhid,bhd->bhi', qc, n)
        h = num / jnp.maximum(jnp.abs(den), jnp.exp(-mt))[..., None]
        m_new = mt[..., -1]
        tail = jnp.exp(b[..., -1:] - b + ic - m_new[..., None])
        carry_dec = jnp.exp(b[..., -1] + m - m_new)
        c = carry_dec[..., None, None] * c + jnp.einsum('bhj,bhjd,bhjv->bhdv', tail, kc, vc)
        n = carry_dec[..., None] * n + jnp.einsum('bhj,bhjd->bhd', tail, kc)
        return (c, n, m_new), h

    (c, n, m), h = lax.scan(step, (c0, n0, m0), (to_chunks(q, L), to_chunks(k, L), to_chunks(v, L),
                                                 to_chunks(ig, L), to_chunks(lf, L)))
    return from_chunks(h), c, n, m


def causal_conv(u, buf, w, b):
    t = u.shape[1]
    xp = jnp.concatenate([buf, u], axis=1)
    out = b + sum(xp[:, i:i + t] * w[i] for i in range(CONV_W))
    return out, xp[:, -(CONV_W - 1):]


def s5_combine(e1, e2):
    a1r, a1i, b1r, b1i = e1
    a2r, a2i, b2r, b2i = e2
    return (a2r * a1r - a2i * a1i, a2r * a1i + a2i * a1r,
            a2r * b1r - a2i * b1i + b2r, a2r * b1i + a2i * b1r + b2i)


def s5_ssm(u, h0_re, h0_im, a_re, a_im, log_dt, b_re, b_im, c_re, c_im, d, w_glu, b_glu):
    f32 = jnp.float32
    bsz, t, _ = u.shape
    dt = jnp.exp(log_dt.astype(f32))[:, None]
    lam_re = -jnp.abs(a_re.astype(f32))
    lam_im = a_im.astype(f32)
    mag = jnp.exp(lam_re * dt)
    ab_re = mag * jnp.cos(lam_im * dt)
    ab_im = mag * jnp.sin(lam_im * dt)
    den = lam_re * lam_re + lam_im * lam_im
    co_re = ((ab_re - 1.0) * lam_re + ab_im * lam_im) / den
    co_im = (ab_im * lam_re - (ab_re - 1.0) * lam_im) / den
    br = b_re.astype(f32)
    bi = b_im.astype(f32)
    bb_re = co_re[..., None] * br - co_im[..., None] * bi
    bb_im = co_re[..., None] * bi + co_im[..., None] * br
    ug = u.reshape(bsz, t, S5_GROUPS, S5_GROUP)
    bu_re = jnp.einsum('btgc,gpc->btgp', ug, bb_re)
    bu_im = jnp.einsum('btgc,gpc->btgp', ug, bb_im)
    bu_re = bu_re.at[:, 0].add(ab_re * h0_re - ab_im * h0_im)
    bu_im = bu_im.at[:, 0].add(ab_re * h0_im + ab_im * h0_re)
    a_r = jnp.broadcast_to(ab_re, bu_re.shape)
    a_i = jnp.broadcast_to(ab_im, bu_im.shape)
    _, _, h_re, h_im = lax.associative_scan(s5_combine, (a_r, a_i, bu_re, bu_im), axis=1)
    y = jnp.einsum('btgp,gcp->btgc', h_re, c_re.astype(f32)) - jnp.einsum('btgp,gcp->btgc', h_im, c_im.astype(f32))
    y = y.reshape(bsz, t, S5_WIDTH) + d * u
    g = jax.nn.gelu(y)
    out = g * jax.nn.sigmoid(g @ w_glu + b_glu)
    return out, h_re[:, -1], h_im[:, -1]


def peer(h, wq, k1, k2, u_tab, v_tab):
    f32 = jnp.float32
    bsz, t, dm = h.shape
    n = bsz * t
    hf = h.reshape(n, dm)
    q = (hf @ wq).astype(f32).reshape(n, PEER_HEADS, PEER_DQ)
    half = PEER_DQ // 2
    s1 = jnp.einsum('nhd,hkd->nhk', q[..., :half], k1.astype(f32))
    s2 = jnp.einsum('nhd,hkd->nhk', q[..., half:], k2.astype(f32))
    v1, i1 = lax.top_k(s1, PEER_TOPK)
    v2, i2 = lax.top_k(s2, PEER_TOPK)
    cand = (v1[..., :, None] + v2[..., None, :]).reshape(n, PEER_HEADS, PEER_TOPK * PEER_TOPK)
    cidx = (i1[..., :, None] * N_KEYS + i2[..., None, :]).reshape(n, PEER_HEADS, PEER_TOPK * PEER_TOPK)
    top_s, top_p = lax.top_k(cand, PEER_TOPK)
    eidx = jnp.take_along_axis(cidx, top_p, axis=-1).reshape(n, PEER_HEADS * PEER_TOPK)
    gate = jax.nn.softmax(top_s, axis=-1).reshape(n, PEER_HEADS * PEER_TOPK)
    nb = -(-n // PEER_BLOCK)
    pad = nb * PEER_BLOCK - n
    hb = jnp.pad(hf, ((0, pad), (0, 0))).reshape(nb, PEER_BLOCK, dm)
    ib = jnp.pad(eidx, ((0, pad), (0, 0))).reshape(nb, PEER_BLOCK, PEER_HEADS * PEER_TOPK)
    gb = jnp.pad(gate, ((0, pad), (0, 0))).reshape(nb, PEER_BLOCK, PEER_HEADS * PEER_TOPK)

    def expert_block(args):
        xb, idb, gtb = args
        ue = jnp.take(u_tab, idb, axis=0)
        act = jax.nn.gelu(jnp.einsum('nd,nkd->nk', xb, ue).astype(f32)) * gtb
        ve = jnp.take(v_tab, idb, axis=0)
        return jnp.einsum('nk,nkd->nd', act, ve.astype(f32))

    out = lax.map(expert_block, (hb, ib, gb)).reshape(nb * PEER_BLOCK, dm)[:n]
    return out.reshape(bsz, t, dm).astype(h.dtype)


def token_mix(h, pos0, st, p, l):
    f32 = jnp.float32
    bsz, t, _ = h.shape
    z = (h @ p['w_in'][l]).astype(f32)
    rq, rk, rv, rg, mu, mv, mo, mi, mf, su, gr, gm, gs = split_in(z)
    st_ret, st_c, st_n, st_m, st_conv, st_sr, st_si = [s.astype(f32) for s in st]
    pos = jnp.arange(t, dtype=f32) + pos0
    rshape = (bsz, t, RET_HEADS, RET_DH)
    q = rotary(rq.reshape(rshape), pos)
    k = rotary(rk.reshape(rshape), pos) * RET_DH ** -0.5
    o_r, new_ret = retention(q, k, rv.reshape(rshape), st_ret)
    o_r = head_norm(o_r).reshape(bsz, t, RET_WIDTH) * p['ret_gn_w'][l] * jax.nn.silu(rg)
    br = o_r @ p['w_ret_out'][l]
    cu, new_conv = causal_conv(mu, st_conv, p['mlstm_conv_w'][l], p['mlstm_conv_b'][l])
    ca = jax.nn.silu(cu)
    cah = ca.reshape(bsz, t, M_HEADS, M_DH)
    qm = jnp.einsum('bthd,hde->bthe', cah, p['mlstm_wq'][l])
    km = jnp.einsum('bthd,hde->bthe', cah, p['mlstm_wk'][l]) * M_DH ** -0.5
    ig = mi + p['mlstm_b_i'][l]
    lf = jax.nn.log_sigmoid(mf + p['mlstm_b_f'][l])
    hm, new_c, new_n, new_m = mlstm(qm, km, mv.reshape(bsz, t, M_HEADS, M_DH), ig, lf, st_c, st_n, st_m)
    hm = jax.nn.sigmoid(mo).reshape(bsz, t, M_HEADS, M_DH) * hm
    om = head_norm(hm).reshape(bsz, t, M_WIDTH) * p['mlstm_gn_w'][l] + p['mlstm_skip'][l] * ca
    bm = om @ p['w_mlstm_out'][l]
    os_, new_sr, new_si = s5_ssm(su, st_sr, st_si, p['s5_a_re'][l], p['s5_a_im'][l], p['s5_log_dt'][l],
                                 p['s5_b_re'][l], p['s5_b_im'][l], p['s5_c_re'][l], p['s5_c_im'][l],
                                 p['s5_d'][l], p['s5_w_glu'][l], p['s5_b_glu'][l])
    bs = os_ @ p['w_s5_out'][l]
    merged = jax.nn.sigmoid(gr) * br + jax.nn.sigmoid(gm) * bm + jax.nn.sigmoid(gs) * bs
    y = (merged @ p['w_o'][l]).astype(h.dtype)
    return y, (new_ret, new_c, new_n, new_m, new_conv, new_sr, new_si)


def zero_state(bsz):
    f32 = jnp.float32
    return (jnp.zeros((bsz, RET_HEADS, RET_DH, RET_DH), f32), jnp.zeros((bsz, M_HEADS, M_DH, M_DH), f32),
            jnp.zeros((bsz, M_HEADS, M_DH), f32), jnp.zeros((bsz, M_HEADS), f32),
            jnp.zeros((bsz, CONV_W - 1, M_WIDTH), f32), jnp.zeros((bsz, S5_GROUPS, S5_STATE), f32),
            jnp.zeros((bsz, S5_GROUPS, S5_STATE), f32))


def trunk(x, c, pos0, init_states, p):
    new = []
    for l in range(DEPTH):
        mod = jnp.split(jax.nn.silu(c) @ p['ada_w'][l] + p['ada_b'][l], 6, axis=-1)
        sh1, sc1, g1, sh2, sc2, g2 = [m[:, None, :] for m in mod]
        h = rms_norm(x, p['norm1_w'][l]) * (1 + sc1) + sh1
        y, st = token_mix(h, pos0, init_states[l], p, l)
        x = x + g1 * y
        h = rms_norm(x, p['norm2_w'][l]) * (1 + sc2) + sh2
        x = x + g2 * peer(h, p['peer_wq'][l], p['peer_k1'][l], p['peer_k2'][l], p['peer_u'][l], p['peer_v'][l])
        new.append(st)
    y = rms_norm(x, p['final_norm_w'])
    stacked = [jnp.stack([s[i] for s in new]) for i in range(7)]
    return y, stacked


def setup_inputs(seed: int = 0) -> dict:
    key = jax.random.key(seed)
    keys = jax.random.split(key, 64)
    cnt = [0]

    def nk():
        k = keys[cnt[0]]
        cnt[0] += 1
        return k

    def nrm(shape, scale):
        return jax.random.normal(nk(), shape, jnp.float32) * scale

    L, D = DEPTH, D_MODEL
    inp = {}
    inp['x_prompt'] = nrm((BATCH, SEQ, D), 1.0)
    inp['x_sample'] = nrm((DEC_BATCH, DEC_SEQ, D), 1.0)
    inp['state_ret'] = nrm((L, DEC_BATCH, RET_HEADS, RET_DH, RET_DH), 1.0)
    inp['state_mlstm_c'] = nrm((L, DEC_BATCH, M_HEADS, M_DH, M_DH), 0.5)
    inp['state_mlstm_n'] = nrm((L, DEC_BATCH, M_HEADS, M_DH), 0.5)
    inp['state_mlstm_m'] = nrm((L, DEC_BATCH, M_HEADS), 1.0)
    inp['state_mlstm_conv'] = nrm((L, DEC_BATCH, CONV_W - 1, M_WIDTH), 1.0)
    inp['state_s5_re'] = nrm((L, DEC_BATCH, S5_GROUPS, S5_STATE), 0.5)
    inp['state_s5_im'] = nrm((L, DEC_BATCH, S5_GROUPS, S5_STATE), 0.5)
    inp['c_prompt'] = nrm((BATCH, D), 1.0)
    inp['c_sample'] = nrm((DEC_BATCH, D), 1.0)
    inp['ada_w'] = nrm((L, D, 6 * D), 0.3 * D ** -0.5)
    inp['ada_b'] = nrm((L, 6 * D), 0.02)
    inp['norm1_w'] = 1.0 + nrm((L, D), 0.02)
    inp['norm2_w'] = 1.0 + nrm((L, D), 0.02)
    inp['final_norm_w'] = 1.0 + nrm((D,), 0.02)
    inp['w_in'] = nrm((L, D, IN_WIDTH), D ** -0.5)
    inp['ret_gn_w'] = 1.0 + nrm((L, RET_WIDTH), 0.02)
    inp['w_ret_out'] = nrm((L, RET_WIDTH, D), RET_WIDTH ** -0.5)
    inp['mlstm_conv_w'] = nrm((L, CONV_W, M_WIDTH), CONV_W ** -0.5)
    inp['mlstm_conv_b'] = nrm((L, M_WIDTH), 0.01)
    inp['mlstm_wq'] = nrm((L, M_HEADS, M_DH, M_DH), M_DH ** -0.5)
    inp['mlstm_wk'] = nrm((L, M_HEADS, M_DH, M_DH), M_DH ** -0.5)
    inp['mlstm_b_i'] = nrm((L, M_HEADS), 0.1)
    inp['mlstm_b_f'] = jnp.linspace(3.0, 6.0, M_HEADS, dtype=jnp.float32)[None, :] + nrm((L, M_HEADS), 0.1)
    inp['mlstm_gn_w'] = 1.0 + nrm((L, M_WIDTH), 0.02)
    inp['mlstm_skip'] = 1.0 + nrm((L, M_WIDTH), 0.02)
    inp['w_mlstm_out'] = nrm((L, M_WIDTH, D), M_WIDTH ** -0.5)
    inp['s5_a_re'] = -0.5 + nrm((L, S5_GROUPS, S5_STATE), 0.01)
    inp['s5_a_im'] = math.pi * jnp.arange(S5_STATE, dtype=jnp.float32)[None, None, :] + nrm((L, S5_GROUPS, S5_STATE), 0.01)
    inp['s5_log_dt'] = jax.random.uniform(nk(), (L, S5_GROUPS), jnp.float32, math.log(1e-3), math.log(1e-1))
    inp['s5_b_re'] = nrm((L, S5_GROUPS, S5_STATE, S5_GROUP), S5_GROUP ** -0.5)
    inp['s5_b_im'] = nrm((L, S5_GROUPS, S5_STATE, S5_GROUP), S5_GROUP ** -0.5)
    inp['s5_c_re'] = nrm((L, S5_GROUPS, S5_GROUP, S5_STATE), S5_STATE ** -0.5)
    inp['s5_c_im'] = nrm((L, S5_GROUPS, S5_GROUP, S5_STATE), S5_STATE ** -0.5)
    inp['s5_d'] = nrm((L, S5_WIDTH), 1.0)
    inp['s5_w_glu'] = nrm((L, S5_WIDTH, S5_WIDTH), S5_WIDTH ** -0.5)
    inp['s5_b_glu'] = nrm((L, S5_WIDTH), 0.01)
    inp['w_s5_out'] = nrm((L, S5_WIDTH, D), S5_WIDTH ** -0.5)
    inp['w_o'] = nrm((L, D, D), D ** -0.5)
    inp['peer_wq'] = nrm((L, D, PEER_HEADS * PEER_DQ), D ** -0.5)
    inp['peer_k1'] = nrm((L, PEER_HEADS, N_KEYS, PEER_DQ // 2), (PEER_DQ // 2) ** -0.5)
    inp['peer_k2'] = nrm((L, PEER_HEADS, N_KEYS, PEER_DQ // 2), (PEER_DQ // 2) ** -0.5)
    inp['peer_u'] = nrm((L, N_EXPERTS, D), D ** -0.5)
    inp['peer_v'] = nrm((L, N_EXPERTS, D), 1.0)
    return inp


def reference(x_prompt, x_sample, state_ret, state_mlstm_c, state_mlstm_n, state_mlstm_m, state_mlstm_conv,
              state_s5_re, state_s5_im, c_prompt, c_sample, ada_w, ada_b, norm1_w, norm2_w, final_norm_w,
              w_in, ret_gn_w, w_ret_out, mlstm_conv_w, mlstm_conv_b, mlstm_wq, mlstm_wk, mlstm_b_i, mlstm_b_f,
              mlstm_gn_w, mlstm_skip, w_mlstm_out, s5_a_re, s5_a_im, s5_log_dt, s5_b_re, s5_b_im, s5_c_re,
              s5_c_im, s5_d, s5_w_glu, s5_b_glu, w_s5_out, w_o, peer_wq, peer_k1, peer_k2, peer_u, peer_v):
    p = dict(ada_w=ada_w, ada_b=ada_b, norm1_w=norm1_w, norm2_w=norm2_w, final_norm_w=final_norm_w,
             w_in=w_in, ret_gn_w=ret_gn_w, w_ret_out=w_ret_out, mlstm_conv_w=mlstm_conv_w,
             mlstm_conv_b=mlstm_conv_b, mlstm_wq=mlstm_wq, mlstm_wk=mlstm_wk, mlstm_b_i=mlstm_b_i,
             mlstm_b_f=mlstm_b_f, mlstm_gn_w=mlstm_gn_w, mlstm_skip=mlstm_skip, w_mlstm_out=w_mlstm_out,
             s5_a_re=s5_a_re, s5_a_im=s5_a_im, s5_log_dt=s5_log_dt, s5_b_re=s5_b_re, s5_b_im=s5_b_im,
             s5_c_re=s5_c_re, s5_c_im=s5_c_im, s5_d=s5_d, s5_w_glu=s5_w_glu, s5_b_glu=s5_b_glu,
             w_s5_out=w_s5_out, w_o=w_o, peer_wq=peer_wq, peer_k1=peer_k1, peer_k2=peer_k2,
             peer_u=peer_u, peer_v=peer_v)
    prompt_init = [zero_state(x_prompt.shape[0]) for _ in range(DEPTH)]
    y_prompt, ps = trunk(x_prompt, c_prompt, 0, prompt_init, p)
    sample_init = [(state_ret[l], state_mlstm_c[l], state_mlstm_n[l], state_mlstm_m[l], state_mlstm_conv[l],
                    state_s5_re[l], state_s5_im[l]) for l in range(DEPTH)]
    y_sample, ss = trunk(x_sample, c_sample, PAST_LEN, sample_init, p)
    p_ret, p_c, p_n, p_m, p_conv, p_sr, p_si = ps
    s_ret, s_c, s_n, s_m, s_conv, s_sr, s_si = ss
    return (y_prompt, y_sample, p_ret, p_c, p_n, p_m, p_conv, p_sr, p_si,
            s_ret, s_c, s_n, s_m, s_conv, s_sr, s_si)
```

```python
import functools
import math

import jax
import jax.numpy as jnp
import numpy as np
from jax import lax
from jax.experimental import pallas as pl
from jax.experimental.pallas import tpu as pltpu

F32 = jnp.float32
BF16 = jnp.bfloat16

EPS = 1e-6
ROPE_BASE = 10000.0
HEADS = 8
DH = 128
CONV_W = 4
S5_GROUP = 16
S5_STATE = 64
PEER_HEADS = 8
PEER_TOPK = 16
N_KEYS = 128
CHUNK = 128
LANES = 128
SUBLANES = 8
VMEM_LIMIT = 56 * 1024 * 1024


def _cparams(sem):
    return pltpu.CompilerParams(dimension_semantics=sem, vmem_limit_bytes=VMEM_LIMIT)


def _tile_cfg(B, T, target):
    if T >= target:
        bb, tt = 1, target
    else:
        bb, tt = min(B, target // T), T
    assert T % tt == 0 and B % bb == 0
    return bb, tt


def _silu(x):
    return x * jax.nn.sigmoid(x)


def _norm_mod(x, w, sc, sh):
    ms = jnp.mean(x * x, axis=-1, keepdims=True)
    y = x * lax.rsqrt(ms + EPS) * w
    return y * (1.0 + sc) + sh


def _head_norm(x):
    mu = jnp.mean(x, axis=-1, keepdims=True)
    xc = x - mu
    var = jnp.mean(xc * xc, axis=-1, keepdims=True)
    return xc * lax.rsqrt(var + EPS)


def _dot(a, b):
    return jnp.dot(a, b, preferred_element_type=F32)


def _dot_nt(a, b, **kw):
    return lax.dot_general(a, b, (((1,), (1,)), ((), ())), preferred_element_type=F32, **kw)


def _dot_tn(a, b):
    return lax.dot_general(a, b, (((0,), (0,)), ((), ())), preferred_element_type=F32)


def _rows(start, n):
    if isinstance(start, int):
        return pl.ds(start, n)
    return pl.ds(pl.multiple_of(start, SUBLANES), n)


def _for_each_seq(bb, body):
    if bb == 1:
        body(0)
    else:
        def f(bi, c):
            body(bi)
            return c
        lax.fori_loop(0, bb, f, 0)


def _ada_kernel(c_ref, w_ref, b_ref, o_ref):
    a = _silu(c_ref[...]).astype(BF16)
    o_ref[...] = _dot(a, w_ref[...].astype(BF16)) + b_ref[...]


def _ada(c_all, ada_w, ada_b):
    L, D, N = ada_w.shape
    Bc = c_all.shape[0]
    TN = 1024
    return pl.pallas_call(
        _ada_kernel,
        grid=(L, N // TN),
        in_specs=[pl.BlockSpec((Bc, D), lambda l, j: (0, 0)),
                  pl.BlockSpec((None, D, TN), lambda l, j: (l, 0, j)),
                  pl.BlockSpec((None, 1, TN), lambda l, j: (l, 0, j))],
        out_specs=pl.BlockSpec((None, Bc, TN), lambda l, j: (l, 0, j)),
        out_shape=jax.ShapeDtypeStruct((L, Bc, N), F32),
        compiler_params=_cparams(("parallel", "parallel")),
        name="ada",
    )(c_all, ada_w, ada_b.reshape(L, 1, N))


def _in_kernel(x_ref, nw_ref, sh_ref, sc_ref, wm_ref, wg_ref, z_ref, zg_ref, h_scr):
    @pl.when(pl.program_id(1) == 0)
    def _():
        h = _norm_mod(x_ref[...], nw_ref[...], sc_ref[...], sh_ref[...])
        hb = h.reshape(h_scr.shape).astype(BF16)
        h_scr[...] = hb
        zg_ref[...] = _dot(hb, wg_ref[...])
    z_ref[...] = _dot(h_scr[...], wm_ref[...])


def _in_proj(x, mod, l, moff, nw, wm, wg):
    B, T, D = x.shape
    bb, tt = _tile_cfg(B, T, 512)
    TM, nT = bb * tt, T // tt
    nI = (B // bb) * nT
    N = wm.shape[1]
    TN = 1024
    xmap = lambda i, j: (i // nT, i % nT, 0)
    mmap = lambda k: (lambda i, j: (l, moff // bb + i // nT, 0, k))
    return pl.pallas_call(
        _in_kernel,
        grid=(nI, N // TN),
        in_specs=[pl.BlockSpec((bb, tt, D), xmap),
                  pl.BlockSpec((1, D), lambda i, j: (0, 0)),
                  pl.BlockSpec((None, bb, 1, D), mmap(0)),
                  pl.BlockSpec((None, bb, 1, D), mmap(1)),
                  pl.BlockSpec((D, TN), lambda i, j: (0, j)),
                  pl.BlockSpec((D, LANES), lambda i, j: (0, 0))],
        out_specs=[pl.BlockSpec((TM, TN), lambda i, j: (i, j)),
                   pl.BlockSpec((TM, LANES), lambda i, j: (i, 0))],
        out_shape=[jax.ShapeDtypeStruct((B * T, N), F32),
                   jax.ShapeDtypeStruct((B * T, LANES), F32)],
        scratch_shapes=[pltpu.VMEM((TM, D), BF16)],
        compiler_params=_cparams(("parallel", "arbitrary")),
        name="in_proj",
    )(x, nw, mod, mod, wm, wg)


def _ret_kernel(zq_ref, zk_ref, zv_ref, zg_ref, cq_ref, sq_ref, ck_ref, sk_ref, intra_ref, cross_ref,
                kdec_ref, cdec_ref, gn_ref, s0_ref, o_ref, s_ref, *, bb, L):
    @pl.when(pl.program_id(1) == 0)
    def _():
        s_ref[...] = s0_ref[...]

    cq, sq, ck, sk = cq_ref[...], sq_ref[...], ck_ref[...], sk_ref[...]

    def seq(bi):
        rows = _rows(bi * L, L)
        for h in range(HEADS):
            cols = slice(h * DH, (h + 1) * DH)
            q = zq_ref[rows, cols]
            k = zk_ref[rows, cols]
            v = zv_ref[rows, cols].astype(BF16)
            g = zg_ref[rows, cols]
            q = (q * cq + pltpu.roll(q, DH // 2, 1) * sq).astype(BF16)
            k = k * ck + pltpu.roll(k, DH // 2, 1) * sk
            s = s_ref[bi, h]
            att = _dot_nt(q, k.astype(BF16)) * intra_ref[h]
            o = _dot(att.astype(BF16), v) + _dot(q, s.astype(BF16)) * cross_ref[h]
            s_ref[bi, h] = cdec_ref[h] * s + _dot_tn((k * kdec_ref[h]).astype(BF16), v)
            o_ref[rows, cols] = _head_norm(o) * gn_ref[:, cols] * _silu(g)

    _for_each_seq(bb, seq)


def _retention(z, B, T, s0, tabs, gn_w):
    L = CHUNK if T % CHUNK == 0 else T
    nC = T // L
    bb = 1 if nC > 1 else min(B, 8)
    R = bb * L
    W = HEADS * DH
    cq, sq, ck, sk, intra, cross, kdec, cdec = tabs
    zspec = lambda k: pl.BlockSpec((R, W), lambda i, c: (i * nC + c, k))
    tspec = pl.BlockSpec((L, DH), lambda i, c: (c, 0))
    full = lambda a: pl.BlockSpec(a.shape, lambda i, c: (0,) * a.ndim)
    sspec = pl.BlockSpec((bb, HEADS, DH, DH), lambda i, c: (i, 0, 0, 0))
    return pl.pallas_call(
        functools.partial(_ret_kernel, bb=bb, L=L),
        grid=(B // bb, nC),
        in_specs=[zspec(0), zspec(1), zspec(2), zspec(3), tspec, tspec, tspec, tspec,
                  full(intra), full(cross), full(kdec), full(cdec), full(gn_w), sspec],
        out_specs=[pl.BlockSpec((R, W), lambda i, c: (i * nC + c, 0)), sspec],
        out_shape=[jax.ShapeDtypeStruct((B * T, W), F32),
                   jax.ShapeDtypeStruct((B, HEADS, DH, DH), F32)],
        compiler_params=_cparams(("parallel", "arbitrary")),
        name="ret",
    )(z, z, z, z, cq, sq, ck, sk, intra, cross, kdec, cdec, gn_w, s0)


def _ret_tables(T, pos0):
    L = CHUNK if T % CHUNK == 0 else T
    half = DH // 2
    inv = jnp.exp(-math.log(ROPE_BASE) * jnp.arange(half, dtype=F32) / half)
    pos = jnp.arange(T, dtype=F32) + pos0
    ang = pos[:, None] * inv[None, :]
    cos, sin = jnp.cos(ang), jnp.sin(ang)
    c = jnp.concatenate([cos, cos], axis=-1)
    s = jnp.concatenate([-sin, sin], axis=-1)
    kscale = DH ** -0.5
    lg = jnp.log1p(-jnp.exp2(-5.0 - jnp.arange(HEADS, dtype=F32)))
    j = jnp.arange(L, dtype=F32)
    diff = j[:, None] - j[None, :]
    intra = jnp.where(diff >= 0, jnp.exp(lg[:, None, None] * jnp.maximum(diff, 0.0)), 0.0)
    cross = jnp.exp(lg[:, None] * (j + 1.0))
    kdec = jnp.exp(lg[:, None] * (L - 1.0 - j))
    cdec = jnp.exp(lg * L)
    bl = lambda a: jnp.broadcast_to(a[..., None], a.shape + (DH,))
    return (c, s, c * kscale, s * kscale, intra, bl(cross), bl(kdec), bl(cdec[:, None]))


def _mlstm_kernel(zu_ref, zv_ref, zo_ref, zg_ref, cw_ref, cb_ref, wqk_ref, gb_ref, gn_ref, sk_ref,
                  c0_ref, n0_ref, m0_ref, conv0_ref,
                  o_ref, c_ref, n_ref, m_ref, conv_ref, xp_scr, tail_scr, ca_scr, *, bb, L, last):
    ci = pl.program_id(1)

    @pl.when(ci == 0)
    def _():
        c_ref[...] = c0_ref[...]
        n_ref[...] = n0_ref[...]
        m_ref[...] = m0_ref[...]
        tail_scr[...] = jnp.zeros(tail_scr.shape, F32)
        tail_scr[:, SUBLANES - (CONV_W - 1):, :] = conv0_ref[...]

    ri = lax.broadcasted_iota(jnp.int32, (L, L), 0)
    rj = lax.broadcasted_iota(jnp.int32, (L, L), 1)
    causal = rj <= ri
    tri = causal.astype(F32)
    lane = lax.broadcasted_iota(jnp.int32, (L, LANES), 1)
    kscale = DH ** -0.5

    def seq(bi):
        rows = _rows(bi * L, L)
        xp_scr[0:SUBLANES, :] = tail_scr[bi]
        xp_scr[SUBLANES:, :] = zu_ref[rows, :]
        cu = cb_ref[...]
        for i in range(CONV_W):
            cu = cu + xp_scr[pl.ds(SUBLANES - (CONV_W - 1) + i, L), :] * cw_ref[pl.ds(i, 1), :]
        tail_scr[bi] = xp_scr[L:L + SUBLANES, :]
        ca_scr[...] = _silu(cu)
        gz = zg_ref[rows, :] + gb_ref[...]
        gates = jnp.where(lane < HEADS, gz, jax.nn.log_sigmoid(gz))
        csum = jnp.dot(tri, gates, preferred_element_type=F32, precision=lax.Precision.HIGHEST)
        gates_t = gates.T
        csum_t = csum.T
        for h in range(HEADS):
            cols = slice(h * DH, (h + 1) * DH)
            ca = ca_scr[:, cols]
            qk = _dot(ca.astype(BF16), wqk_ref[h])
            q = qk[:, :DH]
            k = qk[:, DH:] * kscale
            qb = q.astype(BF16)
            v = zv_ref[rows, cols].astype(BF16)
            i_col = gates[:, h:h + 1]
            b_col = csum[:, HEADS + h:HEADS + h + 1]
            i_row = gates_t[h:h + 1, :]
            b_row = csum_t[HEADS + h:HEADS + h + 1, :]
            m_prev = m_ref[bi, pl.ds(h, 1), :][:, :1]
            c_prev = c_ref[bi, h]
            n_prev = n_ref[bi, pl.ds(h, 1), :]
            dlog = jnp.where(causal, b_col - b_row + i_row, -jnp.inf)
            inter = b_col + m_prev
            mt = jnp.maximum(inter, jnp.max(dlog, axis=-1, keepdims=True))
            s = _dot_nt(qb, k.astype(BF16)) * jnp.exp(dlog - mt)
            w_int = jnp.exp(inter - mt)
            num = _dot(s.astype(BF16), v) + w_int * _dot(qb, c_prev.astype(BF16))
            den = jnp.sum(s, axis=-1, keepdims=True) + w_int * jnp.sum(q * n_prev, axis=-1, keepdims=True)
            hh = num / jnp.maximum(jnp.abs(den), jnp.exp(-mt))
            m_new = mt[L - 1:L, :]
            b_last = b_col[L - 1:L, :]
            tail = jnp.exp(b_last - b_col + i_col - m_new)
            dec = jnp.exp(b_last + m_prev - m_new)
            kt = k * tail
            c_ref[bi, h] = dec * c_prev + _dot_tn(kt.astype(BF16), v)
            n_ref[bi, pl.ds(h, 1), :] = dec * n_prev + jnp.sum(kt, axis=0, keepdims=True)
            m_ref[bi, pl.ds(h, 1), :] = jnp.broadcast_to(m_new, (1, LANES))
            hm = jax.nn.sigmoid(zo_ref[rows, cols]) * hh
            o_ref[rows, cols] = _head_norm(hm) * gn_ref[:, cols] + sk_ref[:, cols] * ca

        @pl.when(ci == last)
        def _():
            conv_ref[bi] = xp_scr[pl.ds(L + SUBLANES - (CONV_W - 1), CONV_W - 1), :]

    _for_each_seq(bb, seq)


def _mlstm(z, zg, B, T, states, cw, cb, wqk, gb, gn_w, skip):
    L = CHUNK if T % CHUNK == 0 else T
    nC = T // L
    bb = 1 if nC > 1 else min(B, 8)
    R = bb * L
    W = HEADS * DH
    c0, n0, m0, conv0 = states
    zspec = lambda k: pl.BlockSpec((R, W), lambda i, c: (i * nC + c, k))
    full = lambda a: pl.BlockSpec(a.shape, lambda i, c: (0,) * a.ndim)
    cspec = pl.BlockSpec((bb, HEADS, DH, DH), lambda i, c: (i, 0, 0, 0))
    nspec = pl.BlockSpec((bb, HEADS, DH), lambda i, c: (i, 0, 0))
    vspec = pl.BlockSpec((bb, CONV_W - 1, W), lambda i, c: (i, 0, 0))
    return pl.pallas_call(
        functools.partial(_mlstm_kernel, bb=bb, L=L, last=nC - 1),
        grid=(B // bb, nC),
        in_specs=[zspec(4), zspec(5), zspec(6), pl.BlockSpec((R, LANES), lambda i, c: (i * nC + c, 0)),
                  full(cw), full(cb), full(wqk), full(gb), full(gn_w), full(skip),
                  cspec, nspec, nspec, vspec],
        out_specs=[pl.BlockSpec((R, W), lambda i, c: (i * nC + c, 0)), cspec, nspec, nspec, vspec],
        out_shape=[jax.ShapeDtypeStruct((B * T, W), F32),
                   jax.ShapeDtypeStruct((B, HEADS, DH, DH), F32),
                   jax.ShapeDtypeStruct((B, HEADS, DH), F32),
                   jax.ShapeDtypeStruct((B, HEADS, LANES), F32),
                   jax.ShapeDtypeStruct((B, CONV_W - 1, W), F32)],
        scratch_shapes=[pltpu.VMEM((L + SUBLANES, W), F32),
                        pltpu.VMEM((bb, SUBLANES, W), F32),
                        pltpu.VMEM((L, W), F32)],
        compiler_params=_cparams(("parallel", "arbitrary")),
        name="mlstm",
    )(z, z, z, zg, cw, cb, wqk, gb, gn_w, skip, c0, n0, m0, conv0)


def _s5_kernel(zu_ref, bw_ref, ar_ref, ai_ref, pr_ref, pi_ref, cwr_ref, cwi_ref, d_ref, wglu_ref, bglu_ref,
               h0r_ref, h0i_ref, o_ref, hr_ref, hi_ref, xr_scr, xi_scr, *, bb, Lc):
    @pl.when(pl.program_id(1) == 0)
    def _():
        hr_ref[...] = h0r_ref[...]
        hi_ref[...] = h0i_ref[...]

    NS = xr_scr.shape[1]
    KB = bw_ref.shape[0]
    SB = NS // KB
    u = zu_ref[...]
    ub = u.astype(BF16)
    for kb in range(KB):
        r = _dot(ub[:, kb * LANES:(kb + 1) * LANES], bw_ref[kb])
        xr_scr[:, kb * SB:(kb + 1) * SB] = r[:, :SB]
        xi_scr[:, kb * SB:(kb + 1) * SB] = r[:, SB:]

    CW = 512
    ngrp = Lc // SUBLANES

    def seq(bi):
        def grp(gi, carry):
            rows = _rows(bi * Lc + gi * SUBLANES, SUBLANES)
            for cc in range(NS // CW):
                cols = slice(cc * CW, (cc + 1) * CW)
                xr = xr_scr[rows, cols]
                xi = xi_scr[rows, cols]
                for lv in range(3):
                    rr = pltpu.roll(xr, 1 << lv, 0)
                    ri = pltpu.roll(xi, 1 << lv, 0)
                    a_r = ar_ref[lv, :, cols]
                    a_i = ai_ref[lv, :, cols]
                    xr, xi = xr + a_r * rr - a_i * ri, xi + a_r * ri + a_i * rr
                cr = jnp.broadcast_to(hr_ref[bi, :, cols], (SUBLANES, CW))
                cim = jnp.broadcast_to(hi_ref[bi, :, cols], (SUBLANES, CW))
                p_r = pr_ref[:, cols]
                p_i = pi_ref[:, cols]
                hr = xr + p_r * cr - p_i * cim
                hi = xi + p_r * cim + p_i * cr
                xr_scr[rows, cols] = hr
                xi_scr[rows, cols] = hi
                hr_ref[bi, :, cols] = hr[SUBLANES - 1:, :]
                hi_ref[bi, :, cols] = hi[SUBLANES - 1:, :]
            return carry
        lax.fori_loop(0, ngrp, grp, 0)

    _for_each_seq(bb, seq)

    hrb = xr_scr[...].astype(BF16)
    hib = xi_scr[...].astype(BF16)
    ys = []
    for kb in range(KB):
        cols = slice(kb * SB, (kb + 1) * SB)
        ys.append(_dot(hrb[:, cols], cwr_ref[kb]) + _dot(hib[:, cols], cwi_ref[kb]))
    y = jnp.concatenate(ys, axis=1) + d_ref[...] * u
    g = jax.nn.gelu(y)
    o_ref[...] = g * jax.nn.sigmoid(_dot(g.astype(BF16), wglu_ref[...]) + bglu_ref[...])


def _s5(z, B, T, h0r, h0i, tabs, d, wglu, bglu):
    bw, ar, ai, pr, pi, cwr, cwi = tabs
    Lc = min(T, 256)
    nC = T // Lc
    bb = 1 if nC > 1 else min(B, max(1, 128 // T))
    R = bb * Lc
    W = d.shape[1]
    NS = h0r.shape[-1]
    full = lambda a: pl.BlockSpec(a.shape, lambda i, c: (0,) * a.ndim)
    hspec = pl.BlockSpec((bb, 1, NS), lambda i, c: (i, 0, 0))
    return pl.pallas_call(
        functools.partial(_s5_kernel, bb=bb, Lc=Lc),
        grid=(B // bb, nC),
        in_specs=[pl.BlockSpec((R, W), lambda i, c: (i * nC + c, 7)),
                  full(bw), full(ar), full(ai), full(pr), full(pi), full(cwr), full(cwi),
                  full(d), full(wglu), full(bglu), hspec, hspec],
        out_specs=[pl.BlockSpec((R, W), lambda i, c: (i * nC + c, 0)), hspec, hspec],
        out_shape=[jax.ShapeDtypeStruct((B * T, W), F32),
                   jax.ShapeDtypeStruct((B, 1, NS), F32),
                   jax.ShapeDtypeStruct((B, 1, NS), F32)],
        scratch_shapes=[pltpu.VMEM((R, NS), F32), pltpu.VMEM((R, NS), F32)],
        compiler_params=_cparams(("parallel", "arbitrary")),
        name="s5",
    )(z, bw, ar, ai, pr, pi, cwr, cwi, d, wglu, bglu, h0r, h0i)


def _s5_tables(a_re, a_im, log_dt, b_re, b_im, c_re, c_im):
    G, P = a_re.shape
    dt = jnp.exp(log_dt.astype(F32))[:, None]
    lam_re = -jnp.abs(a_re.astype(F32))
    lam_im = a_im.astype(F32)

    def power(n):
        n = jnp.asarray(n, F32)[:, None, None]
        mag = jnp.exp(lam_re * dt * n)
        return ((mag * jnp.cos(lam_im * dt * n)).reshape(-1, G * P),
                (mag * jnp.sin(lam_im * dt * n)).reshape(-1, G * P))

    mag = jnp.exp(lam_re * dt)
    ab_re = mag * jnp.cos(lam_im * dt)
    ab_im = mag * jnp.sin(lam_im * dt)
    den = lam_re * lam_re + lam_im * lam_im
    co_re = ((ab_re - 1.0) * lam_re + ab_im * lam_im) / den
    co_im = (ab_im * lam_re - (ab_re - 1.0) * lam_im) / den
    br, bi = b_re.astype(F32), b_im.astype(F32)
    bb_re = co_re[..., None] * br - co_im[..., None] * bi
    bb_im = co_re[..., None] * bi + co_im[..., None] * br
    gpb = LANES // S5_GROUP
    KB = G // gpb
    eye = jnp.eye(gpb, dtype=F32)
    blk = lambda w: jnp.einsum('kgpc,gh->kgchp', w.reshape(KB, gpb, P, S5_GROUP), eye).reshape(KB, LANES, gpb * P)
    bw = jnp.concatenate([blk(bb_re), blk(bb_im)], axis=-1).astype(BF16)
    cblk = lambda w: jnp.einsum('kgcp,gh->kgphc', w.reshape(KB, gpb, S5_GROUP, P), eye).reshape(KB, gpb * P, LANES)
    cwr = cblk(c_re.astype(F32)).astype(BF16)
    cwi = cblk(-c_im.astype(F32)).astype(BF16)
    rows = jnp.arange(SUBLANES)
    ars, ais = [], []
    for lv in range(3):
        r, i = power([1 << lv])
        keep = (rows >= (1 << lv))[:, None]
        ars.append(jnp.where(keep, r, 0.0))
        ais.append(jnp.where(keep, i, 0.0))
    pr, pi = power(np.arange(1, SUBLANES + 1))
    return bw, jnp.stack(ars), jnp.stack(ais), pr, pi, cwr, cwi


def _branch_kernel(or_ref, om_ref, os_ref, gr_ref, gm_ref, gs_ref, wr_ref, wm_ref, ws_ref, o_ref):
    br = _dot(or_ref[...].astype(BF16), wr_ref[...])
    bm = _dot(om_ref[...].astype(BF16), wm_ref[...])
    bs = _dot(os_ref[...].astype(BF16), ws_ref[...])
    merged = (jax.nn.sigmoid(gr_ref[...]) * br + jax.nn.sigmoid(gm_ref[...]) * bm
              + jax.nn.sigmoid(gs_ref[...]) * bs)
    o_ref[...] = merged.astype(BF16)


def _branch(o_r, o_m, o_s, z, wr, wm, ws):
    M, W = o_r.shape
    D = wr.shape[1]
    TM, TN = min(M, 512), 1024
    nJ = D // TN
    ospec = pl.BlockSpec((TM, W), lambda i, j: (i, 0))
    gspec = lambda k: pl.BlockSpec((TM, TN), lambda i, j: (i, 8 + k * nJ + j))
    wspec = pl.BlockSpec((W, TN), lambda i, j: (0, j))
    return pl.pallas_call(
        _branch_kernel,
        grid=(M // TM, nJ),
        in_specs=[ospec, ospec, ospec, gspec(0), gspec(1), gspec(2), wspec, wspec, wspec],
        out_specs=pl.BlockSpec((TM, TN), lambda i, j: (i, j)),
        out_shape=jax.ShapeDtypeStruct((M, D), BF16),
        compiler_params=_cparams(("parallel", "parallel")),
        name="branch",
    )(o_r, o_m, o_s, z, z, z, wr, wm, ws)


def _wo_kernel(m_ref, w_ref, x_ref, g_ref, o_ref):
    y = _dot(m_ref[...], w_ref[...])
    o_ref[...] = x_ref[...] + g_ref[...] * y.reshape(o_ref.shape)


def _wo(merged, w_o, x, mod, l, moff):
    B, T, D = x.shape
    bb, tt = _tile_cfg(B, T, 512)
    TM, nT = bb * tt, T // tt
    TN = 1024
    xspec = pl.BlockSpec((bb, tt, TN), lambda i, j: (i // nT, i % nT, j))
    return pl.pallas_call(
        _wo_kernel,
        grid=((B // bb) * nT, D // TN),
        in_specs=[pl.BlockSpec((TM, D), lambda i, j: (i, 0)),
                  pl.BlockSpec((D, TN), lambda i, j: (0, j)),
                  xspec,
                  pl.BlockSpec((None, bb, 1, TN), lambda i, j: (l, moff // bb + i // nT, 0, 2 * (D // TN) + j))],
        out_specs=xspec,
        out_shape=jax.ShapeDtypeStruct((B, T, D), F32),
        compiler_params=_cparams(("parallel", "parallel")),
        name="wo",
    )(merged, w_o, x, mod)


def _peer_q_kernel(x_ref, nw_ref, sh_ref, sc_ref, wq_ref, k1_ref, k2_ref, h_ref, st_ref):
    h = _norm_mod(x_ref[...], nw_ref[...], sc_ref[...], sh_ref[...])
    hb = h.reshape(h_ref.shape).astype(BF16)
    h_ref[...] = hb
    q = _dot(hb, wq_ref[...])
    dq = k1_ref.shape[-1]
    for hd in range(PEER_HEADS):
        q1 = q[:, (2 * hd) * dq:(2 * hd + 1) * dq]
        q2 = q[:, (2 * hd + 1) * dq:(2 * hd + 2) * dq]
        st_ref[2 * hd] = _dot_nt(k1_ref[hd], q1, precision=lax.Precision.HIGHEST)
        st_ref[2 * hd + 1] = _dot_nt(k2_ref[hd], q2, precision=lax.Precision.HIGHEST)


def _peer_q(x, mod, l, moff, nw, wq, k1, k2):
    B, T, D = x.shape
    bb, tt = _tile_cfg(B, T, 256)
    TM, nT = bb * tt, T // tt
    full = lambda a: pl.BlockSpec(a.shape, lambda i: (0,) * a.ndim)
    mmap = lambda k: (lambda i: (l, moff // bb + i // nT, 0, k))
    return pl.pallas_call(
        _peer_q_kernel,
        grid=((B // bb) * nT,),
        in_specs=[pl.BlockSpec((bb, tt, D), lambda i: (i // nT, i % nT, 0)),
                  pl.BlockSpec((1, D), lambda i: (0, 0)),
                  pl.BlockSpec((None, bb, 1, D), mmap(3)),
                  pl.BlockSpec((None, bb, 1, D), mmap(4)),
                  full(wq), full(k1), full(k2)],
        out_specs=[pl.BlockSpec((TM, D), lambda i: (i, 0)),
                   pl.BlockSpec((2 * PEER_HEADS, N_KEYS, TM), lambda i: (0, 0, i))],
        out_shape=[jax.ShapeDtypeStruct((B * T, D), BF16),
                   jax.ShapeDtypeStruct((2 * PEER_HEADS, N_KEYS, B * T), F32)],
        compiler_params=_cparams(("parallel",)),
        name="peer_q",
    )(x, nw, mod, mod, wq, k1, k2)


def _top_values(cur, n):
    vals = []
    for _ in range(n):
        m = jnp.max(cur, axis=0, keepdims=True)
        vals.append(m)
        cur = jnp.where(cur == m, -jnp.inf, cur)
    return vals


def _peer_sel_kernel(st_ref, e1_ref, e2_ref, tau_ref):
    ts = st_ref.shape[-1]
    neg = jnp.full((1, ts), -jnp.inf, F32)
    taus = []
    for hd in range(PEER_HEADS):
        s1 = st_ref[2 * hd]
        s2 = st_ref[2 * hd + 1]
        v1 = _top_values(s1, PEER_TOPK)
        v2 = _top_values(s2, PEER_TOPK)
        rows = [v1[r] + v2[c] for r in range(PEER_TOPK) for c in range(PEER_TOPK // (r + 1))]
        rows += [neg] * (-len(rows) % SUBLANES)
        top = _top_values(jnp.concatenate(rows, axis=0), PEER_TOPK)
        den = jnp.exp(top[0] - top[0])
        for t in top[1:]:
            den = den + jnp.exp(t - top[0])
        e1_ref[hd] = jnp.exp(s1 - v1[0]) / den
        e2_ref[hd] = jnp.exp(s2 - v2[0])
        taus.append(top[PEER_TOPK - 1])
    tau_ref[...] = jnp.concatenate(taus, axis=0)


def _peer_sel(st):
    H2, K, M = st.shape
    TS = min(M, 256)
    hspec = pl.BlockSpec((PEER_HEADS, K, TS), lambda i: (0, 0, i))
    return pl.pallas_call(
        _peer_sel_kernel,
        grid=(M // TS,),
        in_specs=[pl.BlockSpec((H2, K, TS), lambda i: (0, 0, i))],
        out_specs=[hspec, hspec, pl.BlockSpec((PEER_HEADS, TS), lambda i: (0, i))],
        out_shape=[jax.ShapeDtypeStruct((PEER_HEADS, K, M), F32),
                   jax.ShapeDtypeStruct((PEER_HEADS, K, M), F32),
                   jax.ShapeDtypeStruct((PEER_HEADS, M), F32)],
        compiler_params=_cparams(("parallel",)),
        name="peer_sel",
    )(st)


def _peer_exp_kernel(h_ref, u_ref, v_ref, st_ref, e1_ref, e2_ref, tau_ref, x_ref, g_ref, o_ref,
                     acc_scr, zt_scr, *, slabs):
    kc = pl.program_id(1)

    @pl.when(kc == 0)
    def _():
        acc_scr[...] = jnp.zeros(acc_scr.shape, F32)

    sc = _dot_nt(u_ref[...], h_ref[...])
    for sl in range(slabs):
        i1 = kc * slabs + sl
        w = jnp.zeros((N_KEYS, sc.shape[1]), F32)
        for hd in range(PEER_HEADS):
            t = st_ref[2 * hd, pl.ds(i1, 1), :] + st_ref[2 * hd + 1]
            wgt = e1_ref[hd, pl.ds(i1, 1), :] * e2_ref[hd]
            w = w + jnp.where(t >= tau_ref[pl.ds(hd, 1), :], wgt, 0.0)
        rows = slice(sl * N_KEYS, (sl + 1) * N_KEYS)
        zt_scr[rows, :] = (jax.nn.gelu(sc[rows, :]) * w).astype(BF16)
    acc_scr[...] += _dot_tn(zt_scr[...], v_ref[...])

    @pl.when(kc == pl.num_programs(1) - 1)
    def _():
        o_ref[...] = x_ref[...] + g_ref[...] * acc_scr[...].reshape(o_ref.shape)


def _peer_exp(hb, u, v, st, e1, e2, tau, x, mod, l, moff):
    B, T, D = x.shape
    bb, tt = _tile_cfg(B, T, 256)
    TB, nT = bb * tt, T // tt
    NE = u.shape[0]
    MC = 1024
    xspec = pl.BlockSpec((bb, tt, D), lambda i, k: (i // nT, i % nT, 0))
    hspec = pl.BlockSpec((PEER_HEADS, N_KEYS, TB), lambda i, k: (0, 0, i))
    return pl.pallas_call(
        functools.partial(_peer_exp_kernel, slabs=MC // N_KEYS),
        grid=((B // bb) * nT, NE // MC),
        in_specs=[pl.BlockSpec((TB, D), lambda i, k: (i, 0)),
                  pl.BlockSpec((MC, D), lambda i, k: (k, 0)),
                  pl.BlockSpec((MC, D), lambda i, k: (k, 0)),
                  pl.BlockSpec((2 * PEER_HEADS, N_KEYS, TB), lambda i, k: (0, 0, i)),
                  hspec, hspec,
                  pl.BlockSpec((PEER_HEADS, TB), lambda i, k: (0, i)),
                  xspec,
                  pl.BlockSpec((None, bb, 1, D), lambda i, k: (l, moff // bb + i // nT, 0, 5))],
        out_specs=xspec,
        out_shape=jax.ShapeDtypeStruct((B, T, D), F32),
        scratch_shapes=[pltpu.VMEM((TB, D), F32), pltpu.VMEM((MC, TB), BF16)],
        compiler_params=_cparams(("parallel", "arbitrary")),
        name="peer_exp",
    )(hb, u, v, st, e1, e2, tau, x, mod)


def _final_kernel(x_ref, w_ref, o_ref):
    x = x_ref[...]
    ms = jnp.mean(x * x, axis=-1, keepdims=True)
    o_ref[...] = x * lax.rsqrt(ms + EPS) * w_ref[...]


def _final_norm(x, w):
    B, T, D = x.shape
    bb, tt = _tile_cfg(B, T, 512)
    nT = T // tt
    xspec = pl.BlockSpec((bb, tt, D), lambda i: (i // nT, i % nT, 0))
    return pl.pallas_call(
        _final_kernel,
        grid=((B // bb) * nT,),
        in_specs=[xspec, pl.BlockSpec((1, D), lambda i: (0, 0))],
        out_specs=xspec,
        out_shape=jax.ShapeDtypeStruct((B, T, D), F32),
        compiler_params=_cparams(("parallel",)),
        name="final_norm",
    )(x, w)


def _trunk(x, mod, moff, pos0, states, lw, final_w):
    B, T, D = x.shape
    rtabs = _ret_tables(T, pos0)
    new = []
    for l, (st, w) in enumerate(zip(states, lw)):
        s_ret, s_c, s_n, s_m, s_conv, s_hr, s_hi = st
        z, zg = _in_proj(x, mod, l, moff, w['n1'], w['w_main'], w['w_gate'])
        o_r, n_ret = _retention(z, B, T, s_ret, rtabs, w['ret_gn'])
        o_m, n_c, n_n, n_m, n_conv = _mlstm(z, zg, B, T, (s_c, s_n, s_m, s_conv), w['conv_w'], w['conv_b'],
                                            w['wqk'], w['gate_b'], w['m_gn'], w['m_skip'])
        o_s, n_hr, n_hi = _s5(z, B, T, s_hr, s_hi, w['s5_tabs'], w['s5_d'], w['s5_wglu'], w['s5_bglu'])
        merged = _branch(o_r, o_m, o_s, z, w['w_ret_out'], w['w_mlstm_out'], w['w_s5_out'])
        x = _wo(merged, w['w_o'], x, mod, l, moff)
        hb, sc = _peer_q(x, mod, l, moff, w['n2'], w['peer_wq'], w['peer_k1'], w['peer_k2'])
        e1, e2, tau = _peer_sel(sc)
        x = _peer_exp(hb, w['peer_u'], w['peer_v'], sc, e1, e2, tau, x, mod, l, moff)
        new.append((n_ret, n_c, n_n, n_m[..., 0], n_conv, n_hr, n_hi))
    y = _final_norm(x, final_w)
    return y, [jnp.stack([s[i] for s in new]) for i in range(7)]


def kernel(x_prompt, x_sample, state_ret, state_mlstm_c, state_mlstm_n, state_mlstm_m, state_mlstm_conv,
           state_s5_re, state_s5_im, c_prompt, c_sample, ada_w, ada_b, norm1_w, norm2_w, final_norm_w,
           w_in, ret_gn_w, w_ret_out, mlstm_conv_w, mlstm_conv_b, mlstm_wq, mlstm_wk, mlstm_b_i, mlstm_b_f,
           mlstm_gn_w, mlstm_skip, w_mlstm_out, s5_a_re, s5_a_im, s5_log_dt, s5_b_re, s5_b_im, s5_c_re,
           s5_c_im, s5_d, s5_w_glu, s5_b_glu, w_s5_out, w_o, peer_wq, peer_k1, peer_k2, peer_u, peer_v):
    depth = w_in.shape[0]
    Bp, Tp, D = x_prompt.shape
    Bs, Ts, _ = x_sample.shape
    W = HEADS * DH
    G, P = s5_a_re.shape[1:]
    NS = G * P
    past_len = 16384

    pad = -(Bs + Bp) % SUBLANES
    c_all = jnp.concatenate([c_sample, c_prompt, jnp.zeros((pad, D), F32)], axis=0)
    mod = _ada(c_all, ada_w, ada_b)
    mod = mod.reshape(depth, c_all.shape[0], 1, ada_w.shape[2])

    a_end = 7 * W
    g_end = a_end + 2 * HEADS
    lw = []
    for l in range(depth):
        w_main = jnp.concatenate([w_in[l, :, :a_end], w_in[l, :, g_end:]], axis=1).astype(BF16)
        w_gate = jnp.pad(w_in[l, :, a_end:g_end], ((0, 0), (0, LANES - 2 * HEADS))).astype(BF16)
        gate_b = jnp.pad(jnp.concatenate([mlstm_b_i[l], mlstm_b_f[l]]), (0, LANES - 2 * HEADS)).reshape(1, LANES)
        lw.append(dict(
            n1=norm1_w[l].reshape(1, D), n2=norm2_w[l].reshape(1, D),
            w_main=w_main, w_gate=w_gate, gate_b=gate_b,
            ret_gn=ret_gn_w[l].reshape(1, W),
            conv_w=mlstm_conv_w[l], conv_b=mlstm_conv_b[l].reshape(1, W),
            wqk=jnp.concatenate([mlstm_wq[l], mlstm_wk[l]], axis=-1).astype(BF16),
            m_gn=mlstm_gn_w[l].reshape(1, W), m_skip=mlstm_skip[l].reshape(1, W),
            s5_tabs=_s5_tables(s5_a_re[l], s5_a_im[l], s5_log_dt[l], s5_b_re[l], s5_b_im[l], s5_c_re[l], s5_c_im[l]),
            s5_d=s5_d[l].reshape(1, -1), s5_wglu=s5_w_glu[l].astype(BF16), s5_bglu=s5_b_glu[l].reshape(1, -1),
            w_ret_out=w_ret_out[l].astype(BF16), w_mlstm_out=w_mlstm_out[l].astype(BF16),
            w_s5_out=w_s5_out[l].astype(BF16), w_o=w_o[l].astype(BF16),
            peer_wq=peer_wq[l].astype(BF16), peer_k1=peer_k1[l], peer_k2=peer_k2[l],
            peer_u=peer_u[l].astype(BF16), peer_v=peer_v[l].astype(BF16)))

    def zero_states(B):
        return (jnp.zeros((B, HEADS, DH, DH), F32), jnp.zeros((B, HEADS, DH, DH), F32),
                jnp.zeros((B, HEADS, DH), F32), jnp.zeros((B, HEADS, LANES), F32),
                jnp.zeros((B, CONV_W - 1, W), F32), jnp.zeros((B, 1, NS), F32), jnp.zeros((B, 1, NS), F32))

    prompt_states = [zero_states(Bp) for _ in range(depth)]
    sample_states = [(state_ret[l], state_mlstm_c[l], state_mlstm_n[l],
                      jnp.broadcast_to(state_mlstm_m[l][..., None], (Bs, HEADS, LANES)),
                      state_mlstm_conv[l], state_s5_re[l].reshape(Bs, 1, NS), state_s5_im[l].reshape(Bs, 1, NS))
                     for l in range(depth)]

    y_p, ps = _trunk(x_prompt, mod, Bs, 0, prompt_states, lw, final_norm_w.reshape(1, D))
    y_s, ss = _trunk(x_sample, mod, 0, past_len, sample_states, lw, final_norm_w.reshape(1, D))

    def unpack(st, B):
        r, c, n, m, conv, hr, hi = st
        return (r, c, n, m, conv, hr.reshape(depth, B, G, P), hi.reshape(depth, B, G, P))

    return (y_p, y_s) + unpack(ps, Bp) + unpack(ss, Bs)
```

```python
import functools
import math

import jax
import jax.numpy as jnp
import numpy as np
from jax import lax
from jax.experimental import pallas as pl
from jax.experimental.pallas import tpu as pltpu

F32 = jnp.float32
BF16 = jnp.bfloat16

EPS = 1e-6
ROPE_BASE = 10000.0
HEADS = 8
DH = 128
CONV_W = 4
S5_GROUP = 16
S5_STATE = 64
PEER_HEADS = 8
PEER_TOPK = 16
N_KEYS = 128
CHUNK = 128
LANES = 128
SUBLANES = 8
VMEM_LIMIT = 56 * 1024 * 1024
PEER_TB = 512
PEER_MC = 512
PEER_KQ = 4


def _cparams(sem):
    return pltpu.CompilerParams(dimension_semantics=sem, vmem_limit_bytes=VMEM_LIMIT)


def _tile_cfg(B, T, target):
    if T >= target:
        bb, tt = 1, target
    else:
        bb, tt = min(B, target // T), T
    assert T % tt == 0 and B % bb == 0
    return bb, tt


def _silu(x):
    return x * jax.nn.sigmoid(x)


def _norm_mod(x, w, sc, sh):
    ms = jnp.mean(x * x, axis=-1, keepdims=True)
    y = x * lax.rsqrt(ms + EPS) * w
    return y * (1.0 + sc) + sh


def _head_norm(x):
    mu = jnp.mean(x, axis=-1, keepdims=True)
    xc = x - mu
    var = jnp.mean(xc * xc, axis=-1, keepdims=True)
    return xc * lax.rsqrt(var + EPS)


def _dot(a, b):
    return jnp.dot(a, b, preferred_element_type=F32)


def _dot_nt(a, b, **kw):
    return lax.dot_general(a, b, (((1,), (1,)), ((), ())), preferred_element_type=F32, **kw)


def _dot_tn(a, b):
    return lax.dot_general(a, b, (((0,), (0,)), ((), ())), preferred_element_type=F32)


def _rows(start, n):
    if isinstance(start, int):
        return pl.ds(start, n)
    return pl.ds(pl.multiple_of(start, SUBLANES), n)


def _for_each_seq(bb, body, unroll=1):
    if bb == 1:
        body(0)
    else:
        def f(bi, c):
            body(bi)
            return c
        lax.fori_loop(0, bb, f, 0, unroll=unroll)


def _ada_kernel(c_ref, w_ref, b_ref, o_ref):
    a = _silu(c_ref[...]).astype(BF16)
    o_ref[...] = _dot(a, w_ref[...].astype(BF16)) + b_ref[...]


def _ada(c_all, ada_w, ada_b):
    L, D, N = ada_w.shape
    Bc = c_all.shape[0]
    TN = 1024
    return pl.pallas_call(
        _ada_kernel,
        grid=(L, N // TN),
        in_specs=[pl.BlockSpec((Bc, D), lambda l, j: (0, 0)),
                  pl.BlockSpec((None, D, TN), lambda l, j: (l, 0, j)),
                  pl.BlockSpec((None, 1, TN), lambda l, j: (l, 0, j))],
        out_specs=pl.BlockSpec((None, Bc, TN), lambda l, j: (l, 0, j)),
        out_shape=jax.ShapeDtypeStruct((L, Bc, N), F32),
        compiler_params=_cparams(("parallel", "parallel")),
        name="ada",
    )(c_all, ada_w, ada_b.reshape(L, 1, N))


def _in_kernel(x_ref, nw_ref, sh_ref, sc_ref, wm_ref, wg_ref, z_ref, zg_ref, h_scr):
    @pl.when(pl.program_id(1) == 0)
    def _():
        h = _norm_mod(x_ref[...], nw_ref[...], sc_ref[...], sh_ref[...])
        hb = h.reshape(h_scr.shape).astype(BF16)
        h_scr[...] = hb
        zg_ref[...] = _dot(hb, wg_ref[...])
    z_ref[...] = _dot(h_scr[...], wm_ref[...])


def _in_proj(x, mod, l, moff, nw, wm, wg):
    B, T, D = x.shape
    bb, tt = _tile_cfg(B, T, 512)
    TM, nT = bb * tt, T // tt
    nI = (B // bb) * nT
    N = wm.shape[1]
    TN = 1024
    xmap = lambda i, j: (i // nT, i % nT, 0)
    mmap = lambda k: (lambda i, j: (l, moff // bb + i // nT, 0, k))
    return pl.pallas_call(
        _in_kernel,
        grid=(nI, N // TN),
        in_specs=[pl.BlockSpec((bb, tt, D), xmap),
                  pl.BlockSpec((1, D), lambda i, j: (0, 0)),
                  pl.BlockSpec((None, bb, 1, D), mmap(0)),
                  pl.BlockSpec((None, bb, 1, D), mmap(1)),
                  pl.BlockSpec((D, TN), lambda i, j: (0, j)),
                  pl.BlockSpec((D, LANES), lambda i, j: (0, 0))],
        out_specs=[pl.BlockSpec((TM, TN), lambda i, j: (i, j)),
                   pl.BlockSpec((TM, LANES), lambda i, j: (i, 0))],
        out_shape=[jax.ShapeDtypeStruct((B * T, N), F32),
                   jax.ShapeDtypeStruct((B * T, LANES), F32)],
        scratch_shapes=[pltpu.VMEM((TM, D), BF16)],
        compiler_params=_cparams(("parallel", "arbitrary")),
        name="in_proj",
    )(x, nw, mod, mod, wm, wg)


def _ret_kernel(zq_ref, zk_ref, zv_ref, zg_ref, cq_ref, sq_ref, ck_ref, sk_ref, intra_ref, cross_ref,
                kdec_ref, cdec_ref, gn_ref, s0_ref, o_ref, s_ref, *, bb, L):
    @pl.when(pl.program_id(1) == 0)
    def _():
        s_ref[...] = s0_ref[...]

    cq, sq, ck, sk = cq_ref[...], sq_ref[...], ck_ref[...], sk_ref[...]

    def seq(bi):
        rows = _rows(bi * L, L)
        for h in range(HEADS):
            cols = slice(h * DH, (h + 1) * DH)
            q = zq_ref[rows, cols]
            k = zk_ref[rows, cols]
            v = zv_ref[rows, cols].astype(BF16)
            g = zg_ref[rows, cols]
            q = (q * cq + pltpu.roll(q, DH // 2, 1) * sq).astype(BF16)
            k = k * ck + pltpu.roll(k, DH // 2, 1) * sk
            s = s_ref[bi, h]
            att = _dot_nt(q, k.astype(BF16)) * intra_ref[h]
            o = _dot(att.astype(BF16), v) + _dot(q, s.astype(BF16)) * cross_ref[h]
            s_ref[bi, h] = cdec_ref[h] * s + _dot_tn((k * kdec_ref[h]).astype(BF16), v)
            o_ref[rows, cols] = _head_norm(o) * gn_ref[:, cols] * _silu(g)

    _for_each_seq(bb, seq, unroll=2)


def _retention(z, B, T, s0, tabs, gn_w):
    L = CHUNK if T % CHUNK == 0 else T
    nC = T // L
    bb = 1 if nC > 1 else min(B, 8)
    R = bb * L
    W = HEADS * DH
    cq, sq, ck, sk, intra, cross, kdec, cdec = tabs
    zspec = lambda k: pl.BlockSpec((R, W), lambda i, c: (i * nC + c, k))
    tspec = pl.BlockSpec((L, DH), lambda i, c: (c, 0))
    full = lambda a: pl.BlockSpec(a.shape, lambda i, c: (0,) * a.ndim)
    sspec = pl.BlockSpec((bb, HEADS, DH, DH), lambda i, c: (i, 0, 0, 0))
    return pl.pallas_call(
        functools.partial(_ret_kernel, bb=bb, L=L),
        grid=(B // bb, nC),
        in_specs=[zspec(0), zspec(1), zspec(2), zspec(3), tspec, tspec, tspec, tspec,
                  full(intra), full(cross), full(kdec), full(cdec), full(gn_w), sspec],
        out_specs=[pl.BlockSpec((R, W), lambda i, c: (i * nC + c, 0)), sspec],
        out_shape=[jax.ShapeDtypeStruct((B * T, W), F32),
                   jax.ShapeDtypeStruct((B, HEADS, DH, DH), F32)],
        compiler_params=_cparams(("parallel", "arbitrary")),
        name="ret",
    )(z, z, z, z, cq, sq, ck, sk, intra, cross, kdec, cdec, gn_w, s0)


def _ret_tables(T, pos0):
    L = CHUNK if T % CHUNK == 0 else T
    half = DH // 2
    inv = jnp.exp(-math.log(ROPE_BASE) * jnp.arange(half, dtype=F32) / half)
    pos = jnp.arange(T, dtype=F32) + pos0
    ang = pos[:, None] * inv[None, :]
    cos, sin = jnp.cos(ang), jnp.sin(ang)
    c = jnp.concatenate([cos, cos], axis=-1)
    s = jnp.concatenate([-sin, sin], axis=-1)
    kscale = DH ** -0.5
    lg = jnp.log1p(-jnp.exp2(-5.0 - jnp.arange(HEADS, dtype=F32)))
    j = jnp.arange(L, dtype=F32)
    diff = j[:, None] - j[None, :]
    intra = jnp.where(diff >= 0, jnp.exp(lg[:, None, None] * jnp.maximum(diff, 0.0)), 0.0)
    cross = jnp.exp(lg[:, None] * (j + 1.0))
    kdec = jnp.exp(lg[:, None] * (L - 1.0 - j))
    cdec = jnp.exp(lg * L)
    bl = lambda a: jnp.broadcast_to(a[..., None], a.shape + (DH,))
    return (c, s, c * kscale, s * kscale, intra, bl(cross), bl(kdec), bl(cdec[:, None]))


def _mlstm_kernel(zu_ref, zv_ref, zo_ref, zg_ref, cw_ref, cb_ref, wqk_ref, gb_ref, gn_ref, sk_ref,
                  c0_ref, n0_ref, m0_ref, conv0_ref,
                  o_ref, c_ref, n_ref, m_ref, conv_ref, xp_scr, tail_scr, ca_scr, *, bb, L, last):
    ci = pl.program_id(1)

    @pl.when(ci == 0)
    def _():
        c_ref[...] = c0_ref[...]
        n_ref[...] = n0_ref[...]
        m_ref[...] = m0_ref[...]
        tail_scr[...] = jnp.zeros(tail_scr.shape, F32)
        tail_scr[:, SUBLANES - (CONV_W - 1):, :] = conv0_ref[...]

    ri = lax.broadcasted_iota(jnp.int32, (L, L), 0)
    rj = lax.broadcasted_iota(jnp.int32, (L, L), 1)
    causal = rj <= ri
    tri = causal.astype(F32)
    lane = lax.broadcasted_iota(jnp.int32, (L, LANES), 1)
    kscale = DH ** -0.5

    def seq(bi):
        rows = _rows(bi * L, L)
        xp_scr[bi, 0:SUBLANES, :] = tail_scr[bi]
        xp_scr[bi, SUBLANES:, :] = zu_ref[rows, :]
        cu = cb_ref[...]
        for i in range(CONV_W):
            cu = cu + xp_scr[bi, pl.ds(SUBLANES - (CONV_W - 1) + i, L), :] * cw_ref[pl.ds(i, 1), :]
        tail_scr[bi] = xp_scr[bi, L:L + SUBLANES, :]
        ca_scr[bi] = _silu(cu)
        gz = zg_ref[rows, :] + gb_ref[...]
        gates = jnp.where(lane < HEADS, gz, jax.nn.log_sigmoid(gz))
        csum = jnp.dot(tri, gates, preferred_element_type=F32, precision=lax.Precision.HIGHEST)
        gates_t = gates.T
        csum_t = csum.T
        m_all = m_ref[bi]
        n_all = n_ref[bi]
        n_rows, m_rows = [], []
        for h in range(HEADS):
            cols = slice(h * DH, (h + 1) * DH)
            ca = ca_scr[bi, :, cols]
            qk = _dot(ca.astype(BF16), wqk_ref[h])
            q = qk[:, :DH]
            k = qk[:, DH:] * kscale
            qb = q.astype(BF16)
            v = zv_ref[rows, cols].astype(BF16)
            i_col = gates[:, h:h + 1]
            b_col = csum[:, HEADS + h:HEADS + h + 1]
            i_row = gates_t[h:h + 1, :]
            b_row = csum_t[HEADS + h:HEADS + h + 1, :]
            m_prev = m_all[h:h + 1, :1]
            c_prev = c_ref[bi, h]
            n_prev = n_all[h:h + 1, :]
            dlog = jnp.where(causal, b_col - b_row + i_row, -jnp.inf)
            inter = b_col + m_prev
            mt = jnp.maximum(inter, jnp.max(dlog, axis=-1, keepdims=True))
            s = _dot_nt(qb, k.astype(BF16)) * jnp.exp(dlog - mt)
            w_int = jnp.exp(inter - mt)
            num = _dot(s.astype(BF16), v) + w_int * _dot(qb, c_prev.astype(BF16))
            den = jnp.sum(s, axis=-1, keepdims=True) + w_int * jnp.sum(q * n_prev, axis=-1, keepdims=True)
            hh = num / jnp.maximum(jnp.abs(den), jnp.exp(-mt))
            m_new = mt[L - 1:L, :]
            b_last = b_col[L - 1:L, :]
            tail = jnp.exp(b_last - b_col + i_col - m_new)
            dec = jnp.exp(b_last + m_prev - m_new)
            kt = k * tail
            c_ref[bi, h] = dec * c_prev + _dot_tn(kt.astype(BF16), v)
            n_rows.append(dec * n_prev + jnp.sum(kt, axis=0, keepdims=True))
            m_rows.append(jnp.broadcast_to(m_new, (1, LANES)))
            hm = jax.nn.sigmoid(zo_ref[rows, cols]) * hh
            o_ref[rows, cols] = _head_norm(hm) * gn_ref[:, cols] + sk_ref[:, cols] * ca
        n_ref[bi] = jnp.concatenate(n_rows, axis=0)
        m_ref[bi] = jnp.concatenate(m_rows, axis=0)

    _for_each_seq(bb, seq, unroll=2)

    @pl.when(ci == last)
    def _():
        conv_ref[...] = tail_scr[:, SUBLANES - (CONV_W - 1):, :]


def _mlstm(z, zg, B, T, states, cw, cb, wqk, gb, gn_w, skip):
    L = CHUNK if T % CHUNK == 0 else T
    nC = T // L
    bb = 1 if nC > 1 else min(B, 8)
    R = bb * L
    W = HEADS * DH
    c0, n0, m0, conv0 = states
    zspec = lambda k: pl.BlockSpec((R, W), lambda i, c: (i * nC + c, k))
    full = lambda a: pl.BlockSpec(a.shape, lambda i, c: (0,) * a.ndim)
    cspec = pl.BlockSpec((bb, HEADS, DH, DH), lambda i, c: (i, 0, 0, 0))
    nspec = pl.BlockSpec((bb, HEADS, DH), lambda i, c: (i, 0, 0))
    vspec = pl.BlockSpec((bb, CONV_W - 1, W), lambda i, c: (i, 0, 0))
    return pl.pallas_call(
        functools.partial(_mlstm_kernel, bb=bb, L=L, last=nC - 1),
        grid=(B // bb, nC),
        in_specs=[zspec(4), zspec(5), zspec(6), pl.BlockSpec((R, LANES), lambda i, c: (i * nC + c, 0)),
                  full(cw), full(cb), full(wqk), full(gb), full(gn_w), full(skip),
                  cspec, nspec, nspec, vspec],
        out_specs=[pl.BlockSpec((R, W), lambda i, c: (i * nC + c, 0)), cspec, nspec, nspec, vspec],
        out_shape=[jax.ShapeDtypeStruct((B * T, W), F32),
                   jax.ShapeDtypeStruct((B, HEADS, DH, DH), F32),
                   jax.ShapeDtypeStruct((B, HEADS, DH), F32),
                   jax.ShapeDtypeStruct((B, HEADS, LANES), F32),
                   jax.ShapeDtypeStruct((B, CONV_W - 1, W), F32)],
        scratch_shapes=[pltpu.VMEM((bb, L + SUBLANES, W), F32),
                        pltpu.VMEM((bb, SUBLANES, W), F32),
                        pltpu.VMEM((bb, L, W), F32)],
        compiler_params=_cparams(("parallel", "arbitrary")),
        name="mlstm",
    )(z, z, z, zg, cw, cb, wqk, gb, gn_w, skip, c0, n0, m0, conv0)


def _s5_kernel(zu_ref, bw_ref, ar_ref, ai_ref, pr_ref, pi_ref, cwr_ref, cwi_ref, d_ref, wglu_ref, bglu_ref,
               h0r_ref, h0i_ref, o_ref, hr_ref, hi_ref, xr_scr, xi_scr, *, bb, Lc):
    @pl.when(pl.program_id(1) == 0)
    def _():
        hr_ref[...] = h0r_ref[...]
        hi_ref[...] = h0i_ref[...]

    NS = xr_scr.shape[1]
    KB = bw_ref.shape[0]
    SB = NS // KB
    u = zu_ref[...]
    ub = u.astype(BF16)
    for kb in range(KB):
        r = _dot(ub[:, kb * LANES:(kb + 1) * LANES], bw_ref[kb])
        xr_scr[:, kb * SB:(kb + 1) * SB] = r[:, :SB]
        xi_scr[:, kb * SB:(kb + 1) * SB] = r[:, SB:]

    CW = 512
    ngrp = Lc // SUBLANES

    def seq(bi):
        def grp(gi, carry):
            rows = _rows(bi * Lc + gi * SUBLANES, SUBLANES)
            for cc in range(NS // CW):
                cols = slice(cc * CW, (cc + 1) * CW)
                xr = xr_scr[rows, cols]
                xi = xi_scr[rows, cols]
                for lv in range(3):
                    rr = pltpu.roll(xr, 1 << lv, 0)
                    ri = pltpu.roll(xi, 1 << lv, 0)
                    a_r = ar_ref[lv, :, cols]
                    a_i = ai_ref[lv, :, cols]
                    xr, xi = xr + a_r * rr - a_i * ri, xi + a_r * ri + a_i * rr
                cr = jnp.broadcast_to(hr_ref[bi, :, cols], (SUBLANES, CW))
                cim = jnp.broadcast_to(hi_ref[bi, :, cols], (SUBLANES, CW))
                p_r = pr_ref[:, cols]
                p_i = pi_ref[:, cols]
                hr = xr + p_r * cr - p_i * cim
                hi = xi + p_r * cim + p_i * cr
                xr_scr[rows, cols] = hr
                xi_scr[rows, cols] = hi
                hr_ref[bi, :, cols] = hr[SUBLANES - 1:, :]
                hi_ref[bi, :, cols] = hi[SUBLANES - 1:, :]
            return carry
        lax.fori_loop(0, ngrp, grp, 0)

    _for_each_seq(bb, seq)

    hrb = xr_scr[...].astype(BF16)
    hib = xi_scr[...].astype(BF16)
    ys = []
    for kb in range(KB):
        cols = slice(kb * SB, (kb + 1) * SB)
        ys.append(_dot(hrb[:, cols], cwr_ref[kb]) + _dot(hib[:, cols], cwi_ref[kb]))
    y = jnp.concatenate(ys, axis=1) + d_ref[...] * u
    g = jax.nn.gelu(y)
    o_ref[...] = g * jax.nn.sigmoid(_dot(g.astype(BF16), wglu_ref[...]) + bglu_ref[...])


def _s5(z, B, T, h0r, h0i, tabs, d, wglu, bglu):
    bw, ar, ai, pr, pi, cwr, cwi = tabs
    Lc = min(T, 256)
    nC = T // Lc
    bb = 1 if nC > 1 else min(B, max(1, 128 // T))
    R = bb * Lc
    W = d.shape[1]
    NS = h0r.shape[-1]
    full = lambda a: pl.BlockSpec(a.shape, lambda i, c: (0,) * a.ndim)
    hspec = pl.BlockSpec((bb, 1, NS), lambda i, c: (i, 0, 0))
    return pl.pallas_call(
        functools.partial(_s5_kernel, bb=bb, Lc=Lc),
        grid=(B // bb, nC),
        in_specs=[pl.BlockSpec((R, W), lambda i, c: (i * nC + c, 7)),
                  full(bw), full(ar), full(ai), full(pr), full(pi), full(cwr), full(cwi),
                  full(d), full(wglu), full(bglu), hspec, hspec],
        out_specs=[pl.BlockSpec((R, W), lambda i, c: (i * nC + c, 0)), hspec, hspec],
        out_shape=[jax.ShapeDtypeStruct((B * T, W), F32),
                   jax.ShapeDtypeStruct((B, 1, NS), F32),
                   jax.ShapeDtypeStruct((B, 1, NS), F32)],
        scratch_shapes=[pltpu.VMEM((R, NS), F32), pltpu.VMEM((R, NS), F32)],
        compiler_params=_cparams(("parallel", "arbitrary")),
        name="s5",
    )(z, bw, ar, ai, pr, pi, cwr, cwi, d, wglu, bglu, h0r, h0i)


def _s5_tables(a_re, a_im, log_dt, b_re, b_im, c_re, c_im):
    G, P = a_re.shape
    dt = jnp.exp(log_dt.astype(F32))[:, None]
    lam_re = -jnp.abs(a_re.astype(F32))
    lam_im = a_im.astype(F32)

    def power(n):
        n = jnp.asarray(n, F32)[:, None, None]
        mag = jnp.exp(lam_re * dt * n)
        return ((mag * jnp.cos(lam_im * dt * n)).reshape(-1, G * P),
                (mag * jnp.sin(lam_im * dt * n)).reshape(-1, G * P))

    mag = jnp.exp(lam_re * dt)
    ab_re = mag * jnp.cos(lam_im * dt)
    ab_im = mag * jnp.sin(lam_im * dt)
    den = lam_re * lam_re + lam_im * lam_im
    co_re = ((ab_re - 1.0) * lam_re + ab_im * lam_im) / den
    co_im = (ab_im * lam_re - (ab_re - 1.0) * lam_im) / den
    br, bi = b_re.astype(F32), b_im.astype(F32)
    bb_re = co_re[..., None] * br - co_im[..., None] * bi
    bb_im = co_re[..., None] * bi + co_im[..., None] * br
    gpb = LANES // S5_GROUP
    KB = G // gpb
    eye = jnp.eye(gpb, dtype=F32)
    blk = lambda w: jnp.einsum('kgpc,gh->kgchp', w.reshape(KB, gpb, P, S5_GROUP), eye).reshape(KB, LANES, gpb * P)
    bw = jnp.concatenate([blk(bb_re), blk(bb_im)], axis=-1).astype(BF16)
    cblk = lambda w: jnp.einsum('kgcp,gh->kgphc', w.reshape(KB, gpb, S5_GROUP, P), eye).reshape(KB, gpb * P, LANES)
    cwr = cblk(c_re.astype(F32)).astype(BF16)
    cwi = cblk(-c_im.astype(F32)).astype(BF16)
    rows = jnp.arange(SUBLANES)
    ars, ais = [], []
    for lv in range(3):
        r, i = power([1 << lv])
        keep = (rows >= (1 << lv))[:, None]
        ars.append(jnp.where(keep, r, 0.0))
        ais.append(jnp.where(keep, i, 0.0))
    pr, pi = power(np.arange(1, SUBLANES + 1))
    return bw, jnp.stack(ars), jnp.stack(ais), pr, pi, cwr, cwi


def _branch_kernel(or_ref, om_ref, os_ref, gr_ref, gm_ref, gs_ref, wr_ref, wm_ref, ws_ref, o_ref):
    br = _dot(or_ref[...].astype(BF16), wr_ref[...])
    bm = _dot(om_ref[...].astype(BF16), wm_ref[...])
    bs = _dot(os_ref[...].astype(BF16), ws_ref[...])
    merged = (jax.nn.sigmoid(gr_ref[...]) * br + jax.nn.sigmoid(gm_ref[...]) * bm
              + jax.nn.sigmoid(gs_ref[...]) * bs)
    o_ref[...] = merged.astype(BF16)


def _branch(o_r, o_m, o_s, z, wr, wm, ws):
    M, W = o_r.shape
    D = wr.shape[1]
    TM, TN = min(M, 512), 1024
    nJ = D // TN
    ospec = pl.BlockSpec((TM, W), lambda i, j: (i, 0))
    gspec = lambda k: pl.BlockSpec((TM, TN), lambda i, j: (i, 8 + k * nJ + j))
    wspec = pl.BlockSpec((W, TN), lambda i, j: (0, j))
    return pl.pallas_call(
        _branch_kernel,
        grid=(M // TM, nJ),
        in_specs=[ospec, ospec, ospec, gspec(0), gspec(1), gspec(2), wspec, wspec, wspec],
        out_specs=pl.BlockSpec((TM, TN), lambda i, j: (i, j)),
        out_shape=jax.ShapeDtypeStruct((M, D), BF16),
        compiler_params=_cparams(("parallel", "parallel")),
        name="branch",
    )(o_r, o_m, o_s, z, z, z, wr, wm, ws)


def _wo_kernel(m_ref, w_ref, x_ref, g_ref, o_ref):
    y = _dot(m_ref[...], w_ref[...])
    o_ref[...] = x_ref[...] + g_ref[...] * y.reshape(o_ref.shape)


def _wo(merged, w_o, x, mod, l, moff):
    B, T, D = x.shape
    bb, tt = _tile_cfg(B, T, 512)
    TM, nT = bb * tt, T // tt
    TN = 1024
    xspec = pl.BlockSpec((bb, tt, TN), lambda i, j: (i // nT, i % nT, j))
    return pl.pallas_call(
        _wo_kernel,
        grid=((B // bb) * nT, D // TN),
        in_specs=[pl.BlockSpec((TM, D), lambda i, j: (i, 0)),
                  pl.BlockSpec((D, TN), lambda i, j: (0, j)),
                  xspec,
                  pl.BlockSpec((None, bb, 1, TN), lambda i, j: (l, moff // bb + i // nT, 0, 2 * (D // TN) + j))],
        out_specs=xspec,
        out_shape=jax.ShapeDtypeStruct((B, T, D), F32),
        compiler_params=_cparams(("parallel", "parallel")),
        name="wo",
    )(merged, w_o, x, mod)


def _peer_q_kernel(x_ref, nw_ref, sh_ref, sc_ref, wq_ref, k1_ref, k2_ref, h_ref, st_ref):
    h = _norm_mod(x_ref[...], nw_ref[...], sc_ref[...], sh_ref[...])
    hb = h.reshape(-1, h.shape[-1]).astype(BF16)
    kq = h_ref.shape[-1]
    for j in range(h_ref.shape[0]):
        h_ref[j] = hb[:, j * kq:(j + 1) * kq]
    q = _dot(hb, wq_ref[...])
    dq = k1_ref.shape[-1]
    for hd in range(PEER_HEADS):
        q1 = q[:, (2 * hd) * dq:(2 * hd + 1) * dq]
        q2 = q[:, (2 * hd + 1) * dq:(2 * hd + 2) * dq]
        st_ref[2 * hd] = _dot_nt(k1_ref[hd], q1, precision=lax.Precision.HIGHEST)
        st_ref[2 * hd + 1] = _dot_nt(k2_ref[hd], q2, precision=lax.Precision.HIGHEST)


def _peer_q(x, mod, l, moff, nw, wq, k1, k2):
    B, T, D = x.shape
    bb, tt = _tile_cfg(B, T, 256)
    TM, nT = bb * tt, T // tt
    full = lambda a: pl.BlockSpec(a.shape, lambda i: (0,) * a.ndim)
    mmap = lambda k: (lambda i: (l, moff // bb + i // nT, 0, k))
    return pl.pallas_call(
        _peer_q_kernel,
        grid=((B // bb) * nT,),
        in_specs=[pl.BlockSpec((bb, tt, D), lambda i: (i // nT, i % nT, 0)),
                  pl.BlockSpec((1, D), lambda i: (0, 0)),
                  pl.BlockSpec((None, bb, 1, D), mmap(3)),
                  pl.BlockSpec((None, bb, 1, D), mmap(4)),
                  full(wq), full(k1), full(k2)],
        out_specs=[pl.BlockSpec((PEER_KQ, TM, D // PEER_KQ), lambda i: (0, i, 0)),
                   pl.BlockSpec((2 * PEER_HEADS, N_KEYS, TM), lambda i: (0, 0, i))],
        out_shape=[jax.ShapeDtypeStruct((PEER_KQ, B * T, D // PEER_KQ), BF16),
                   jax.ShapeDtypeStruct((2 * PEER_HEADS, N_KEYS, B * T), F32)],
        compiler_params=_cparams(("parallel",)),
        name="peer_q",
    )(x, nw, mod, mod, wq, k1, k2)


def _top_values(cur, n):
    vals = []
    for _ in range(n):
        m = jnp.max(cur, axis=0, keepdims=True)
        vals.append(m)
        cur = jnp.where(cur == m, -jnp.inf, cur)
    return vals


def _peer_sel_kernel(st_ref, e1_ref, e2_ref, th_ref):
    ts = st_ref.shape[-1]
    neg = jnp.full((1, ts), -jnp.inf, F32)
    n = PEER_TOPK + 1
    ths = []
    for hd in range(PEER_HEADS):
        s1 = st_ref[2 * hd]
        s2 = st_ref[2 * hd + 1]
        v1 = _top_values(s1, n)
        v2 = _top_values(s2, n)
        rows = [v1[r] + v2[c] for r in range(n) for c in range(n // (r + 1))]
        rows += [neg] * (-len(rows) % SUBLANES)
        top = _top_values(jnp.concatenate(rows, axis=0), n)
        den = jnp.exp(top[0] - top[0])
        for t in top[1:PEER_TOPK]:
            den = den + jnp.exp(t - top[0])
        e1_ref[hd] = jnp.exp(s1 - v1[0]) / den
        e2_ref[hd] = jnp.exp(s2 - v2[0])
        ths.append(jnp.exp(0.5 * (top[PEER_TOPK - 1] + top[PEER_TOPK]) - top[0]) / den)
    th_ref[...] = jnp.concatenate(ths, axis=0)


def _peer_sel(st):
    H2, K, M = st.shape
    TS = min(M, 256)
    hspec = pl.BlockSpec((PEER_HEADS, K, TS), lambda i: (0, 0, i))
    return pl.pallas_call(
        _peer_sel_kernel,
        grid=(M // TS,),
        in_specs=[pl.BlockSpec((H2, K, TS), lambda i: (0, 0, i))],
        out_specs=[hspec, hspec, pl.BlockSpec((PEER_HEADS, TS), lambda i: (0, i))],
        out_shape=[jax.ShapeDtypeStruct((PEER_HEADS, K, M), F32),
                   jax.ShapeDtypeStruct((PEER_HEADS, K, M), F32),
                   jax.ShapeDtypeStruct((PEER_HEADS, M), F32)],
        compiler_params=_cparams(("parallel",)),
        name="peer_sel",
    )(st)


def _peer_exp_kernel(h_ref, u_ref, vta_ref, vtb_ref, e1_ref, e2_ref, th_ref, x_ref, g_ref, o_ref,
                     acc_scr, sc0, sc1, zt0, zt1, *, mc, nc):
    s = pl.program_id(1)
    slabs = mc // N_KEYS

    @pl.when(s == 0)
    def _():
        acc_scr[...] = jnp.zeros(acc_scr.shape, F32)
        sc1[...] = jnp.zeros(sc1.shape, F32)
        zt0[...] = jnp.zeros(zt0.shape, BF16)
        zt1[...] = jnp.zeros(zt1.shape, BF16)

    nq = u_ref.shape[0]
    assert slabs == nq

    def substep(half, sc_a, sc_b, zt_b, zt_c, chunk_b):
        c = jnp.clip(chunk_b, 0, nc - 1)
        cols = slice(half * mc, (half + 1) * mc)
        sc_a[...] = jnp.zeros(sc_a.shape, F32)

        def piece(j, carry):
            acc_scr[j] += _dot((vta_ref, vtb_ref)[half][j], zt_c[...])
            rs = pl.ds(pl.multiple_of(j * N_KEYS, N_KEYS), N_KEYS)
            i1 = c * slabs + j
            for tc in range(sc_b.shape[1] // LANES):
                tl = slice(tc * LANES, (tc + 1) * LANES)
                w = jnp.zeros((N_KEYS, LANES), F32)
                for hd in range(PEER_HEADS):
                    p = e1_ref[hd, pl.ds(i1, 1), :][:, tl] * e2_ref[hd, :, tl]
                    w = jnp.where(p >= th_ref[pl.ds(hd, 1), tl], w + p, w)
                zt_b[rs, tl] = (jax.nn.gelu(sc_b[rs, tl]) * w).astype(BF16)
            sc_a[...] += _dot_nt(u_ref[j, cols, :], h_ref[j])
            return carry

        lax.fori_loop(0, nq, piece, 0)

    substep(0, sc0, sc1, zt1, zt0, 2 * s - 1)
    substep(1, sc1, sc0, zt0, zt1, 2 * s)

    @pl.when(s == pl.num_programs(1) - 1)
    def _():
        o_ref[...] = x_ref[...] + g_ref[...] * acc_scr[...].reshape(-1, acc_scr.shape[-1]).T.reshape(o_ref.shape)


def _peer_exp(hb, u, vt, e1, e2, th, x, mod, l, moff):
    B, T, D = x.shape
    bb, tt = _tile_cfg(B, T, PEER_TB)
    TB, nT = bb * tt, T // tt
    NE = u.shape[1]
    MC = PEER_MC
    NC = NE // MC
    NP = NC // 2
    xspec = pl.BlockSpec((bb, tt, D), lambda i, s: (i // nT, i % nT, 0))
    hspec = pl.BlockSpec((PEER_HEADS, N_KEYS, TB), lambda i, s: (0, 0, i))
    return pl.pallas_call(
        functools.partial(_peer_exp_kernel, mc=MC, nc=NC),
        grid=((B // bb) * nT, NP + 1),
        in_specs=[pl.BlockSpec((PEER_KQ, TB, D // PEER_KQ), lambda i, s: (0, i, 0)),
                  pl.BlockSpec((PEER_KQ, 2 * MC, D // PEER_KQ), lambda i, s: (0, jnp.minimum(s, NP - 1), 0)),
                  pl.BlockSpec((PEER_KQ, D // PEER_KQ, MC), lambda i, s: (0, 0, 2 * jnp.maximum(s - 1, 0))),
                  pl.BlockSpec((PEER_KQ, D // PEER_KQ, MC), lambda i, s: (0, 0, 2 * jnp.maximum(s - 1, 0) + 1)),
                  hspec, hspec,
                  pl.BlockSpec((PEER_HEADS, TB), lambda i, s: (0, i)),
                  xspec,
                  pl.BlockSpec((None, bb, 1, D), lambda i, s: (l, moff // bb + i // nT, 0, 5))],
        out_specs=xspec,
        out_shape=jax.ShapeDtypeStruct((B, T, D), F32),
        scratch_shapes=[pltpu.VMEM((PEER_KQ, D // PEER_KQ, TB), F32),
                        pltpu.VMEM((MC, TB), F32), pltpu.VMEM((MC, TB), F32),
                        pltpu.VMEM((MC, TB), BF16), pltpu.VMEM((MC, TB), BF16)],
        compiler_params=_cparams(("parallel", "arbitrary")),
        name="peer_exp",
    )(hb, u, vt, vt, e1, e2, th, x, mod)


def _final_kernel(x_ref, w_ref, o_ref):
    x = x_ref[...]
    ms = jnp.mean(x * x, axis=-1, keepdims=True)
    o_ref[...] = x * lax.rsqrt(ms + EPS) * w_ref[...]


def _final_norm(x, w):
    B, T, D = x.shape
    bb, tt = _tile_cfg(B, T, 512)
    nT = T // tt
    xspec = pl.BlockSpec((bb, tt, D), lambda i: (i // nT, i % nT, 0))
    return pl.pallas_call(
        _final_kernel,
        grid=((B // bb) * nT,),
        in_specs=[xspec, pl.BlockSpec((1, D), lambda i: (0, 0))],
        out_specs=xspec,
        out_shape=jax.ShapeDtypeStruct((B, T, D), F32),
        compiler_params=_cparams(("parallel",)),
        name="final_norm",
    )(x, w)


def _trunk(x, mod, moff, pos0, states, lw, final_w):
    B, T, D = x.shape
    rtabs = _ret_tables(T, pos0)
    new = []
    for l, (st, w) in enumerate(zip(states, lw)):
        s_ret, s_c, s_n, s_m, s_conv, s_hr, s_hi = st
        z, zg = _in_proj(x, mod, l, moff, w['n1'], w['w_main'], w['w_gate'])
        o_r, n_ret = _retention(z, B, T, s_ret, rtabs, w['ret_gn'])
        o_m, n_c, n_n, n_m, n_conv = _mlstm(z, zg, B, T, (s_c, s_n, s_m, s_conv), w['conv_w'], w['conv_b'],
                                            w['wqk'], w['gate_b'], w['m_gn'], w['m_skip'])
        o_s, n_hr, n_hi = _s5(z, B, T, s_hr, s_hi, w['s5_tabs'], w['s5_d'], w['s5_wglu'], w['s5_bglu'])
        merged = _branch(o_r, o_m, o_s, z, w['w_ret_out'], w['w_mlstm_out'], w['w_s5_out'])
        x = _wo(merged, w['w_o'], x, mod, l, moff)
        hb, sc = _peer_q(x, mod, l, moff, w['n2'], w['peer_wq'], w['peer_k1'], w['peer_k2'])
        e1, e2, th = _peer_sel(sc)
        x = _peer_exp(hb, w['peer_u'], w['peer_vt'], e1, e2, th, x, mod, l, moff)
        new.append((n_ret, n_c, n_n, n_m[..., 0], n_conv, n_hr, n_hi))
    y = _final_norm(x, final_w)
    return y, [jnp.stack([s[i] for s in new]) for i in range(7)]


def kernel(x_prompt, x_sample, state_ret, state_mlstm_c, state_mlstm_n, state_mlstm_m, state_mlstm_conv,
           state_s5_re, state_s5_im, c_prompt, c_sample, ada_w, ada_b, norm1_w, norm2_w, final_norm_w,
           w_in, ret_gn_w, w_ret_out, mlstm_conv_w, mlstm_conv_b, mlstm_wq, mlstm_wk, mlstm_b_i, mlstm_b_f,
           mlstm_gn_w, mlstm_skip, w_mlstm_out, s5_a_re, s5_a_im, s5_log_dt, s5_b_re, s5_b_im, s5_c_re,
           s5_c_im, s5_d, s5_w_glu, s5_b_glu, w_s5_out, w_o, peer_wq, peer_k1, peer_k2, peer_u, peer_v):
    depth = w_in.shape[0]
    Bp, Tp, D = x_prompt.shape
    Bs, Ts, _ = x_sample.shape
    W = HEADS * DH
    G, P = s5_a_re.shape[1:]
    NS = G * P
    past_len = 16384

    pad = -(Bs + Bp) % SUBLANES
    c_all = jnp.concatenate([c_sample, c_prompt, jnp.zeros((pad, D), F32)], axis=0)
    mod = _ada(c_all, ada_w, ada_b)
    mod = mod.reshape(depth, c_all.shape[0], 1, ada_w.shape[2])

    a_end = 7 * W
    g_end = a_end + 2 * HEADS
    lw = []
    for l in range(depth):
        w_main = jnp.concatenate([w_in[l, :, :a_end], w_in[l, :, g_end:]], axis=1).astype(BF16)
        w_gate = jnp.pad(w_in[l, :, a_end:g_end], ((0, 0), (0, LANES - 2 * HEADS))).astype(BF16)
        gate_b = jnp.pad(jnp.concatenate([mlstm_b_i[l], mlstm_b_f[l]]), (0, LANES - 2 * HEADS)).reshape(1, LANES)
        lw.append(dict(
            n1=norm1_w[l].reshape(1, D), n2=norm2_w[l].reshape(1, D),
            w_main=w_main, w_gate=w_gate, gate_b=gate_b,
            ret_gn=ret_gn_w[l].reshape(1, W),
            conv_w=mlstm_conv_w[l], conv_b=mlstm_conv_b[l].reshape(1, W),
            wqk=jnp.concatenate([mlstm_wq[l], mlstm_wk[l]], axis=-1).astype(BF16),
            m_gn=mlstm_gn_w[l].reshape(1, W), m_skip=mlstm_skip[l].reshape(1, W),
            s5_tabs=_s5_tables(s5_a_re[l], s5_a_im[l], s5_log_dt[l], s5_b_re[l], s5_b_im[l], s5_c_re[l], s5_c_im[l]),
            s5_d=s5_d[l].reshape(1, -1), s5_wglu=s5_w_glu[l].astype(BF16), s5_bglu=s5_b_glu[l].reshape(1, -1),
            w_ret_out=w_ret_out[l].astype(BF16), w_mlstm_out=w_mlstm_out[l].astype(BF16),
            w_s5_out=w_s5_out[l].astype(BF16), w_o=w_o[l].astype(BF16),
            peer_wq=peer_wq[l].astype(BF16), peer_k1=peer_k1[l], peer_k2=peer_k2[l],
            peer_u=peer_u[l].astype(BF16).reshape(-1, PEER_KQ, D // PEER_KQ).transpose(1, 0, 2), peer_vt=peer_v[l].T.astype(BF16).reshape(PEER_KQ, D // PEER_KQ, -1)))

    def zero_states(B):
        return (jnp.zeros((B, HEADS, DH, DH), F32), jnp.zeros((B, HEADS, DH, DH), F32),
                jnp.zeros((B, HEADS, DH), F32), jnp.zeros((B, HEADS, LANES), F32),
                jnp.zeros((B, CONV_W - 1, W), F32), jnp.zeros((B, 1, NS), F32), jnp.zeros((B, 1, NS), F32))

    prompt_states = [zero_states(Bp) for _ in range(depth)]
    sample_states = [(state_ret[l], state_mlstm_c[l], state_mlstm_n[l],
                      jnp.broadcast_to(state_mlstm_m[l][..., None], (Bs, HEADS, LANES)),
                      state_mlstm_conv[l], state_s5_re[l].reshape(Bs, 1, NS), state_s5_im[l].reshape(Bs, 1, NS))
                     for l in range(depth)]

    y_p, ps = _trunk(x_prompt, mod, Bs, 0, prompt_states, lw, final_norm_w.reshape(1, D))
    y_s, ss = _trunk(x_sample, mod, 0, past_len, sample_states, lw, final_norm_w.reshape(1, D))

    def unpack(st, B):
        r, c, n, m, conv, hr, hi = st
        return (r, c, n, m, conv, hr.reshape(depth, B, G, P), hi.reshape(depth, B, G, P))

    return (y_p, y_s) + unpack(ps, Bp) + unpack(ss, Bs)
```

```python
import functools
import math

import jax
import jax.numpy as jnp
import numpy as np
from jax import lax
from jax.experimental import pallas as pl
from jax.experimental.pallas import tpu as pltpu

F32 = jnp.float32
BF16 = jnp.bfloat16

EPS = 1e-6
ROPE_BASE = 10000.0
HEADS = 8
DH = 128
CONV_W = 4
S5_GROUP = 16
S5_STATE = 64
PEER_HEADS = 8
PEER_TOPK = 16
N_KEYS = 128
CHUNK = 128
LANES = 128
SUBLANES = 8
VMEM_LIMIT = 56 * 1024 * 1024
RET_GROUP = 4
MLSTM_GROUP = 2
PEER_TB = 512
PEER_MC = 512
PEER_KQ = 4


def _cparams(sem):
    return pltpu.CompilerParams(dimension_semantics=sem, vmem_limit_bytes=VMEM_LIMIT)


def _tile_cfg(B, T, target):
    if T >= target:
        bb, tt = 1, target
    else:
        bb, tt = min(B, target // T), T
    assert T % tt == 0 and B % bb == 0
    return bb, tt


def _silu(x):
    return x * jax.nn.sigmoid(x)


def _norm_mod(x, w, sc, sh):
    ms = jnp.mean(x * x, axis=-1, keepdims=True)
    y = x * lax.rsqrt(ms + EPS) * w
    return y * (1.0 + sc) + sh


def _head_norm(x):
    mu = jnp.mean(x, axis=-1, keepdims=True)
    xc = x - mu
    var = jnp.mean(xc * xc, axis=-1, keepdims=True)
    return xc * lax.rsqrt(var + EPS)


def _dot(a, b):
    return jnp.dot(a, b, preferred_element_type=F32)


def _dot_nt(a, b, **kw):
    return lax.dot_general(a, b, (((1,), (1,)), ((), ())), preferred_element_type=F32, **kw)


def _dot_tn(a, b):
    return lax.dot_general(a, b, (((0,), (0,)), ((), ())), preferred_element_type=F32)


def _rows(start, n):
    if isinstance(start, int):
        return pl.ds(start, n)
    return pl.ds(pl.multiple_of(start, SUBLANES), n)


def _for_each_group(bb, gs, body):
    gs = min(gs, bb)
    assert bb % gs == 0
    if bb == gs:
        body(list(range(bb)))
    else:
        def f(gi, c):
            body([gi * gs + i for i in range(gs)])
            return c
        lax.fori_loop(0, bb // gs, f, 0)


def _for_each_seq(bb, body, unroll=1):
    if bb == 1:
        body(0)
    else:
        def f(bi, c):
            body(bi)
            return c
        lax.fori_loop(0, bb, f, 0, unroll=unroll)


def _ada_kernel(c_ref, w_ref, b_ref, o_ref):
    a = _silu(c_ref[...]).astype(BF16)
    o_ref[...] = _dot(a, w_ref[...].astype(BF16)) + b_ref[...]


def _ada(c_all, ada_w, ada_b):
    L, D, N = ada_w.shape
    Bc = c_all.shape[0]
    TN = 1024
    return pl.pallas_call(
        _ada_kernel,
        grid=(L, N // TN),
        in_specs=[pl.BlockSpec((Bc, D), lambda l, j: (0, 0)),
                  pl.BlockSpec((None, D, TN), lambda l, j: (l, 0, j)),
                  pl.BlockSpec((None, 1, TN), lambda l, j: (l, 0, j))],
        out_specs=pl.BlockSpec((None, Bc, TN), lambda l, j: (l, 0, j)),
        out_shape=jax.ShapeDtypeStruct((L, Bc, N), F32),
        compiler_params=_cparams(("parallel", "parallel")),
        name="ada",
    )(c_all, ada_w, ada_b.reshape(L, 1, N))


def _in_kernel(x_ref, nw_ref, sh_ref, sc_ref, wm_ref, wg_ref, z_ref, zg_ref, h_scr):
    @pl.when(pl.program_id(1) == 0)
    def _():
        h = _norm_mod(x_ref[...], nw_ref[...], sc_ref[...], sh_ref[...])
        hb = h.reshape(h_scr.shape).astype(BF16)
        h_scr[...] = hb
        zg_ref[...] = _dot(hb, wg_ref[...])
    z_ref[...] = _dot(h_scr[...], wm_ref[...])


def _in_proj(x, mod, l, moff, nw, wm, wg):
    B, T, D = x.shape
    bb, tt = _tile_cfg(B, T, 512)
    TM, nT = bb * tt, T // tt
    nI = (B // bb) * nT
    N = wm.shape[1]
    TN = 1024
    xmap = lambda i, j: (i // nT, i % nT, 0)
    mmap = lambda k: (lambda i, j: (l, moff // bb + i // nT, 0, k))
    return pl.pallas_call(
        _in_kernel,
        grid=(nI, N // TN),
        in_specs=[pl.BlockSpec((bb, tt, D), xmap),
                  pl.BlockSpec((1, D), lambda i, j: (0, 0)),
                  pl.BlockSpec((None, bb, 1, D), mmap(0)),
                  pl.BlockSpec((None, bb, 1, D), mmap(1)),
                  pl.BlockSpec((D, TN), lambda i, j: (0, j)),
                  pl.BlockSpec((D, LANES), lambda i, j: (0, 0))],
        out_specs=[pl.BlockSpec((TM, TN), lambda i, j: (i, j)),
                   pl.BlockSpec((TM, LANES), lambda i, j: (i, 0))],
        out_shape=[jax.ShapeDtypeStruct((B * T, N), F32),
                   jax.ShapeDtypeStruct((B * T, LANES), F32)],
        scratch_shapes=[pltpu.VMEM((TM, D), BF16)],
        compiler_params=_cparams(("parallel", "arbitrary")),
        name="in_proj",
    )(x, nw, mod, mod, wm, wg)


def _ret_kernel(zq_ref, zk_ref, zv_ref, zg_ref, cq_ref, sq_ref, ck_ref, sk_ref, intra_ref, cross_ref,
                kdec_ref, cdec_ref, gn_ref, s0_ref, o_ref, s_ref, *, bb, L):
    @pl.when(pl.program_id(1) == 0)
    def _():
        s_ref[...] = s0_ref[...]

    cq, sq, ck, sk = cq_ref[...], sq_ref[...], ck_ref[...], sk_ref[...]

    def group(bis):
        chains = [(i, h) for i in range(len(bis)) for h in range(HEADS)]
        rows = [_rows(bi * L, L) for bi in bis]
        cols = [slice(h * DH, (h + 1) * DH) for h in range(HEADS)]
        q, k, v, s = {}, {}, {}, {}
        for c in chains:
            i, h = c
            qf = zq_ref[rows[i], cols[h]]
            kf = zk_ref[rows[i], cols[h]]
            q[c] = (qf * cq + pltpu.roll(qf, DH // 2, 1) * sq).astype(BF16)
            k[c] = kf * ck + pltpu.roll(kf, DH // 2, 1) * sk
            v[c] = zv_ref[rows[i], cols[h]].astype(BF16)
            s[c] = s_ref[bis[i], h]
        att = {c: _dot_nt(q[c], k[c].astype(BF16)) for c in chains}
        inter = {c: _dot(q[c], s[c].astype(BF16)) for c in chains}
        upd = {c: _dot_tn((k[c] * kdec_ref[c[1]]).astype(BF16), v[c]) for c in chains}
        o = {c: _dot((att[c] * intra_ref[c[1]]).astype(BF16), v[c]) + inter[c] * cross_ref[c[1]] for c in chains}
        for c in chains:
            i, h = c
            s_ref[bis[i], h] = cdec_ref[h] * s[c] + upd[c]
            o_ref[rows[i], cols[h]] = _head_norm(o[c]) * gn_ref[:, cols[h]] * _silu(zg_ref[rows[i], cols[h]])

    _for_each_group(bb, RET_GROUP, group)


def _retention(z, B, T, s0, tabs, gn_w):
    L = CHUNK if T % CHUNK == 0 else T
    nC = T // L
    bb = 1 if nC > 1 else min(B, 8)
    R = bb * L
    W = HEADS * DH
    cq, sq, ck, sk, intra, cross, kdec, cdec = tabs
    zspec = lambda k: pl.BlockSpec((R, W), lambda i, c: (i * nC + c, k))
    tspec = pl.BlockSpec((L, DH), lambda i, c: (c, 0))
    full = lambda a: pl.BlockSpec(a.shape, lambda i, c: (0,) * a.ndim)
    sspec = pl.BlockSpec((bb, HEADS, DH, DH), lambda i, c: (i, 0, 0, 0))
    return pl.pallas_call(
        functools.partial(_ret_kernel, bb=bb, L=L),
        grid=(B // bb, nC),
        in_specs=[zspec(0), zspec(1), zspec(2), zspec(3), tspec, tspec, tspec, tspec,
                  full(intra), full(cross), full(kdec), full(cdec), full(gn_w), sspec],
        out_specs=[pl.BlockSpec((R, W), lambda i, c: (i * nC + c, 0)), sspec],
        out_shape=[jax.ShapeDtypeStruct((B * T, W), F32),
                   jax.ShapeDtypeStruct((B, HEADS, DH, DH), F32)],
        compiler_params=_cparams(("parallel", "arbitrary")),
        name="ret",
    )(z, z, z, z, cq, sq, ck, sk, intra, cross, kdec, cdec, gn_w, s0)


def _ret_tables(T, pos0):
    L = CHUNK if T % CHUNK == 0 else T
    half = DH // 2
    inv = jnp.exp(-math.log(ROPE_BASE) * jnp.arange(half, dtype=F32) / half)
    pos = jnp.arange(T, dtype=F32) + pos0
    ang = pos[:, None] * inv[None, :]
    cos, sin = jnp.cos(ang), jnp.sin(ang)
    c = jnp.concatenate([cos, cos], axis=-1)
    s = jnp.concatenate([-sin, sin], axis=-1)
    kscale = DH ** -0.5
    lg = jnp.log1p(-jnp.exp2(-5.0 - jnp.arange(HEADS, dtype=F32)))
    j = jnp.arange(L, dtype=F32)
    diff = j[:, None] - j[None, :]
    intra = jnp.where(diff >= 0, jnp.exp(lg[:, None, None] * jnp.maximum(diff, 0.0)), 0.0)
    cross = jnp.exp(lg[:, None] * (j + 1.0))
    kdec = jnp.exp(lg[:, None] * (L - 1.0 - j))
    cdec = jnp.exp(lg * L)
    bl = lambda a: jnp.broadcast_to(a[..., None], a.shape + (DH,))
    return (c, s, c * kscale, s * kscale, intra, bl(cross), bl(kdec), bl(cdec[:, None]))


def _mlstm_kernel(zu_ref, zv_ref, zo_ref, zg_ref, cw_ref, cb_ref, wqk_ref, gb_ref, gn_ref, sk_ref,
                  c0_ref, n0_ref, m0_ref, conv0_ref,
                  o_ref, c_ref, n_ref, m_ref, conv_ref, xp_scr, tail_scr, ca_scr, *, bb, L, last):
    ci = pl.program_id(1)

    @pl.when(ci == 0)
    def _():
        c_ref[...] = c0_ref[...]
        n_ref[...] = n0_ref[...]
        m_ref[...] = m0_ref[...]
        tail_scr[...] = jnp.zeros(tail_scr.shape, F32)
        tail_scr[:, SUBLANES - (CONV_W - 1):, :] = conv0_ref[...]

    ri = lax.broadcasted_iota(jnp.int32, (L, L), 0)
    rj = lax.broadcasted_iota(jnp.int32, (L, L), 1)
    causal = rj <= ri
    tri = causal.astype(F32)
    lane = lax.broadcasted_iota(jnp.int32, (L, LANES), 1)
    kscale = DH ** -0.5

    def group(bis):
        ns = len(bis)
        rows = [_rows(bi * L, L) for bi in bis]
        cols = [slice(h * DH, (h + 1) * DH) for h in range(HEADS)]
        chains = [(i, h) for i in range(ns) for h in range(HEADS)]
        gates, csum, gates_t, csum_t, m_all, n_all = [], [], [], [], [], []
        for i, bi in enumerate(bis):
            xp_scr[bi, 0:SUBLANES, :] = tail_scr[bi]
            xp_scr[bi, SUBLANES:, :] = zu_ref[rows[i], :]
            cu = cb_ref[...]
            for t in range(CONV_W):
                cu = cu + xp_scr[bi, pl.ds(SUBLANES - (CONV_W - 1) + t, L), :] * cw_ref[pl.ds(t, 1), :]
            tail_scr[bi] = xp_scr[bi, L:L + SUBLANES, :]
            ca_scr[bi] = _silu(cu)
            gz = zg_ref[rows[i], :] + gb_ref[...]
            g = jnp.where(lane < HEADS, gz, jax.nn.log_sigmoid(gz))
            cs = jnp.dot(tri, g, preferred_element_type=F32, precision=lax.Precision.HIGHEST)
            gates.append(g)
            csum.append(cs)
            gates_t.append(g.T)
            csum_t.append(cs.T)
            m_all.append(m_ref[bi])
            n_all.append(n_ref[bi])
        ca = {c: ca_scr[bis[c[0]], :, cols[c[1]]] for c in chains}
        qk = {c: _dot(ca[c].astype(BF16), wqk_ref[c[1]]) for c in chains}
        q = {c: qk[c][:, :DH] for c in chains}
        k = {c: qk[c][:, DH:] * kscale for c in chains}
        qb = {c: q[c].astype(BF16) for c in chains}
        v = {c: zv_ref[rows[c[0]], cols[c[1]]].astype(BF16) for c in chains}
        c_prev = {c: c_ref[bis[c[0]], c[1]] for c in chains}
        sraw = {c: _dot_nt(qb[c], k[c].astype(BF16)) for c in chains}
        qc = {c: _dot(qb[c], c_prev[c].astype(BF16)) for c in chains}
        s, mt, w_int, m_prev, n_prev, b_col, i_col = {}, {}, {}, {}, {}, {}, {}
        for c in chains:
            i, h = c
            i_col[c] = gates[i][:, h:h + 1]
            b_col[c] = csum[i][:, HEADS + h:HEADS + h + 1]
            i_row = gates_t[i][h:h + 1, :]
            b_row = csum_t[i][HEADS + h:HEADS + h + 1, :]
            m_prev[c] = m_all[i][h:h + 1, :1]
            n_prev[c] = n_all[i][h:h + 1, :]
            dlog = jnp.where(causal, b_col[c] - b_row + i_row, -jnp.inf)
            inter = b_col[c] + m_prev[c]
            mt[c] = jnp.maximum(inter, jnp.max(dlog, axis=-1, keepdims=True))
            s[c] = sraw[c] * jnp.exp(dlog - mt[c])
            w_int[c] = jnp.exp(inter - mt[c])
        sv = {c: _dot(s[c].astype(BF16), v[c]) for c in chains}
        kt, dec, m_new = {}, {}, {}
        for c in chains:
            m_new[c] = mt[c][L - 1:L, :]
            b_last = b_col[c][L - 1:L, :]
            tail = jnp.exp(b_last - b_col[c] + i_col[c] - m_new[c])
            dec[c] = jnp.exp(b_last + m_prev[c] - m_new[c])
            kt[c] = k[c] * tail
        upd = {c: _dot_tn(kt[c].astype(BF16), v[c]) for c in chains}
        n_rows = [[] for _ in range(ns)]
        m_rows = [[] for _ in range(ns)]
        for c in chains:
            i, h = c
            num = sv[c] + w_int[c] * qc[c]
            den = (jnp.sum(s[c], axis=-1, keepdims=True)
                   + w_int[c] * jnp.sum(q[c] * n_prev[c], axis=-1, keepdims=True))
            hh = num / jnp.maximum(jnp.abs(den), jnp.exp(-mt[c]))
            c_ref[bis[i], h] = dec[c] * c_prev[c] + upd[c]
            n_rows[i].append(dec[c] * n_prev[c] + jnp.sum(kt[c], axis=0, keepdims=True))
            m_rows[i].append(jnp.broadcast_to(m_new[c], (1, LANES)))
            hm = jax.nn.sigmoid(zo_ref[rows[i], cols[h]]) * hh
            o_ref[rows[i], cols[h]] = _head_norm(hm) * gn_ref[:, cols[h]] + sk_ref[:, cols[h]] * ca[c]
        for i, bi in enumerate(bis):
            n_ref[bi] = jnp.concatenate(n_rows[i], axis=0)
            m_ref[bi] = jnp.concatenate(m_rows[i], axis=0)

    _for_each_group(bb, MLSTM_GROUP, group)

    @pl.when(ci == last)
    def _():
        conv_ref[...] = tail_scr[:, SUBLANES - (CONV_W - 1):, :]


def _mlstm(z, zg, B, T, states, cw, cb, wqk, gb, gn_w, skip):
    L = CHUNK if T % CHUNK == 0 else T
    nC = T // L
    bb = 1 if nC > 1 else min(B, 8)
    R = bb * L
    W = HEADS * DH
    c0, n0, m0, conv0 = states
    zspec = lambda k: pl.BlockSpec((R, W), lambda i, c: (i * nC + c, k))
    full = lambda a: pl.BlockSpec(a.shape, lambda i, c: (0,) * a.ndim)
    cspec = pl.BlockSpec((bb, HEADS, DH, DH), lambda i, c: (i, 0, 0, 0))
    nspec = pl.BlockSpec((bb, HEADS, DH), lambda i, c: (i, 0, 0))
    vspec = pl.BlockSpec((bb, CONV_W - 1, W), lambda i, c: (i, 0, 0))
    return pl.pallas_call(
        functools.partial(_mlstm_kernel, bb=bb, L=L, last=nC - 1),
        grid=(B // bb, nC),
        in_specs=[zspec(4), zspec(5), zspec(6), pl.BlockSpec((R, LANES), lambda i, c: (i * nC + c, 0)),
                  full(cw), full(cb), full(wqk), full(gb), full(gn_w), full(skip),
                  cspec, nspec, nspec, vspec],
        out_specs=[pl.BlockSpec((R, W), lambda i, c: (i * nC + c, 0)), cspec, nspec, nspec, vspec],
        out_shape=[jax.ShapeDtypeStruct((B * T, W), F32),
                   jax.ShapeDtypeStruct((B, HEADS, DH, DH), F32),
                   jax.ShapeDtypeStruct((B, HEADS, DH), F32),
                   jax.ShapeDtypeStruct((B, HEADS, LANES), F32),
                   jax.ShapeDtypeStruct((B, CONV_W - 1, W), F32)],
        scratch_shapes=[pltpu.VMEM((bb, L + SUBLANES, W), F32),
                        pltpu.VMEM((bb, SUBLANES, W), F32),
                        pltpu.VMEM((bb, L, W), F32)],
        compiler_params=_cparams(("parallel", "arbitrary")),
        name="mlstm",
    )(z, z, z, zg, cw, cb, wqk, gb, gn_w, skip, c0, n0, m0, conv0)


def _s5_kernel(zu_ref, bw_ref, ar_ref, ai_ref, pr_ref, pi_ref, cwr_ref, cwi_ref, d_ref, wglu_ref, bglu_ref,
               h0r_ref, h0i_ref, o_ref, hr_ref, hi_ref, xr_scr, xi_scr, *, bb, Lc):
    @pl.when(pl.program_id(1) == 0)
    def _():
        hr_ref[...] = h0r_ref[...]
        hi_ref[...] = h0i_ref[...]

    NS = xr_scr.shape[1]
    KB = bw_ref.shape[0]
    SB = NS // KB
    u = zu_ref[...]
    ub = u.astype(BF16)
    for kb in range(KB):
        r = _dot(ub[:, kb * LANES:(kb + 1) * LANES], bw_ref[kb])
        xr_scr[:, kb * SB:(kb + 1) * SB] = r[:, :SB]
        xi_scr[:, kb * SB:(kb + 1) * SB] = r[:, SB:]

    CW = 512
    ngrp = Lc // SUBLANES

    def seq(bi):
        def grp(gi, carry):
            rows = _rows(bi * Lc + gi * SUBLANES, SUBLANES)
            for cc in range(NS // CW):
                cols = slice(cc * CW, (cc + 1) * CW)
                xr = xr_scr[rows, cols]
                xi = xi_scr[rows, cols]
                for lv in range(3):
                    rr = pltpu.roll(xr, 1 << lv, 0)
                    ri = pltpu.roll(xi, 1 << lv, 0)
                    a_r = ar_ref[lv, :, cols]
                    a_i = ai_ref[lv, :, cols]
                    xr, xi = xr + a_r * rr - a_i * ri, xi + a_r * ri + a_i * rr
                cr = jnp.broadcast_to(hr_ref[bi, :, cols], (SUBLANES, CW))
                cim = jnp.broadcast_to(hi_ref[bi, :, cols], (SUBLANES, CW))
                p_r = pr_ref[:, cols]
                p_i = pi_ref[:, cols]
                hr = xr + p_r * cr - p_i * cim
                hi = xi + p_r * cim + p_i * cr
                xr_scr[rows, cols] = hr
                xi_scr[rows, cols] = hi
                hr_ref[bi, :, cols] = hr[SUBLANES - 1:, :]
                hi_ref[bi, :, cols] = hi[SUBLANES - 1:, :]
            return carry
        lax.fori_loop(0, ngrp, grp, 0)

    _for_each_seq(bb, seq)

    hrb = xr_scr[...].astype(BF16)
    hib = xi_scr[...].astype(BF16)
    ys = []
    for kb in range(KB):
        cols = slice(kb * SB, (kb + 1) * SB)
        ys.append(_dot(hrb[:, cols], cwr_ref[kb]) + _dot(hib[:, cols], cwi_ref[kb]))
    y = jnp.concatenate(ys, axis=1) + d_ref[...] * u
    g = jax.nn.gelu(y)
    o_ref[...] = g * jax.nn.sigmoid(_dot(g.astype(BF16), wglu_ref[...]) + bglu_ref[...])


def _s5(z, B, T, h0r, h0i, tabs, d, wglu, bglu):
    bw, ar, ai, pr, pi, cwr, cwi = tabs
    Lc = min(T, 256)
    nC = T // Lc
    bb = 1 if nC > 1 else min(B, max(1, 128 // T))
    R = bb * Lc
    W = d.shape[1]
    NS = h0r.shape[-1]
    full = lambda a: pl.BlockSpec(a.shape, lambda i, c: (0,) * a.ndim)
    hspec = pl.BlockSpec((bb, 1, NS), lambda i, c: (i, 0, 0))
    return pl.pallas_call(
        functools.partial(_s5_kernel, bb=bb, Lc=Lc),
        grid=(B // bb, nC),
        in_specs=[pl.BlockSpec((R, W), lambda i, c: (i * nC + c, 7)),
                  full(bw), full(ar), full(ai), full(pr), full(pi), full(cwr), full(cwi),
                  full(d), full(wglu), full(bglu), hspec, hspec],
        out_specs=[pl.BlockSpec((R, W), lambda i, c: (i * nC + c, 0)), hspec, hspec],
        out_shape=[jax.ShapeDtypeStruct((B * T, W), F32),
                   jax.ShapeDtypeStruct((B, 1, NS), F32),
                   jax.ShapeDtypeStruct((B, 1, NS), F32)],
        scratch_shapes=[pltpu.VMEM((R, NS), F32), pltpu.VMEM((R, NS), F32)],
        compiler_params=_cparams(("parallel", "arbitrary")),
        name="s5",
    )(z, bw, ar, ai, pr, pi, cwr, cwi, d, wglu, bglu, h0r, h0i)


def _s5_tables(a_re, a_im, log_dt, b_re, b_im, c_re, c_im):
    G, P = a_re.shape
    dt = jnp.exp(log_dt.astype(F32))[:, None]
    lam_re = -jnp.abs(a_re.astype(F32))
    lam_im = a_im.astype(F32)

    def power(n):
        n = jnp.asarray(n, F32)[:, None, None]
        mag = jnp.exp(lam_re * dt * n)
        return ((mag * jnp.cos(lam_im * dt * n)).reshape(-1, G * P),
                (mag * jnp.sin(lam_im * dt * n)).reshape(-1, G * P))

    mag = jnp.exp(lam_re * dt)
    ab_re = mag * jnp.cos(lam_im * dt)
    ab_im = mag * jnp.sin(lam_im * dt)
    den = lam_re * lam_re + lam_im * lam_im
    co_re = ((ab_re - 1.0) * lam_re + ab_im * lam_im) / den
    co_im = (ab_im * lam_re - (ab_re - 1.0) * lam_im) / den
    br, bi = b_re.astype(F32), b_im.astype(F32)
    bb_re = co_re[..., None] * br - co_im[..., None] * bi
    bb_im = co_re[..., None] * bi + co_im[..., None] * br
    gpb = LANES // S5_GROUP
    KB = G // gpb
    eye = jnp.eye(gpb, dtype=F32)
    blk = lambda w: jnp.einsum('kgpc,gh->kgchp', w.reshape(KB, gpb, P, S5_GROUP), eye).reshape(KB, LANES, gpb * P)
    bw = jnp.concatenate([blk(bb_re), blk(bb_im)], axis=-1).astype(BF16)
    cblk = lambda w: jnp.einsum('kgcp,gh->kgphc', w.reshape(KB, gpb, S5_GROUP, P), eye).reshape(KB, gpb * P, LANES)
    cwr = cblk(c_re.astype(F32)).astype(BF16)
    cwi = cblk(-c_im.astype(F32)).astype(BF16)
    rows = jnp.arange(SUBLANES)
    ars, ais = [], []
    for lv in range(3):
        r, i = power([1 << lv])
        keep = (rows >= (1 << lv))[:, None]
        ars.append(jnp.where(keep, r, 0.0))
        ais.append(jnp.where(keep, i, 0.0))
    pr, pi = power(np.arange(1, SUBLANES + 1))
    return bw, jnp.stack(ars), jnp.stack(ais), pr, pi, cwr, cwi


def _branch_kernel(or_ref, om_ref, os_ref, gr_ref, gm_ref, gs_ref, wr_ref, wm_ref, ws_ref, o_ref):
    br = _dot(or_ref[...].astype(BF16), wr_ref[...])
    bm = _dot(om_ref[...].astype(BF16), wm_ref[...])
    bs = _dot(os_ref[...].astype(BF16), ws_ref[...])
    merged = (jax.nn.sigmoid(gr_ref[...]) * br + jax.nn.sigmoid(gm_ref[...]) * bm
              + jax.nn.sigmoid(gs_ref[...]) * bs)
    o_ref[...] = merged.astype(BF16)


def _branch(o_r, o_m, o_s, z, wr, wm, ws):
    M, W = o_r.shape
    D = wr.shape[1]
    TM, TN = min(M, 512), 1024
    nJ = D // TN
    ospec = pl.BlockSpec((TM, W), lambda i, j: (i, 0))
    gspec = lambda k: pl.BlockSpec((TM, TN), lambda i, j: (i, 8 + k * nJ + j))
    wspec = pl.BlockSpec((W, TN), lambda i, j: (0, j))
    return pl.pallas_call(
        _branch_kernel,
        grid=(M // TM, nJ),
        in_specs=[ospec, ospec, ospec, gspec(0), gspec(1), gspec(2), wspec, wspec, wspec],
        out_specs=pl.BlockSpec((TM, TN), lambda i, j: (i, j)),
        out_shape=jax.ShapeDtypeStruct((M, D), BF16),
        compiler_params=_cparams(("parallel", "parallel")),
        name="branch",
    )(o_r, o_m, o_s, z, z, z, wr, wm, ws)


def _wo_kernel(m_ref, w_ref, x_ref, g_ref, o_ref):
    y = _dot(m_ref[...], w_ref[...])
    o_ref[...] = x_ref[...] + g_ref[...] * y.reshape(o_ref.shape)


def _wo(merged, w_o, x, mod, l, moff):
    B, T, D = x.shape
    bb, tt = _tile_cfg(B, T, 512)
    TM, nT = bb * tt, T // tt
    TN = 1024
    xspec = pl.BlockSpec((bb, tt, TN), lambda i, j: (i // nT, i % nT, j))
    return pl.pallas_call(
        _wo_kernel,
        grid=((B // bb) * nT, D // TN),
        in_specs=[pl.BlockSpec((TM, D), lambda i, j: (i, 0)),
                  pl.BlockSpec((D, TN), lambda i, j: (0, j)),
                  xspec,
                  pl.BlockSpec((None, bb, 1, TN), lambda i, j: (l, moff // bb + i // nT, 0, 2 * (D // TN) + j))],
        out_specs=xspec,
        out_shape=jax.ShapeDtypeStruct((B, T, D), F32),
        compiler_params=_cparams(("parallel", "parallel")),
        name="wo",
    )(merged, w_o, x, mod)


def _peer_q_kernel(x_ref, nw_ref, sh_ref, sc_ref, wq_ref, k1_ref, k2_ref, h_ref, st_ref):
    h = _norm_mod(x_ref[...], nw_ref[...], sc_ref[...], sh_ref[...])
    hb = h.reshape(-1, h.shape[-1]).astype(BF16)
    kq = h_ref.shape[-1]
    for j in range(h_ref.shape[0]):
        h_ref[j] = hb[:, j * kq:(j + 1) * kq]
    q = _dot(hb, wq_ref[...])
    dq = k1_ref.shape[-1]
    for hd in range(PEER_HEADS):
        q1 = q[:, (2 * hd) * dq:(2 * hd + 1) * dq]
        q2 = q[:, (2 * hd + 1) * dq:(2 * hd + 2) * dq]
        st_ref[2 * hd] = _dot_nt(k1_ref[hd], q1, precision=lax.Precision.HIGHEST)
        st_ref[2 * hd + 1] = _dot_nt(k2_ref[hd], q2, precision=lax.Precision.HIGHEST)


def _peer_q(x, mod, l, moff, nw, wq, k1, k2):
    B, T, D = x.shape
    bb, tt = _tile_cfg(B, T, 256)
    TM, nT = bb * tt, T // tt
    full = lambda a: pl.BlockSpec(a.shape, lambda i: (0,) * a.ndim)
    mmap = lambda k: (lambda i: (l, moff // bb + i // nT, 0, k))
    return pl.pallas_call(
        _peer_q_kernel,
        grid=((B // bb) * nT,),
        in_specs=[pl.BlockSpec((bb, tt, D), lambda i: (i // nT, i % nT, 0)),
                  pl.BlockSpec((1, D), lambda i: (0, 0)),
                  pl.BlockSpec((None, bb, 1, D), mmap(3)),
                  pl.BlockSpec((None, bb, 1, D), mmap(4)),
                  full(wq), full(k1), full(k2)],
        out_specs=[pl.BlockSpec((PEER_KQ, TM, D // PEER_KQ), lambda i: (0, i, 0)),
                   pl.BlockSpec((2 * PEER_HEADS, N_KEYS, TM), lambda i: (0, 0, i))],
        out_shape=[jax.ShapeDtypeStruct((PEER_KQ, B * T, D // PEER_KQ), BF16),
                   jax.ShapeDtypeStruct((2 * PEER_HEADS, N_KEYS, B * T), F32)],
        compiler_params=_cparams(("parallel",)),
        name="peer_q",
    )(x, nw, mod, mod, wq, k1, k2)


def _top_values(cur, n):
    vals = []
    for _ in range(n):
        m = jnp.max(cur, axis=0, keepdims=True)
        vals.append(m)
        cur = jnp.where(cur == m, -jnp.inf, cur)
    return vals


def _peer_sel_kernel(st_ref, e1_ref, e2_ref, th_ref):
    ts = st_ref.shape[-1]
    neg = jnp.full((1, ts), -jnp.inf, F32)
    n = PEER_TOPK + 1
    ths = []
    for hd in range(PEER_HEADS):
        s1 = st_ref[2 * hd]
        s2 = st_ref[2 * hd + 1]
        v1 = _top_values(s1, n)
        v2 = _top_values(s2, n)
        rows = [v1[r] + v2[c] for r in range(n) for c in range(n // (r + 1))]
        rows += [neg] * (-len(rows) % SUBLANES)
        top = _top_values(jnp.concatenate(rows, axis=0), n)
        den = jnp.exp(top[0] - top[0])
        for t in top[1:PEER_TOPK]:
            den = den + jnp.exp(t - top[0])
        e1_ref[hd] = jnp.exp(s1 - v1[0]) / den
        e2_ref[hd] = jnp.exp(s2 - v2[0])
        ths.append(jnp.exp(0.5 * (top[PEER_TOPK - 1] + top[PEER_TOPK]) - top[0]) / den)
    th_ref[...] = jnp.concatenate(ths, axis=0)


def _peer_sel(st):
    H2, K, M = st.shape
    TS = min(M, 256)
    hspec = pl.BlockSpec((PEER_HEADS, K, TS), lambda i: (0, 0, i))
    return pl.pallas_call(
        _peer_sel_kernel,
        grid=(M // TS,),
        in_specs=[pl.BlockSpec((H2, K, TS), lambda i: (0, 0, i))],
        out_specs=[hspec, hspec, pl.BlockSpec((PEER_HEADS, TS), lambda i: (0, i))],
        out_shape=[jax.ShapeDtypeStruct((PEER_HEADS, K, M), F32),
                   jax.ShapeDtypeStruct((PEER_HEADS, K, M), F32),
                   jax.ShapeDtypeStruct((PEER_HEADS, M), F32)],
        compiler_params=_cparams(("parallel",)),
        name="peer_sel",
    )(st)


def _peer_exp_kernel(h_ref, u_ref, vta_ref, vtb_ref, e1_ref, e2_ref, th_ref, x_ref, g_ref, o_ref,
                     acc_scr, sc0, sc1, zt0, zt1, *, mc, nc):
    s = pl.program_id(1)
    slabs = mc // N_KEYS

    @pl.when(s == 0)
    def _():
        acc_scr[...] = jnp.zeros(acc_scr.shape, F32)
        sc1[...] = jnp.zeros(sc1.shape, F32)
        zt0[...] = jnp.zeros(zt0.shape, BF16)
        zt1[...] = jnp.zeros(zt1.shape, BF16)

    nq = u_ref.shape[0]
    assert slabs == nq

    def substep(half, sc_a, sc_b, zt_b, zt_c, chunk_b):
        c = jnp.clip(chunk_b, 0, nc - 1)
        cols = slice(half * mc, (half + 1) * mc)
        sc_a[...] = jnp.zeros(sc_a.shape, F32)

        def piece(j, carry):
            acc_scr[j] += _dot((vta_ref, vtb_ref)[half][j], zt_c[...])
            rs = pl.ds(pl.multiple_of(j * N_KEYS, N_KEYS), N_KEYS)
            i1 = c * slabs + j
            for tc in range(sc_b.shape[1] // LANES):
                tl = slice(tc * LANES, (tc + 1) * LANES)
                w = jnp.zeros((N_KEYS, LANES), F32)
                for hd in range(PEER_HEADS):
                    p = e1_ref[hd, pl.ds(i1, 1), :][:, tl] * e2_ref[hd, :, tl]
                    w = jnp.where(p >= th_ref[pl.ds(hd, 1), tl], w + p, w)
                zt_b[rs, tl] = (jax.nn.gelu(sc_b[rs, tl]) * w).astype(BF16)
            sc_a[...] += _dot_nt(u_ref[j, cols, :], h_ref[j])
            return carry

        lax.fori_loop(0, nq, piece, 0)

    substep(0, sc0, sc1, zt1, zt0, 2 * s - 1)
    substep(1, sc1, sc0, zt0, zt1, 2 * s)

    @pl.when(s == pl.num_programs(1) - 1)
    def _():
        o_ref[...] = x_ref[...] + g_ref[...] * acc_scr[...].reshape(-1, acc_scr.shape[-1]).T.reshape(o_ref.shape)


def _peer_exp(hb, u, vt, e1, e2, th, x, mod, l, moff):
    B, T, D = x.shape
    bb, tt = _tile_cfg(B, T, PEER_TB)
    TB, nT = bb * tt, T // tt
    MC = PEER_MC
    NP = u.shape[0]
    NC = vt.shape[0]
    assert NC == 2 * NP and u.shape[2] == 2 * MC and vt.shape[3] == MC
    xspec = pl.BlockSpec((bb, tt, D), lambda i, s: (i // nT, i % nT, 0))
    hspec = pl.BlockSpec((PEER_HEADS, N_KEYS, TB), lambda i, s: (0, 0, i))
    return pl.pallas_call(
        functools.partial(_peer_exp_kernel, mc=MC, nc=NC),
        grid=((B // bb) * nT, NP + 1),
        in_specs=[pl.BlockSpec((PEER_KQ, TB, D // PEER_KQ), lambda i, s: (0, i, 0)),
                  pl.BlockSpec((None, PEER_KQ, 2 * MC, D // PEER_KQ),
                               lambda i, s: (jnp.minimum(s, NP - 1), 0, 0, 0)),
                  pl.BlockSpec((None, PEER_KQ, D // PEER_KQ, MC),
                               lambda i, s: (2 * jnp.maximum(s - 1, 0), 0, 0, 0)),
                  pl.BlockSpec((None, PEER_KQ, D // PEER_KQ, MC),
                               lambda i, s: (2 * jnp.maximum(s - 1, 0) + 1, 0, 0, 0)),
                  hspec, hspec,
                  pl.BlockSpec((PEER_HEADS, TB), lambda i, s: (0, i)),
                  xspec,
                  pl.BlockSpec((None, bb, 1, D), lambda i, s: (l, moff // bb + i // nT, 0, 5))],
        out_specs=xspec,
        out_shape=jax.ShapeDtypeStruct((B, T, D), F32),
        scratch_shapes=[pltpu.VMEM((PEER_KQ, D // PEER_KQ, TB), F32),
                        pltpu.VMEM((MC, TB), F32), pltpu.VMEM((MC, TB), F32),
                        pltpu.VMEM((MC, TB), BF16), pltpu.VMEM((MC, TB), BF16)],
        compiler_params=_cparams(("parallel", "arbitrary")),
        name="peer_exp",
    )(hb, u, vt, vt, e1, e2, th, x, mod)


def _final_kernel(x_ref, w_ref, o_ref):
    x = x_ref[...]
    ms = jnp.mean(x * x, axis=-1, keepdims=True)
    o_ref[...] = x * lax.rsqrt(ms + EPS) * w_ref[...]


def _final_norm(x, w):
    B, T, D = x.shape
    bb, tt = _tile_cfg(B, T, 512)
    nT = T // tt
    xspec = pl.BlockSpec((bb, tt, D), lambda i: (i // nT, i % nT, 0))
    return pl.pallas_call(
        _final_kernel,
        grid=((B // bb) * nT,),
        in_specs=[xspec, pl.BlockSpec((1, D), lambda i: (0, 0))],
        out_specs=xspec,
        out_shape=jax.ShapeDtypeStruct((B, T, D), F32),
        compiler_params=_cparams(("parallel",)),
        name="final_norm",
    )(x, w)


def _trunk(x, mod, moff, pos0, states, lw, final_w):
    B, T, D = x.shape
    rtabs = _ret_tables(T, pos0)
    new = []
    for l, (st, w) in enumerate(zip(states, lw)):
        s_ret, s_c, s_n, s_m, s_conv, s_hr, s_hi = st
        z, zg = _in_proj(x, mod, l, moff, w['n1'], w['w_main'], w['w_gate'])
        o_r, n_ret = _retention(z, B, T, s_ret, rtabs, w['ret_gn'])
        o_m, n_c, n_n, n_m, n_conv = _mlstm(z, zg, B, T, (s_c, s_n, s_m, s_conv), w['conv_w'], w['conv_b'],
                                            w['wqk'], w['gate_b'], w['m_gn'], w['m_skip'])
        o_s, n_hr, n_hi = _s5(z, B, T, s_hr, s_hi, w['s5_tabs'], w['s5_d'], w['s5_wglu'], w['s5_bglu'])
        merged = _branch(o_r, o_m, o_s, z, w['w_ret_out'], w['w_mlstm_out'], w['w_s5_out'])
        x = _wo(merged, w['w_o'], x, mod, l, moff)
        hb, sc = _peer_q(x, mod, l, moff, w['n2'], w['peer_wq'], w['peer_k1'], w['peer_k2'])
        e1, e2, th = _peer_sel(sc)
        x = _peer_exp(hb, w['peer_u'], w['peer_vt'], e1, e2, th, x, mod, l, moff)
        new.append((n_ret, n_c, n_n, n_m[..., 0], n_conv, n_hr, n_hi))
    y = _final_norm(x, final_w)
    return y, [jnp.stack([s[i] for s in new]) for i in range(7)]


def kernel(x_prompt, x_sample, state_ret, state_mlstm_c, state_mlstm_n, state_mlstm_m, state_mlstm_conv,
           state_s5_re, state_s5_im, c_prompt, c_sample, ada_w, ada_b, norm1_w, norm2_w, final_norm_w,
           w_in, ret_gn_w, w_ret_out, mlstm_conv_w, mlstm_conv_b, mlstm_wq, mlstm_wk, mlstm_b_i, mlstm_b_f,
           mlstm_gn_w, mlstm_skip, w_mlstm_out, s5_a_re, s5_a_im, s5_log_dt, s5_b_re, s5_b_im, s5_c_re,
           s5_c_im, s5_d, s5_w_glu, s5_b_glu, w_s5_out, w_o, peer_wq, peer_k1, peer_k2, peer_u, peer_v):
    depth = w_in.shape[0]
    Bp, Tp, D = x_prompt.shape
    Bs, Ts, _ = x_sample.shape
    W = HEADS * DH
    G, P = s5_a_re.shape[1:]
    NS = G * P
    past_len = 16384

    pad = -(Bs + Bp) % SUBLANES
    c_all = jnp.concatenate([c_sample, c_prompt, jnp.zeros((pad, D), F32)], axis=0)
    mod = _ada(c_all, ada_w, ada_b)
    mod = mod.reshape(depth, c_all.shape[0], 1, ada_w.shape[2])

    a_end = 7 * W
    g_end = a_end + 2 * HEADS
    lw = []
    for l in range(depth):
        w_main = jnp.concatenate([w_in[l, :, :a_end], w_in[l, :, g_end:]], axis=1).astype(BF16)
        w_gate = jnp.pad(w_in[l, :, a_end:g_end], ((0, 0), (0, LANES - 2 * HEADS))).astype(BF16)
        gate_b = jnp.pad(jnp.concatenate([mlstm_b_i[l], mlstm_b_f[l]]), (0, LANES - 2 * HEADS)).reshape(1, LANES)
        lw.append(dict(
            n1=norm1_w[l].reshape(1, D), n2=norm2_w[l].reshape(1, D),
            w_main=w_main, w_gate=w_gate, gate_b=gate_b,
            ret_gn=ret_gn_w[l].reshape(1, W),
            conv_w=mlstm_conv_w[l], conv_b=mlstm_conv_b[l].reshape(1, W),
            wqk=jnp.concatenate([mlstm_wq[l], mlstm_wk[l]], axis=-1).astype(BF16),
            m_gn=mlstm_gn_w[l].reshape(1, W), m_skip=mlstm_skip[l].reshape(1, W),
            s5_tabs=_s5_tables(s5_a_re[l], s5_a_im[l], s5_log_dt[l], s5_b_re[l], s5_b_im[l], s5_c_re[l], s5_c_im[l]),
            s5_d=s5_d[l].reshape(1, -1), s5_wglu=s5_w_glu[l].astype(BF16), s5_bglu=s5_b_glu[l].reshape(1, -1),
            w_ret_out=w_ret_out[l].astype(BF16), w_mlstm_out=w_mlstm_out[l].astype(BF16),
            w_s5_out=w_s5_out[l].astype(BF16), w_o=w_o[l].astype(BF16),
            peer_wq=peer_wq[l].astype(BF16), peer_k1=peer_k1[l], peer_k2=peer_k2[l],
            peer_u=peer_u[l].reshape(-1, 2 * PEER_MC, PEER_KQ, D // PEER_KQ).transpose(0, 2, 1, 3).astype(BF16),
            peer_vt=peer_v[l].reshape(-1, PEER_MC, PEER_KQ, D // PEER_KQ).transpose(0, 2, 3, 1).astype(BF16)))

    def zero_states(B):
        return (jnp.zeros((B, HEADS, DH, DH), F32), jnp.zeros((B, HEADS, DH, DH), F32),
                jnp.zeros((B, HEADS, DH), F32), jnp.zeros((B, HEADS, LANES), F32),
                jnp.zeros((B, CONV_W - 1, W), F32), jnp.zeros((B, 1, NS), F32), jnp.zeros((B, 1, NS), F32))

    prompt_states = [zero_states(Bp) for _ in range(depth)]
    sample_states = [(state_ret[l], state_mlstm_c[l], state_mlstm_n[l],
                      jnp.broadcast_to(state_mlstm_m[l][..., None], (Bs, HEADS, LANES)),
                      state_mlstm_conv[l], state_s5_re[l].reshape(Bs, 1, NS), state_s5_im[l].reshape(Bs, 1, NS))
                     for l in range(depth)]

    y_p, ps = _trunk(x_prompt, mod, Bs, 0, prompt_states, lw, final_norm_w.reshape(1, D))
    y_s, ss = _trunk(x_sample, mod, 0, past_len, sample_states, lw, final_norm_w.reshape(1, D))

    def unpack(st, B):
        r, c, n, m, conv, hr, hi = st
        return (r, c, n, m, conv, hr.reshape(depth, B, G, P), hi.reshape(depth, B, G, P))

    return (y_p, y_s) + unpack(ps, Bp) + unpack(ss, Bs)
```

```python
import functools
import math

import jax
import jax.numpy as jnp
import numpy as np
from jax import lax
from jax.experimental import pallas as pl
from jax.experimental.pallas import tpu as pltpu

F32 = jnp.float32
BF16 = jnp.bfloat16

EPS = 1e-6
ROPE_BASE = 10000.0
HEADS = 8
DH = 128
CONV_W = 4
S5_GROUP = 16
S5_STATE = 64
PEER_HEADS = 8
PEER_TOPK = 16
N_KEYS = 128
CHUNK = 128
LANES = 128
SUBLANES = 8
VMEM_LIMIT = 56 * 1024 * 1024
RET_GROUP = 4
MLSTM_GROUP = 2
PEER_TB = 512
PEER_MC = 512
PEER_KQ = 4


def _cparams(sem):
    return pltpu.CompilerParams(dimension_semantics=sem, vmem_limit_bytes=VMEM_LIMIT)


def _tile_cfg(B, T, target):
    if T >= target:
        bb, tt = 1, target
    else:
        bb, tt = min(B, target // T), T
    assert T % tt == 0 and B % bb == 0
    return bb, tt


def _silu(x):
    return x * jax.nn.sigmoid(x)


def _norm_mod(x, w, sc, sh):
    ms = jnp.mean(x * x, axis=-1, keepdims=True)
    y = x * lax.rsqrt(ms + EPS) * w
    return y * (1.0 + sc) + sh


def _head_norm(x):
    mu = jnp.mean(x, axis=-1, keepdims=True)
    xc = x - mu
    var = jnp.mean(xc * xc, axis=-1, keepdims=True)
    return xc * lax.rsqrt(var + EPS)


def _dot(a, b):
    return jnp.dot(a, b, preferred_element_type=F32)


def _dot_nt(a, b, **kw):
    return lax.dot_general(a, b, (((1,), (1,)), ((), ())), preferred_element_type=F32, **kw)


def _dot_tn(a, b):
    return lax.dot_general(a, b, (((0,), (0,)), ((), ())), preferred_element_type=F32)


def _rows(start, n):
    if isinstance(start, int):
        return pl.ds(start, n)
    return pl.ds(pl.multiple_of(start, SUBLANES), n)


def _for_each_group(bb, gs, body):
    gs = min(gs, bb)
    assert bb % gs == 0
    if bb == gs:
        body(list(range(bb)))
    else:
        def f(gi, c):
            body([gi * gs + i for i in range(gs)])
            return c
        lax.fori_loop(0, bb // gs, f, 0)


def _for_each_seq(bb, body, unroll=1):
    if bb == 1:
        body(0)
    else:
        def f(bi, c):
            body(bi)
            return c
        lax.fori_loop(0, bb, f, 0, unroll=unroll)


def _ada_kernel(c_ref, w_ref, b_ref, o_ref):
    a = _silu(c_ref[...]).astype(BF16)
    o_ref[...] = _dot(a, w_ref[...].astype(BF16)) + b_ref[...]


def _ada(c_all, ada_w, ada_b):
    L, D, N = ada_w.shape
    Bc = c_all.shape[0]
    TN = 1024
    return pl.pallas_call(
        _ada_kernel,
        grid=(L, N // TN),
        in_specs=[pl.BlockSpec((Bc, D), lambda l, j: (0, 0)),
                  pl.BlockSpec((None, D, TN), lambda l, j: (l, 0, j)),
                  pl.BlockSpec((None, 1, TN), lambda l, j: (l, 0, j))],
        out_specs=pl.BlockSpec((None, Bc, TN), lambda l, j: (l, 0, j)),
        out_shape=jax.ShapeDtypeStruct((L, Bc, N), F32),
        compiler_params=_cparams(("parallel", "parallel")),
        name="ada",
    )(c_all, ada_w, ada_b.reshape(L, 1, N))


def _in_kernel(x_ref, nw_ref, sh_ref, sc_ref, wa_ref, wb_ref, wg_ref, z_ref, zg_ref, h_scr, *, na):
    j = pl.program_id(1)

    @pl.when(j == 0)
    def _():
        h = _norm_mod(x_ref[...], nw_ref[...], sc_ref[...], sh_ref[...])
        hb = h.reshape(h_scr.shape).astype(BF16)
        h_scr[...] = hb
        zg_ref[...] = _dot(hb, wg_ref[...])

    @pl.when(j < na)
    def _():
        z_ref[...] = _dot(h_scr[...], wa_ref[...])

    @pl.when(j >= na)
    def _():
        z_ref[...] = _dot(h_scr[...], wb_ref[...])


def _in_proj(x, mod, l, moff, nw, wa, wb, wg):
    B, T, D = x.shape
    bb, tt = _tile_cfg(B, T, 512)
    TM, nT = bb * tt, T // tt
    nI = (B // bb) * nT
    TN = 1024
    na = wa.shape[1] // TN
    N = wa.shape[1] + wb.shape[1]
    xmap = lambda i, j: (i // nT, i % nT, 0)
    mmap = lambda k: (lambda i, j: (l, moff // bb + i // nT, 0, k))
    return pl.pallas_call(
        functools.partial(_in_kernel, na=na),
        grid=(nI, N // TN),
        in_specs=[pl.BlockSpec((bb, tt, D), xmap),
                  pl.BlockSpec((1, D), lambda i, j: (0, 0)),
                  pl.BlockSpec((None, bb, 1, D), mmap(0)),
                  pl.BlockSpec((None, bb, 1, D), mmap(1)),
                  pl.BlockSpec((D, TN), lambda i, j: (0, jnp.minimum(j, na - 1))),
                  pl.BlockSpec((D, TN), lambda i, j: (0, jnp.maximum(j - na, 0))),
                  pl.BlockSpec((D, LANES), lambda i, j: (0, 0))],
        out_specs=[pl.BlockSpec((TM, TN), lambda i, j: (i, j)),
                   pl.BlockSpec((TM, LANES), lambda i, j: (i, 0))],
        out_shape=[jax.ShapeDtypeStruct((B * T, N), F32),
                   jax.ShapeDtypeStruct((B * T, LANES), F32)],
        scratch_shapes=[pltpu.VMEM((TM, D), BF16)],
        compiler_params=_cparams(("parallel", "arbitrary")),
        name="in_proj",
    )(x, nw, mod, mod, wa, wb, wg)


def _ret_kernel(zq_ref, zk_ref, zv_ref, zg_ref, cq_ref, sq_ref, ck_ref, sk_ref, intra_ref, cross_ref,
                kdec_ref, cdec_ref, gn_ref, s0_ref, o_ref, s_ref, *, bb, L):
    @pl.when(pl.program_id(1) == 0)
    def _():
        s_ref[...] = s0_ref[...]

    cq, sq, ck, sk = cq_ref[...], sq_ref[...], ck_ref[...], sk_ref[...]

    def group(bis):
        chains = [(i, h) for i in range(len(bis)) for h in range(HEADS)]
        rows = [_rows(bi * L, L) for bi in bis]
        cols = [slice(h * DH, (h + 1) * DH) for h in range(HEADS)]
        q, k, v, s = {}, {}, {}, {}
        for c in chains:
            i, h = c
            qf = zq_ref[rows[i], cols[h]]
            kf = zk_ref[rows[i], cols[h]]
            q[c] = (qf * cq + pltpu.roll(qf, DH // 2, 1) * sq).astype(BF16)
            k[c] = kf * ck + pltpu.roll(kf, DH // 2, 1) * sk
            v[c] = zv_ref[rows[i], cols[h]].astype(BF16)
            s[c] = s_ref[bis[i], h]
        att = {c: _dot_nt(q[c], k[c].astype(BF16)) for c in chains}
        inter = {c: _dot(q[c], s[c].astype(BF16)) for c in chains}
        upd = {c: _dot_tn((k[c] * kdec_ref[c[1]]).astype(BF16), v[c]) for c in chains}
        o = {c: _dot((att[c] * intra_ref[c[1]]).astype(BF16), v[c]) + inter[c] * cross_ref[c[1]] for c in chains}
        for c in chains:
            i, h = c
            s_ref[bis[i], h] = cdec_ref[h] * s[c] + upd[c]
            o_ref[rows[i], cols[h]] = _head_norm(o[c]) * gn_ref[:, cols[h]] * _silu(zg_ref[rows[i], cols[h]])

    _for_each_group(bb, RET_GROUP, group)


def _retention(z, B, T, s0, tabs, gn_w):
    L = CHUNK if T % CHUNK == 0 else T
    nC = T // L
    bb = 1 if nC > 1 else min(B, 8)
    R = bb * L
    W = HEADS * DH
    cq, sq, ck, sk, intra, cross, kdec, cdec = tabs
    zspec = lambda k: pl.BlockSpec((R, W), lambda i, c: (i * nC + c, k))
    tspec = pl.BlockSpec((L, DH), lambda i, c: (c, 0))
    full = lambda a: pl.BlockSpec(a.shape, lambda i, c: (0,) * a.ndim)
    sspec = pl.BlockSpec((bb, HEADS, DH, DH), lambda i, c: (i, 0, 0, 0))
    return pl.pallas_call(
        functools.partial(_ret_kernel, bb=bb, L=L),
        grid=(B // bb, nC),
        in_specs=[zspec(0), zspec(1), zspec(2), zspec(3), tspec, tspec, tspec, tspec,
                  full(intra), full(cross), full(kdec), full(cdec), full(gn_w), sspec],
        out_specs=[pl.BlockSpec((R, W), lambda i, c: (i * nC + c, 0)), sspec],
        out_shape=[jax.ShapeDtypeStruct((B * T, W), F32),
                   jax.ShapeDtypeStruct((B, HEADS, DH, DH), F32)],
        compiler_params=_cparams(("parallel", "arbitrary")),
        name="ret",
    )(z, z, z, z, cq, sq, ck, sk, intra, cross, kdec, cdec, gn_w, s0)


def _ret_tables(T, pos0):
    L = CHUNK if T % CHUNK == 0 else T
    half = DH // 2
    inv = jnp.exp(-math.log(ROPE_BASE) * jnp.arange(half, dtype=F32) / half)
    pos = jnp.arange(T, dtype=F32) + pos0
    ang = pos[:, None] * inv[None, :]
    cos, sin = jnp.cos(ang), jnp.sin(ang)
    c = jnp.concatenate([cos, cos], axis=-1)
    s = jnp.concatenate([-sin, sin], axis=-1)
    kscale = DH ** -0.5
    lg = jnp.log1p(-jnp.exp2(-5.0 - jnp.arange(HEADS, dtype=F32)))
    j = jnp.arange(L, dtype=F32)
    diff = j[:, None] - j[None, :]
    intra = jnp.where(diff >= 0, jnp.exp(lg[:, None, None] * jnp.maximum(diff, 0.0)), 0.0)
    cross = jnp.exp(lg[:, None] * (j + 1.0))
    kdec = jnp.exp(lg[:, None] * (L - 1.0 - j))
    cdec = jnp.exp(lg * L)
    bl = lambda a: jnp.broadcast_to(a[..., None], a.shape + (DH,))
    return (c, s, c * kscale, s * kscale, intra, bl(cross), bl(kdec), bl(cdec[:, None]))


def _mlstm_kernel(zu_ref, zv_ref, zo_ref, zg_ref, cw_ref, cb_ref, wqk_ref, gb_ref, gn_ref, sk_ref,
                  c0_ref, n0_ref, m0_ref, conv0_ref,
                  o_ref, c_ref, n_ref, m_ref, conv_ref, xp_scr, tail_scr, ca_scr, *, bb, L, last):
    ci = pl.program_id(1)

    @pl.when(ci == 0)
    def _():
        c_ref[...] = c0_ref[...]
        n_ref[...] = n0_ref[...]
        m_ref[...] = m0_ref[...]
        tail_scr[...] = jnp.zeros(tail_scr.shape, F32)
        tail_scr[:, SUBLANES - (CONV_W - 1):, :] = conv0_ref[...]

    ri = lax.broadcasted_iota(jnp.int32, (L, L), 0)
    rj = lax.broadcasted_iota(jnp.int32, (L, L), 1)
    causal = rj <= ri
    tri = causal.astype(F32)
    lane = lax.broadcasted_iota(jnp.int32, (L, LANES), 1)
    kscale = DH ** -0.5

    def group(bis):
        ns = len(bis)
        rows = [_rows(bi * L, L) for bi in bis]
        cols = [slice(h * DH, (h + 1) * DH) for h in range(HEADS)]
        chains = [(i, h) for i in range(ns) for h in range(HEADS)]
        gates, csum, gates_t, csum_t, m_all, n_all = [], [], [], [], [], []
        for i, bi in enumerate(bis):
            xp_scr[bi, 0:SUBLANES, :] = tail_scr[bi]
            xp_scr[bi, SUBLANES:, :] = zu_ref[rows[i], :]
            cu = cb_ref[...]
            for t in range(CONV_W):
                cu = cu + xp_scr[bi, pl.ds(SUBLANES - (CONV_W - 1) + t, L), :] * cw_ref[pl.ds(t, 1), :]
            tail_scr[bi] = xp_scr[bi, L:L + SUBLANES, :]
            ca_scr[bi] = _silu(cu)
            gz = zg_ref[rows[i], :] + gb_ref[...]
            g = jnp.where(lane < HEADS, gz, jax.nn.log_sigmoid(gz))
            cs = jnp.dot(tri, g, preferred_element_type=F32, precision=lax.Precision.HIGHEST)
            gates.append(g)
            csum.append(cs)
            gates_t.append(g.T)
            csum_t.append(cs.T)
            m_all.append(m_ref[bi])
            n_all.append(n_ref[bi])
        ca = {c: ca_scr[bis[c[0]], :, cols[c[1]]] for c in chains}
        qk = {c: _dot(ca[c].astype(BF16), wqk_ref[c[1]]) for c in chains}
        q = {c: qk[c][:, :DH] for c in chains}
        k = {c: qk[c][:, DH:] * kscale for c in chains}
        qb = {c: q[c].astype(BF16) for c in chains}
        v = {c: zv_ref[rows[c[0]], cols[c[1]]].astype(BF16) for c in chains}
        c_prev = {c: c_ref[bis[c[0]], c[1]] for c in chains}
        sraw = {c: _dot_nt(qb[c], k[c].astype(BF16)) for c in chains}
        qc = {c: _dot(qb[c], c_prev[c].astype(BF16)) for c in chains}
        s, mt, w_int, m_prev, n_prev, b_col, i_col = {}, {}, {}, {}, {}, {}, {}
        for c in chains:
            i, h = c
            i_col[c] = gates[i][:, h:h + 1]
            b_col[c] = csum[i][:, HEADS + h:HEADS + h + 1]
            i_row = gates_t[i][h:h + 1, :]
            b_row = csum_t[i][HEADS + h:HEADS + h + 1, :]
            m_prev[c] = m_all[i][h:h + 1, :1]
            n_prev[c] = n_all[i][h:h + 1, :]
            dlog = jnp.where(causal, b_col[c] - b_row + i_row, -jnp.inf)
            inter = b_col[c] + m_prev[c]
            mt[c] = jnp.maximum(inter, jnp.max(dlog, axis=-1, keepdims=True))
            s[c] = sraw[c] * jnp.exp(dlog - mt[c])
            w_int[c] = jnp.exp(inter - mt[c])
        sv = {c: _dot(s[c].astype(BF16), v[c]) for c in chains}
        kt, dec, m_new = {}, {}, {}
        for c in chains:
            m_new[c] = mt[c][L - 1:L, :]
            b_last = b_col[c][L - 1:L, :]
            tail = jnp.exp(b_last - b_col[c] + i_col[c] - m_new[c])
            dec[c] = jnp.exp(b_last + m_prev[c] - m_new[c])
            kt[c] = k[c] * tail
        upd = {c: _dot_tn(kt[c].astype(BF16), v[c]) for c in chains}
        n_rows = [[] for _ in range(ns)]
        m_rows = [[] for _ in range(ns)]
        for c in chains:
            i, h = c
            num = sv[c] + w_int[c] * qc[c]
            den = (jnp.sum(s[c], axis=-1, keepdims=True)
                   + w_int[c] * jnp.sum(q[c] * n_prev[c], axis=-1, keepdims=True))
            hh = num / jnp.maximum(jnp.abs(den), jnp.exp(-mt[c]))
            c_ref[bis[i], h] = dec[c] * c_prev[c] + upd[c]
            n_rows[i].append(dec[c] * n_prev[c] + jnp.sum(kt[c], axis=0, keepdims=True))
            m_rows[i].append(jnp.broadcast_to(m_new[c], (1, LANES)))
            hm = jax.nn.sigmoid(zo_ref[rows[i], cols[h]]) * hh
            o_ref[rows[i], cols[h]] = _head_norm(hm) * gn_ref[:, cols[h]] + sk_ref[:, cols[h]] * ca[c]
        for i, bi in enumerate(bis):
            n_ref[bi] = jnp.concatenate(n_rows[i], axis=0)
            m_ref[bi] = jnp.concatenate(m_rows[i], axis=0)

    _for_each_group(bb, MLSTM_GROUP, group)

    @pl.when(ci == last)
    def _():
        conv_ref[...] = tail_scr[:, SUBLANES - (CONV_W - 1):, :]


def _mlstm(z, zg, B, T, states, cw, cb, wqk, gb, gn_w, skip):
    L = CHUNK if T % CHUNK == 0 else T
    nC = T // L
    bb = 1 if nC > 1 else min(B, 8)
    R = bb * L
    W = HEADS * DH
    c0, n0, m0, conv0 = states
    zspec = lambda k: pl.BlockSpec((R, W), lambda i, c: (i * nC + c, k))
    full = lambda a: pl.BlockSpec(a.shape, lambda i, c: (0,) * a.ndim)
    cspec = pl.BlockSpec((bb, HEADS, DH, DH), lambda i, c: (i, 0, 0, 0))
    nspec = pl.BlockSpec((bb, HEADS, DH), lambda i, c: (i, 0, 0))
    vspec = pl.BlockSpec((bb, CONV_W - 1, W), lambda i, c: (i, 0, 0))
    return pl.pallas_call(
        functools.partial(_mlstm_kernel, bb=bb, L=L, last=nC - 1),
        grid=(B // bb, nC),
        in_specs=[zspec(4), zspec(5), zspec(6), pl.BlockSpec((R, LANES), lambda i, c: (i * nC + c, 0)),
                  full(cw), full(cb), full(wqk), full(gb), full(gn_w), full(skip),
                  cspec, nspec, nspec, vspec],
        out_specs=[pl.BlockSpec((R, W), lambda i, c: (i * nC + c, 0)), cspec, nspec, nspec, vspec],
        out_shape=[jax.ShapeDtypeStruct((B * T, W), F32),
                   jax.ShapeDtypeStruct((B, HEADS, DH, DH), F32),
                   jax.ShapeDtypeStruct((B, HEADS, DH), F32),
                   jax.ShapeDtypeStruct((B, HEADS, LANES), F32),
                   jax.ShapeDtypeStruct((B, CONV_W - 1, W), F32)],
        scratch_shapes=[pltpu.VMEM((bb, L + SUBLANES, W), F32),
                        pltpu.VMEM((bb, SUBLANES, W), F32),
                        pltpu.VMEM((bb, L, W), F32)],
        compiler_params=_cparams(("parallel", "arbitrary")),
        name="mlstm",
    )(z, z, z, zg, cw, cb, wqk, gb, gn_w, skip, c0, n0, m0, conv0)


def _s5_kernel(zu_ref, bw_ref, ar_ref, ai_ref, pr_ref, pi_ref, cwr_ref, cwi_ref, d_ref, wglu_ref, bglu_ref,
               h0r_ref, h0i_ref, o_ref, hr_ref, hi_ref, xr_scr, xi_scr, *, bb, Lc):
    @pl.when(pl.program_id(1) == 0)
    def _():
        hr_ref[...] = h0r_ref[...]
        hi_ref[...] = h0i_ref[...]

    NB = xr_scr.shape[0]
    KB = bw_ref.shape[0]
    BPK = NB // KB
    u = zu_ref[...]
    ub = u.astype(BF16)
    for kb in range(KB):
        r = _dot(ub[:, kb * LANES:(kb + 1) * LANES], bw_ref[kb])
        for t in range(BPK):
            xr_scr[kb * BPK + t] = r[:, t * LANES:(t + 1) * LANES]
            xi_scr[kb * BPK + t] = r[:, (BPK + t) * LANES:(BPK + t + 1) * LANES]

    def cmul_add(a_r, a_i, x_r, x_i, b_r, b_i):
        return a_r * x_r - a_i * x_i + b_r, a_r * x_i + a_i * x_r + b_i

    def scan8(x_r, x_i, cb):
        cl = slice(cb * LANES, (cb + 1) * LANES)
        for lv in range(3):
            x_r, x_i = cmul_add(ar_ref[lv, :, cl], ai_ref[lv, :, cl], pltpu.roll(x_r, 1 << lv, 0),
                                pltpu.roll(x_i, 1 << lv, 0), x_r, x_i)
        return x_r, x_i

    def bcast(row):
        return jnp.broadcast_to(row, (SUBLANES, LANES))

    def seq(bi):
        def grp(gi, carry):
            rows = _rows(bi * Lc + gi * SUBLANES, SUBLANES)
            for cb in range(NB):
                cl = slice(cb * LANES, (cb + 1) * LANES)
                x_r, x_i = scan8(xr_scr[cb, rows, :], xi_scr[cb, rows, :], cb)
                h_r, h_i = cmul_add(pr_ref[:, cl], pi_ref[:, cl], bcast(hr_ref[bi, :, cl]),
                                    bcast(hi_ref[bi, :, cl]), x_r, x_i)
                xr_scr[cb, rows, :] = h_r
                xi_scr[cb, rows, :] = h_i
                hr_ref[bi, :, cl] = h_r[SUBLANES - 1:, :]
                hi_ref[bi, :, cl] = h_i[SUBLANES - 1:, :]
            return carry
        lax.fori_loop(0, Lc // SUBLANES, grp, 0)

    _for_each_seq(bb, seq)

    ys = []
    for kb in range(KB):
        hrb = jnp.concatenate([xr_scr[kb * BPK + t] for t in range(BPK)], axis=1).astype(BF16)
        hib = jnp.concatenate([xi_scr[kb * BPK + t] for t in range(BPK)], axis=1).astype(BF16)
        ys.append(_dot(hrb, cwr_ref[kb]) + _dot(hib, cwi_ref[kb]))
    y = jnp.concatenate(ys, axis=1) + d_ref[...] * u
    g = jax.nn.gelu(y)
    o_ref[...] = g * jax.nn.sigmoid(_dot(g.astype(BF16), wglu_ref[...]) + bglu_ref[...])


def _s5(z, B, T, h0r, h0i, tabs, d, wglu, bglu):
    Lc = min(T, 256)
    nC = T // Lc
    bb = 1 if nC > 1 else min(B, max(1, 128 // T))
    bw, ar, ai, pr, pi, cwr, cwi = tabs
    R = bb * Lc
    W = d.shape[1]
    NS = h0r.shape[-1]
    full = lambda a: pl.BlockSpec(a.shape, lambda i, c: (0,) * a.ndim)
    hspec = pl.BlockSpec((bb, 1, NS), lambda i, c: (i, 0, 0))
    return pl.pallas_call(
        functools.partial(_s5_kernel, bb=bb, Lc=Lc),
        grid=(B // bb, nC),
        in_specs=[pl.BlockSpec((R, W), lambda i, c: (i * nC + c, 7)),
                  full(bw), full(ar), full(ai), full(pr), full(pi), full(cwr), full(cwi),
                  full(d), full(wglu), full(bglu), hspec, hspec],
        out_specs=[pl.BlockSpec((R, W), lambda i, c: (i * nC + c, 0)), hspec, hspec],
        out_shape=[jax.ShapeDtypeStruct((B * T, W), F32),
                   jax.ShapeDtypeStruct((B, 1, NS), F32),
                   jax.ShapeDtypeStruct((B, 1, NS), F32)],
        scratch_shapes=[pltpu.VMEM((NS // LANES, R, LANES), F32), pltpu.VMEM((NS // LANES, R, LANES), F32)],
        compiler_params=_cparams(("parallel", "arbitrary")),
        name="s5",
    )(z, bw, ar, ai, pr, pi, cwr, cwi, d, wglu, bglu, h0r, h0i)


def _s5_tables(a_re, a_im, log_dt, b_re, b_im, c_re, c_im):
    G, P = a_re.shape
    dt = jnp.exp(log_dt.astype(F32))[:, None]
    lam_re = -jnp.abs(a_re.astype(F32))
    lam_im = a_im.astype(F32)

    def power(n):
        n = jnp.asarray(n, F32)[:, None, None]
        mag = jnp.exp(lam_re * dt * n)
        return ((mag * jnp.cos(lam_im * dt * n)).reshape(-1, G * P),
                (mag * jnp.sin(lam_im * dt * n)).reshape(-1, G * P))

    mag = jnp.exp(lam_re * dt)
    ab_re = mag * jnp.cos(lam_im * dt)
    ab_im = mag * jnp.sin(lam_im * dt)
    den = lam_re * lam_re + lam_im * lam_im
    co_re = ((ab_re - 1.0) * lam_re + ab_im * lam_im) / den
    co_im = (ab_im * lam_re - (ab_re - 1.0) * lam_im) / den
    br, bi = b_re.astype(F32), b_im.astype(F32)
    bb_re = co_re[..., None] * br - co_im[..., None] * bi
    bb_im = co_re[..., None] * bi + co_im[..., None] * br
    gpb = LANES // S5_GROUP
    KB = G // gpb
    eye = jnp.eye(gpb, dtype=F32)
    blk = lambda w: jnp.einsum('kgpc,gh->kgchp', w.reshape(KB, gpb, P, S5_GROUP), eye).reshape(KB, LANES, gpb * P)
    bw = jnp.concatenate([blk(bb_re), blk(bb_im)], axis=-1).astype(BF16)
    cblk = lambda w: jnp.einsum('kgcp,gh->kgphc', w.reshape(KB, gpb, S5_GROUP, P), eye).reshape(KB, gpb * P, LANES)
    cwr = cblk(c_re.astype(F32)).astype(BF16)
    cwi = cblk(-c_im.astype(F32)).astype(BF16)
    rows = jnp.arange(SUBLANES)
    ars, ais = [], []
    for lv in range(3):
        r, i = power([1 << lv])
        keep = (rows >= (1 << lv))[:, None]
        ars.append(jnp.where(keep, r, 0.0))
        ais.append(jnp.where(keep, i, 0.0))
    pr, pi = power(np.arange(1, SUBLANES + 1))
    return bw, jnp.stack(ars), jnp.stack(ais), pr, pi, cwr, cwi


def _branch_kernel(or_ref, om_ref, os_ref, gr_ref, gm_ref, gs_ref, wr_ref, wm_ref, ws_ref, o_ref):
    br = _dot(or_ref[...].astype(BF16), wr_ref[...])
    bm = _dot(om_ref[...].astype(BF16), wm_ref[...])
    bs = _dot(os_ref[...].astype(BF16), ws_ref[...])
    merged = (jax.nn.sigmoid(gr_ref[...]) * br + jax.nn.sigmoid(gm_ref[...]) * bm
              + jax.nn.sigmoid(gs_ref[...]) * bs)
    o_ref[...] = merged.astype(BF16)


def _branch(o_r, o_m, o_s, z, wr, wm, ws):
    M, W = o_r.shape
    D = wr.shape[1]
    TM, TN = min(M, 512), 1024
    nJ = D // TN
    ospec = pl.BlockSpec((TM, W), lambda i, j: (i, 0))
    gspec = lambda k: pl.BlockSpec((TM, TN), lambda i, j: (i, 8 + k * nJ + j))
    wspec = pl.BlockSpec((W, TN), lambda i, j: (0, j))
    return pl.pallas_call(
        _branch_kernel,
        grid=(M // TM, nJ),
        in_specs=[ospec, ospec, ospec, gspec(0), gspec(1), gspec(2), wspec, wspec, wspec],
        out_specs=pl.BlockSpec((TM, TN), lambda i, j: (i, j)),
        out_shape=jax.ShapeDtypeStruct((M, D), BF16),
        compiler_params=_cparams(("parallel", "parallel")),
        name="branch",
    )(o_r, o_m, o_s, z, z, z, wr, wm, ws)


def _wo_kernel(m_ref, w_ref, x_ref, g_ref, o_ref):
    y = _dot(m_ref[...], w_ref[...])
    o_ref[...] = x_ref[...] + g_ref[...] * y.reshape(o_ref.shape)


def _wo(merged, w_o, x, mod, l, moff):
    B, T, D = x.shape
    bb, tt = _tile_cfg(B, T, 512)
    TM, nT = bb * tt, T // tt
    TN = 1024
    xspec = pl.BlockSpec((bb, tt, TN), lambda i, j: (i // nT, i % nT, j))
    return pl.pallas_call(
        _wo_kernel,
        grid=((B // bb) * nT, D // TN),
        in_specs=[pl.BlockSpec((TM, D), lambda i, j: (i, 0)),
                  pl.BlockSpec((D, TN), lambda i, j: (0, j)),
                  xspec,
                  pl.BlockSpec((None, bb, 1, TN), lambda i, j: (l, moff // bb + i // nT, 0, 2 * (D // TN) + j))],
        out_specs=xspec,
        out_shape=jax.ShapeDtypeStruct((B, T, D), F32),
        compiler_params=_cparams(("parallel", "parallel")),
        name="wo",
    )(merged, w_o, x, mod)


def _prep_u_kernel(u_ref, o_ref):
    kq = o_ref.shape[-1]
    for j in range(o_ref.shape[0]):
        o_ref[j] = u_ref[:, j * kq:(j + 1) * kq].astype(BF16)


def _prep_u(peer_u):
    L, NE, D = peer_u.shape
    R = 2 * PEER_MC
    kq = D // PEER_KQ
    return pl.pallas_call(
        _prep_u_kernel,
        grid=(L, NE // R),
        in_specs=[pl.BlockSpec((None, R, D), lambda l, p: (l, p, 0))],
        out_specs=pl.BlockSpec((None, None, PEER_KQ, R, kq), lambda l, p: (l, p, 0, 0, 0)),
        out_shape=jax.ShapeDtypeStruct((L, NE // R, PEER_KQ, R, kq), BF16),
        compiler_params=_cparams(("parallel", "parallel")),
        name="prep_u",
    )(peer_u)


def _prep_v_kernel(v_ref, o_ref):
    kq = o_ref.shape[1]
    for j in range(o_ref.shape[0]):
        o_ref[j] = v_ref[:, j * kq:(j + 1) * kq].T.astype(BF16)


def _prep_v(peer_v):
    L, NE, D = peer_v.shape
    kq = D // PEER_KQ
    return pl.pallas_call(
        _prep_v_kernel,
        grid=(L, NE // PEER_MC),
        in_specs=[pl.BlockSpec((None, PEER_MC, D), lambda l, c: (l, c, 0))],
        out_specs=pl.BlockSpec((None, None, PEER_KQ, kq, PEER_MC), lambda l, c: (l, c, 0, 0, 0)),
        out_shape=jax.ShapeDtypeStruct((L, NE // PEER_MC, PEER_KQ, kq, PEER_MC), BF16),
        compiler_params=_cparams(("parallel", "parallel")),
        name="prep_v",
    )(peer_v)


def _peer_q_kernel(x_ref, nw_ref, sh_ref, sc_ref, wq_ref, k1_ref, k2_ref, h_ref, st_ref):
    h = _norm_mod(x_ref[...], nw_ref[...], sc_ref[...], sh_ref[...])
    hb = h.reshape(-1, h.shape[-1]).astype(BF16)
    kq = h_ref.shape[-1]
    for j in range(h_ref.shape[0]):
        h_ref[j] = hb[:, j * kq:(j + 1) * kq]
    q = _dot(hb, wq_ref[...])
    dq = k1_ref.shape[-1]
    for hd in range(PEER_HEADS):
        q1 = q[:, (2 * hd) * dq:(2 * hd + 1) * dq]
        q2 = q[:, (2 * hd + 1) * dq:(2 * hd + 2) * dq]
        st_ref[2 * hd] = _dot_nt(k1_ref[hd], q1, precision=lax.Precision.HIGHEST)
        st_ref[2 * hd + 1] = _dot_nt(k2_ref[hd], q2, precision=lax.Precision.HIGHEST)


def _peer_q(x, mod, l, moff, nw, wq, k1, k2):
    B, T, D = x.shape
    bb, tt = _tile_cfg(B, T, 256)
    TM, nT = bb * tt, T // tt
    full = lambda a: pl.BlockSpec(a.shape, lambda i: (0,) * a.ndim)
    mmap = lambda k: (lambda i: (l, moff // bb + i // nT, 0, k))
    return pl.pallas_call(
        _peer_q_kernel,
        grid=((B // bb) * nT,),
        in_specs=[pl.BlockSpec((bb, tt, D), lambda i: (i // nT, i % nT, 0)),
                  pl.BlockSpec((1, D), lambda i: (0, 0)),
                  pl.BlockSpec((None, bb, 1, D), mmap(3)),
                  pl.BlockSpec((None, bb, 1, D), mmap(4)),
                  full(wq), full(k1), full(k2)],
        out_specs=[pl.BlockSpec((PEER_KQ, TM, D // PEER_KQ), lambda i: (0, i, 0)),
                   pl.BlockSpec((2 * PEER_HEADS, N_KEYS, TM), lambda i: (0, 0, i))],
        out_shape=[jax.ShapeDtypeStruct((PEER_KQ, B * T, D // PEER_KQ), BF16),
                   jax.ShapeDtypeStruct((2 * PEER_HEADS, N_KEYS, B * T), F32)],
        compiler_params=_cparams(("parallel",)),
        name="peer_q",
    )(x, nw, mod, mod, wq, k1, k2)


def _top_values(cur, n):
    vals = []
    for _ in range(n):
        m = jnp.max(cur, axis=0, keepdims=True)
        vals.append(m)
        cur = jnp.where(cur == m, -jnp.inf, cur)
    return vals


def _peer_sel_kernel(st_ref, e1_ref, e2_ref, th_ref):
    ts = st_ref.shape[-1]
    neg = jnp.full((1, ts), -jnp.inf, F32)
    n = PEER_TOPK + 1
    ths = []
    for hd in range(PEER_HEADS):
        s1 = st_ref[2 * hd]
        s2 = st_ref[2 * hd + 1]
        v1 = _top_values(s1, n)
        v2 = _top_values(s2, n)
        rows = [v1[r] + v2[c] for r in range(n) for c in range(n // (r + 1))]
        rows += [neg] * (-len(rows) % SUBLANES)
        top = _top_values(jnp.concatenate(rows, axis=0), n)
        den = jnp.exp(top[0] - top[0])
        for t in top[1:PEER_TOPK]:
            den = den + jnp.exp(t - top[0])
        e1_ref[hd] = jnp.exp(s1 - v1[0]) / den
        e2_ref[hd] = jnp.exp(s2 - v2[0])
        ths.append(jnp.exp(0.5 * (top[PEER_TOPK - 1] + top[PEER_TOPK]) - top[0]) / den)
    th_ref[...] = jnp.concatenate(ths, axis=0)


def _peer_sel(st):
    H2, K, M = st.shape
    TS = min(M, 256)
    hspec = pl.BlockSpec((PEER_HEADS, K, TS), lambda i: (0, 0, i))
    return pl.pallas_call(
        _peer_sel_kernel,
        grid=(M // TS,),
        in_specs=[pl.BlockSpec((H2, K, TS), lambda i: (0, 0, i))],
        out_specs=[hspec, hspec, pl.BlockSpec((PEER_HEADS, TS), lambda i: (0, i))],
        out_shape=[jax.ShapeDtypeStruct((PEER_HEADS, K, M), F32),
                   jax.ShapeDtypeStruct((PEER_HEADS, K, M), F32),
                   jax.ShapeDtypeStruct((PEER_HEADS, M), F32)],
        compiler_params=_cparams(("parallel",)),
        name="peer_sel",
    )(st)


def _peer_exp_kernel(h_ref, u_ref, vta_ref, vtb_ref, e1_ref, e2_ref, th_ref, x_ref, g_ref, o_ref,
                     acc_scr, sc0, sc1, zt0, zt1, *, mc, nc):
    s = pl.program_id(1)
    slabs = mc // N_KEYS

    @pl.when(s == 0)
    def _():
        acc_scr[...] = jnp.zeros(acc_scr.shape, F32)
        sc1[...] = jnp.zeros(sc1.shape, F32)
        zt0[...] = jnp.zeros(zt0.shape, BF16)
        zt1[...] = jnp.zeros(zt1.shape, BF16)

    nq = u_ref.shape[0]
    spp = slabs // nq
    assert spp * nq == slabs

    def substep(half, sc_a, sc_b, zt_b, zt_c, chunk_b):
        c = jnp.clip(chunk_b, 0, nc - 1)
        cols = slice(half * mc, (half + 1) * mc)
        sc_a[...] = jnp.zeros(sc_a.shape, F32)

        def piece(j, carry):
            acc_scr[j] += _dot((vta_ref, vtb_ref)[half][j], zt_c[...])
            for sl in range(spp):
                rs = pl.ds(pl.multiple_of((j * spp + sl) * N_KEYS, N_KEYS), N_KEYS)
                i1 = c * slabs + j * spp + sl
                for tc in range(sc_b.shape[1] // LANES):
                    tl = slice(tc * LANES, (tc + 1) * LANES)
                    w = jnp.zeros((N_KEYS, LANES), F32)
                    for hd in range(PEER_HEADS):
                        p = e1_ref[hd, pl.ds(i1, 1), :][:, tl] * e2_ref[hd, :, tl]
                        w = jnp.where(p >= th_ref[pl.ds(hd, 1), tl], w + p, w)
                    zt_b[rs, tl] = (jax.nn.gelu(sc_b[rs, tl]) * w).astype(BF16)
            sc_a[...] += _dot_nt(u_ref[j, cols, :], h_ref[j])
            return carry

        lax.fori_loop(0, nq, piece, 0)

    substep(0, sc0, sc1, zt1, zt0, 2 * s - 1)
    substep(1, sc1, sc0, zt0, zt1, 2 * s)

    @pl.when(s == pl.num_programs(1) - 1)
    def _():
        o_ref[...] = x_ref[...] + g_ref[...] * acc_scr[...].reshape(-1, acc_scr.shape[-1]).T.reshape(o_ref.shape)


def _peer_exp(hb, u, vt, e1, e2, th, x, mod, l, moff):
    B, T, D = x.shape
    bb, tt = _tile_cfg(B, T, PEER_TB)
    TB, nT = bb * tt, T // tt
    MC = PEER_MC
    NP = u.shape[0]
    NC = vt.shape[0]
    assert NC == 2 * NP and u.shape[2] == 2 * MC and vt.shape[3] == MC
    xspec = pl.BlockSpec((bb, tt, D), lambda i, s: (i // nT, i % nT, 0))
    hspec = pl.BlockSpec((PEER_HEADS, N_KEYS, TB), lambda i, s: (0, 0, i))
    return pl.pallas_call(
        functools.partial(_peer_exp_kernel, mc=MC, nc=NC),
        grid=((B // bb) * nT, NP + 1),
        in_specs=[pl.BlockSpec((PEER_KQ, TB, D // PEER_KQ), lambda i, s: (0, i, 0)),
                  pl.BlockSpec((None, PEER_KQ, 2 * MC, D // PEER_KQ),
                               lambda i, s: (jnp.minimum(s, NP - 1), 0, 0, 0)),
                  pl.BlockSpec((None, PEER_KQ, D // PEER_KQ, MC),
                               lambda i, s: (2 * jnp.maximum(s - 1, 0), 0, 0, 0)),
                  pl.BlockSpec((None, PEER_KQ, D // PEER_KQ, MC),
                               lambda i, s: (2 * jnp.maximum(s - 1, 0) + 1, 0, 0, 0)),
                  hspec, hspec,
                  pl.BlockSpec((PEER_HEADS, TB), lambda i, s: (0, i)),
                  xspec,
                  pl.BlockSpec((None, bb, 1, D), lambda i, s: (l, moff // bb + i // nT, 0, 5))],
        out_specs=xspec,
        out_shape=jax.ShapeDtypeStruct((B, T, D), F32),
        scratch_shapes=[pltpu.VMEM((PEER_KQ, D // PEER_KQ, TB), F32),
                        pltpu.VMEM((MC, TB), F32), pltpu.VMEM((MC, TB), F32),
                        pltpu.VMEM((MC, TB), BF16), pltpu.VMEM((MC, TB), BF16)],
        compiler_params=_cparams(("parallel", "arbitrary")),
        name="peer_exp",
    )(hb, u, vt, vt, e1, e2, th, x, mod)


def _final_kernel(x_ref, w_ref, o_ref):
    x = x_ref[...]
    ms = jnp.mean(x * x, axis=-1, keepdims=True)
    o_ref[...] = x * lax.rsqrt(ms + EPS) * w_ref[...]


def _final_norm(x, w):
    B, T, D = x.shape
    bb, tt = _tile_cfg(B, T, 512)
    nT = T // tt
    xspec = pl.BlockSpec((bb, tt, D), lambda i: (i // nT, i % nT, 0))
    return pl.pallas_call(
        _final_kernel,
        grid=((B // bb) * nT,),
        in_specs=[xspec, pl.BlockSpec((1, D), lambda i: (0, 0))],
        out_specs=xspec,
        out_shape=jax.ShapeDtypeStruct((B, T, D), F32),
        compiler_params=_cparams(("parallel",)),
        name="final_norm",
    )(x, w)


def _trunk(x, mod, moff, pos0, states, lw, final_w):
    B, T, D = x.shape
    rtabs = _ret_tables(T, pos0)
    new = []
    for l, (st, w) in enumerate(zip(states, lw)):
        s_ret, s_c, s_n, s_m, s_conv, s_hr, s_hi = st
        z, zg = _in_proj(x, mod, l, moff, w['n1'], w['w_a'], w['w_b'], w['w_gate'])
        o_r, n_ret = _retention(z, B, T, s_ret, rtabs, w['ret_gn'])
        o_m, n_c, n_n, n_m, n_conv = _mlstm(z, zg, B, T, (s_c, s_n, s_m, s_conv), w['conv_w'], w['conv_b'],
                                            w['wqk'], w['gate_b'], w['m_gn'], w['m_skip'])
        o_s, n_hr, n_hi = _s5(z, B, T, s_hr, s_hi, w['s5_tabs'], w['s5_d'], w['s5_wglu'], w['s5_bglu'])
        merged = _branch(o_r, o_m, o_s, z, w['w_ret_out'], w['w_mlstm_out'], w['w_s5_out'])
        x = _wo(merged, w['w_o'], x, mod, l, moff)
        hb, sc = _peer_q(x, mod, l, moff, w['n2'], w['peer_wq'], w['peer_k1'], w['peer_k2'])
        e1, e2, th = _peer_sel(sc)
        x = _peer_exp(hb, w['peer_u'], w['peer_vt'], e1, e2, th, x, mod, l, moff)
        new.append((n_ret, n_c, n_n, n_m[..., 0], n_conv, n_hr, n_hi))
    y = _final_norm(x, final_w)
    return y, [jnp.stack([s[i] for s in new]) for i in range(7)]


def kernel(x_prompt, x_sample, state_ret, state_mlstm_c, state_mlstm_n, state_mlstm_m, state_mlstm_conv,
           state_s5_re, state_s5_im, c_prompt, c_sample, ada_w, ada_b, norm1_w, norm2_w, final_norm_w,
           w_in, ret_gn_w, w_ret_out, mlstm_conv_w, mlstm_conv_b, mlstm_wq, mlstm_wk, mlstm_b_i, mlstm_b_f,
           mlstm_gn_w, mlstm_skip, w_mlstm_out, s5_a_re, s5_a_im, s5_log_dt, s5_b_re, s5_b_im, s5_c_re,
           s5_c_im, s5_d, s5_w_glu, s5_b_glu, w_s5_out, w_o, peer_wq, peer_k1, peer_k2, peer_u, peer_v):
    depth = w_in.shape[0]
    Bp, Tp, D = x_prompt.shape
    Bs, Ts, _ = x_sample.shape
    W = HEADS * DH
    G, P = s5_a_re.shape[1:]
    NS = G * P
    past_len = 16384

    pad = -(Bs + Bp) % SUBLANES
    c_all = jnp.concatenate([c_sample, c_prompt, jnp.zeros((pad, D), F32)], axis=0)
    mod = _ada(c_all, ada_w, ada_b)
    mod = mod.reshape(depth, c_all.shape[0], 1, ada_w.shape[2])

    a_end = 7 * W
    g_end = a_end + 2 * HEADS
    u_tiles = _prep_u(peer_u)
    vt_tiles = _prep_v(peer_v)
    lw = []
    for l in range(depth):
        w_gate = jnp.pad(w_in[l, :, a_end:g_end], ((0, 0), (0, LANES - 2 * HEADS))).astype(BF16)
        gate_b = jnp.pad(jnp.concatenate([mlstm_b_i[l], mlstm_b_f[l]]), (0, LANES - 2 * HEADS)).reshape(1, LANES)
        lw.append(dict(
            n1=norm1_w[l].reshape(1, D), n2=norm2_w[l].reshape(1, D),
            w_a=w_in[l, :, :a_end].astype(BF16), w_b=w_in[l, :, g_end:].astype(BF16), w_gate=w_gate, gate_b=gate_b,
            ret_gn=ret_gn_w[l].reshape(1, W),
            conv_w=mlstm_conv_w[l], conv_b=mlstm_conv_b[l].reshape(1, W),
            wqk=jnp.concatenate([mlstm_wq[l], mlstm_wk[l]], axis=-1).astype(BF16),
            m_gn=mlstm_gn_w[l].reshape(1, W), m_skip=mlstm_skip[l].reshape(1, W),
            s5_tabs=_s5_tables(s5_a_re[l], s5_a_im[l], s5_log_dt[l], s5_b_re[l], s5_b_im[l], s5_c_re[l], s5_c_im[l]),
            s5_d=s5_d[l].reshape(1, -1), s5_wglu=s5_w_glu[l].astype(BF16), s5_bglu=s5_b_glu[l].reshape(1, -1),
            w_ret_out=w_ret_out[l].astype(BF16), w_mlstm_out=w_mlstm_out[l].astype(BF16),
            w_s5_out=w_s5_out[l].astype(BF16), w_o=w_o[l].astype(BF16),
            peer_wq=peer_wq[l].astype(BF16), peer_k1=peer_k1[l], peer_k2=peer_k2[l],
            peer_u=u_tiles[l], peer_vt=vt_tiles[l]))

    def zero_states(B):
        return (jnp.zeros((B, HEADS, DH, DH), F32), jnp.zeros((B, HEADS, DH, DH), F32),
                jnp.zeros((B, HEADS, DH), F32), jnp.zeros((B, HEADS, LANES), F32),
                jnp.zeros((B, CONV_W - 1, W), F32), jnp.zeros((B, 1, NS), F32), jnp.zeros((B, 1, NS), F32))

    prompt_states = [zero_states(Bp) for _ in range(depth)]
    sample_states = [(state_ret[l], state_mlstm_c[l], state_mlstm_n[l],
                      jnp.broadcast_to(state_mlstm_m[l][..., None], (Bs, HEADS, LANES)),
                      state_mlstm_conv[l], state_s5_re[l].reshape(Bs, 1, NS), state_s5_im[l].reshape(Bs, 1, NS))
                     for l in range(depth)]

    y_p, ps = _trunk(x_prompt, mod, Bs, 0, prompt_states, lw, final_norm_w.reshape(1, D))
    y_s, ss = _trunk(x_sample, mod, 0, past_len, sample_states, lw, final_norm_w.reshape(1, D))

    def unpack(st, B):
        r, c, n, m, conv, hr, hi = st
        return (r, c, n, m, conv, hr.reshape(depth, B, G, P), hi.reshape(depth, B, G, P))

    return (y_p, y_s) + unpack(ps, Bp) + unpack(ss, Bs)
```

```python
import functools
import math

import jax
import jax.numpy as jnp
import numpy as np
from jax import lax
from jax.experimental import pallas as pl
from jax.experimental.pallas import tpu as pltpu

F32 = jnp.float32
BF16 = jnp.bfloat16

EPS = 1e-6
ROPE_BASE = 10000.0
HEADS = 8
DH = 128
CONV_W = 4
S5_GROUP = 16
S5_STATE = 64
PEER_HEADS = 8
PEER_TOPK = 16
N_KEYS = 128
CHUNK = 128
LANES = 128
SUBLANES = 8
VMEM_LIMIT = 56 * 1024 * 1024
RET_GROUP = 4
MLSTM_GROUP = 2
PEER_TB = 512
PEER_MC = 512
PEER_KQ = 4


def _cparams(sem):
    return pltpu.CompilerParams(dimension_semantics=sem, vmem_limit_bytes=VMEM_LIMIT)


def _tile_cfg(B, T, target):
    if T >= target:
        bb, tt = 1, target
    else:
        bb, tt = min(B, target // T), T
    assert T % tt == 0 and B % bb == 0
    return bb, tt


def _silu(x):
    return x * jax.nn.sigmoid(x)


def _norm_mod(x, w, sc, sh):
    ms = jnp.mean(x * x, axis=-1, keepdims=True)
    y = x * lax.rsqrt(ms + EPS) * w
    return y * (1.0 + sc) + sh


def _head_norm(x):
    mu = jnp.mean(x, axis=-1, keepdims=True)
    xc = x - mu
    var = jnp.mean(xc * xc, axis=-1, keepdims=True)
    return xc * lax.rsqrt(var + EPS)


def _dot(a, b):
    return jnp.dot(a, b, preferred_element_type=F32)


def _dot_nt(a, b, **kw):
    return lax.dot_general(a, b, (((1,), (1,)), ((), ())), preferred_element_type=F32, **kw)


def _dot_tn(a, b):
    return lax.dot_general(a, b, (((0,), (0,)), ((), ())), preferred_element_type=F32)


def _rows(start, n):
    if isinstance(start, int):
        return pl.ds(start, n)
    return pl.ds(pl.multiple_of(start, SUBLANES), n)


def _for_each_group(bb, gs, body):
    gs = min(gs, bb)
    assert bb % gs == 0
    if bb == gs:
        body(list(range(bb)))
    else:
        def f(gi, c):
            body([gi * gs + i for i in range(gs)])
            return c
        lax.fori_loop(0, bb // gs, f, 0)


def _for_each_seq(bb, body, unroll=1):
    if bb == 1:
        body(0)
    else:
        def f(bi, c):
            body(bi)
            return c
        lax.fori_loop(0, bb, f, 0, unroll=unroll)


def _ada_kernel(c_ref, w_ref, b_ref, o_ref):
    a = _silu(c_ref[...]).astype(BF16)
    o_ref[...] = _dot(a, w_ref[...].astype(BF16)) + b_ref[...]


def _ada(c_all, ada_w, ada_b):
    L, D, N = ada_w.shape
    Bc = c_all.shape[0]
    TN = 1024
    return pl.pallas_call(
        _ada_kernel,
        grid=(L, N // TN),
        in_specs=[pl.BlockSpec((Bc, D), lambda l, j: (0, 0)),
                  pl.BlockSpec((None, D, TN), lambda l, j: (l, 0, j)),
                  pl.BlockSpec((None, 1, TN), lambda l, j: (l, 0, j))],
        out_specs=pl.BlockSpec((None, Bc, TN), lambda l, j: (l, 0, j)),
        out_shape=jax.ShapeDtypeStruct((L, Bc, N), F32),
        compiler_params=_cparams(("parallel", "parallel")),
        name="ada",
    )(c_all, ada_w, ada_b.reshape(L, 1, N))


def _in_kernel(x_ref, nw_ref, sh_ref, sc_ref, wa_ref, wb_ref, wg_ref, z_ref, zg_ref, h_scr, *, na):
    j = pl.program_id(1)

    @pl.when(j == 0)
    def _():
        h = _norm_mod(x_ref[...], nw_ref[...], sc_ref[...], sh_ref[...])
        hb = h.reshape(h_scr.shape).astype(BF16)
        h_scr[...] = hb
        zg_ref[...] = _dot(hb, wg_ref[...])

    @pl.when(j < na)
    def _():
        z_ref[...] = _dot(h_scr[...], wa_ref[...])

    @pl.when(j >= na)
    def _():
        z_ref[...] = _dot(h_scr[...], wb_ref[...])


def _in_proj(x, mod, l, moff, nw, wa, wb, wg):
    B, T, D = x.shape
    bb, tt = _tile_cfg(B, T, 512)
    TM, nT = bb * tt, T // tt
    nI = (B // bb) * nT
    TN = 1024
    na = wa.shape[1] // TN
    N = wa.shape[1] + wb.shape[1]
    xmap = lambda i, j: (i // nT, i % nT, 0)
    mmap = lambda k: (lambda i, j: (l, moff // bb + i // nT, 0, k))
    return pl.pallas_call(
        functools.partial(_in_kernel, na=na),
        grid=(nI, N // TN),
        in_specs=[pl.BlockSpec((bb, tt, D), xmap),
                  pl.BlockSpec((1, D), lambda i, j: (0, 0)),
                  pl.BlockSpec((None, bb, 1, D), mmap(0)),
                  pl.BlockSpec((None, bb, 1, D), mmap(1)),
                  pl.BlockSpec((D, TN), lambda i, j: (0, jnp.minimum(j, na - 1))),
                  pl.BlockSpec((D, TN), lambda i, j: (0, jnp.maximum(j - na, 0))),
                  pl.BlockSpec((D, LANES), lambda i, j: (0, 0))],
        out_specs=[pl.BlockSpec((TM, TN), lambda i, j: (i, j)),
                   pl.BlockSpec((TM, LANES), lambda i, j: (i, 0))],
        out_shape=[jax.ShapeDtypeStruct((B * T, N), F32),
                   jax.ShapeDtypeStruct((B * T, LANES), F32)],
        scratch_shapes=[pltpu.VMEM((TM, D), BF16)],
        compiler_params=_cparams(("parallel", "arbitrary")),
        name="in_proj",
    )(x, nw, mod, mod, wa, wb, wg)


def _ret_kernel(*refs, bb, L):
    (zq_ref, zk_ref, zv_ref, zg_ref, cq_ref, sq_ref, ck_ref, sk_ref, intra_ref, cross_ref,
     kdec_ref, cdec_ref, gn_ref, s0_ref) = refs[:14]
    o_ref, s_ref = refs[-2:]

    @pl.when(pl.program_id(1) == 0)
    def _():
        s_ref[...] = s0_ref[...]

    cq, sq, ck, sk = cq_ref[...], sq_ref[...], ck_ref[...], sk_ref[...]

    def group(bis):
        chains = [(i, h) for i in range(len(bis)) for h in range(HEADS)]
        rows = [_rows(bi * L, L) for bi in bis]
        cols = [slice(h * DH, (h + 1) * DH) for h in range(HEADS)]
        q, k, v, s = {}, {}, {}, {}
        for c in chains:
            i, h = c
            qf = zq_ref[rows[i], cols[h]]
            kf = zk_ref[rows[i], cols[h]]
            q[c] = (qf * cq + pltpu.roll(qf, DH // 2, 1) * sq).astype(BF16)
            k[c] = kf * ck + pltpu.roll(kf, DH // 2, 1) * sk
            v[c] = zv_ref[rows[i], cols[h]].astype(BF16)
            s[c] = s_ref[bis[i], h]
        att = {c: _dot_nt(q[c], k[c].astype(BF16)) for c in chains}
        inter = {c: _dot(q[c], s[c].astype(BF16)) for c in chains}
        upd = {c: _dot_tn((k[c] * kdec_ref[c[1]]).astype(BF16), v[c]) for c in chains}
        o = {c: _dot((att[c] * intra_ref[c[1]]).astype(BF16), v[c]) + inter[c] * cross_ref[c[1]] for c in chains}
        for c in chains:
            i, h = c
            s_ref[bis[i], h] = cdec_ref[h] * s[c] + upd[c]
            o_ref[rows[i], cols[h]] = _head_norm(o[c]) * gn_ref[:, cols[h]] * _silu(zg_ref[rows[i], cols[h]])

    _for_each_group(bb, RET_GROUP, group)


def _layer_state_io(s_in, lin, l, depth, prev, bb, n_inputs, out_index):
    tail = s_in.shape[2:]
    zeros = (0,) * len(tail)
    in_spec = pl.BlockSpec((None, bb) + tail, lambda i, c: (lin, i) + zeros)
    out_spec = pl.BlockSpec((None, bb) + tail, lambda i, c: (l, i) + zeros)
    out_shape = jax.ShapeDtypeStruct((depth,) + s_in.shape[1:], s_in.dtype)
    extra_specs, extra_args, aliases = [], [], {}
    if prev is not None:
        extra_specs = [pl.BlockSpec(memory_space=pl.ANY)]
        extra_args = [prev]
        aliases = {n_inputs: out_index}
    return in_spec, out_spec, out_shape, extra_specs, extra_args, aliases


def _retention(z, B, T, s_in, lin, l, depth, s_prev, tabs, gn_w):
    L = CHUNK if T % CHUNK == 0 else T
    nC = T // L
    bb = 1 if nC > 1 else min(B, 8)
    R = bb * L
    W = HEADS * DH
    cq, sq, ck, sk, intra, cross, kdec, cdec = tabs
    zspec = lambda k: pl.BlockSpec((R, W), lambda i, c: (i * nC + c, k))
    tspec = pl.BlockSpec((L, DH), lambda i, c: (c, 0))
    full = lambda a: pl.BlockSpec(a.shape, lambda i, c: (0,) * a.ndim)
    s_ispec, s_ospec, s_shape, xspecs, xargs, aliases = _layer_state_io(s_in, lin, l, depth, s_prev, bb, 14, 1)
    return pl.pallas_call(
        functools.partial(_ret_kernel, bb=bb, L=L),
        grid=(B // bb, nC),
        in_specs=[zspec(0), zspec(1), zspec(2), zspec(3), tspec, tspec, tspec, tspec,
                  full(intra), full(cross), full(kdec), full(cdec), full(gn_w), s_ispec] + xspecs,
        out_specs=[pl.BlockSpec((R, W), lambda i, c: (i * nC + c, 0)), s_ospec],
        out_shape=[jax.ShapeDtypeStruct((B * T, W), F32), s_shape],
        input_output_aliases=aliases,
        compiler_params=_cparams(("parallel", "arbitrary")),
        name="ret",
    )(z, z, z, z, cq, sq, ck, sk, intra, cross, kdec, cdec, gn_w, s_in, *xargs)


def _ret_tables(T, pos0):
    L = CHUNK if T % CHUNK == 0 else T
    half = DH // 2
    inv = jnp.exp(-math.log(ROPE_BASE) * jnp.arange(half, dtype=F32) / half)
    pos = jnp.arange(T, dtype=F32) + pos0
    ang = pos[:, None] * inv[None, :]
    cos, sin = jnp.cos(ang), jnp.sin(ang)
    c = jnp.concatenate([cos, cos], axis=-1)
    s = jnp.concatenate([-sin, sin], axis=-1)
    kscale = DH ** -0.5
    lg = jnp.log1p(-jnp.exp2(-5.0 - jnp.arange(HEADS, dtype=F32)))
    j = jnp.arange(L, dtype=F32)
    diff = j[:, None] - j[None, :]
    intra = jnp.where(diff >= 0, jnp.exp(lg[:, None, None] * jnp.maximum(diff, 0.0)), 0.0)
    cross = jnp.exp(lg[:, None] * (j + 1.0))
    kdec = jnp.exp(lg[:, None] * (L - 1.0 - j))
    cdec = jnp.exp(lg * L)
    bl = lambda a: jnp.broadcast_to(a[..., None], a.shape + (DH,))
    return (c, s, c * kscale, s * kscale, intra, bl(cross), bl(kdec), bl(cdec[:, None]))


def _mlstm_kernel(*refs, bb, L, last):
    (zu_ref, zv_ref, zo_ref, zg_ref, cw_ref, cb_ref, wqk_ref, gb_ref, gn_ref, sk_ref,
     c0_ref, n0_ref, m0_ref, conv0_ref) = refs[:14]
    o_ref, c_ref, n_ref, m_ref, conv_ref, xp_scr, tail_scr, ca_scr = refs[-8:]
    ci = pl.program_id(1)

    @pl.when(ci == 0)
    def _():
        c_ref[...] = c0_ref[...]
        n_ref[...] = n0_ref[...]
        m_ref[...] = m0_ref[...]
        tail_scr[...] = jnp.zeros(tail_scr.shape, F32)
        tail_scr[:, SUBLANES - (CONV_W - 1):, :] = conv0_ref[...]

    ri = lax.broadcasted_iota(jnp.int32, (L, L), 0)
    rj = lax.broadcasted_iota(jnp.int32, (L, L), 1)
    causal = rj <= ri
    tri = causal.astype(F32)
    lane = lax.broadcasted_iota(jnp.int32, (L, LANES), 1)
    kscale = DH ** -0.5

    def group(bis):
        ns = len(bis)
        rows = [_rows(bi * L, L) for bi in bis]
        cols = [slice(h * DH, (h + 1) * DH) for h in range(HEADS)]
        chains = [(i, h) for i in range(ns) for h in range(HEADS)]
        gates, csum, gates_t, csum_t, m_all, n_all = [], [], [], [], [], []
        for i, bi in enumerate(bis):
            xp_scr[bi, 0:SUBLANES, :] = tail_scr[bi]
            xp_scr[bi, SUBLANES:, :] = zu_ref[rows[i], :]
            cu = cb_ref[...]
            for t in range(CONV_W):
                cu = cu + xp_scr[bi, pl.ds(SUBLANES - (CONV_W - 1) + t, L), :] * cw_ref[pl.ds(t, 1), :]
            tail_scr[bi] = xp_scr[bi, L:L + SUBLANES, :]
            ca_scr[bi] = _silu(cu)
            gz = zg_ref[rows[i], :] + gb_ref[...]
            g = jnp.where(lane < HEADS, gz, jax.nn.log_sigmoid(gz))
            cs = jnp.dot(tri, g, preferred_element_type=F32, precision=lax.Precision.HIGHEST)
            gates.append(g)
            csum.append(cs)
            gates_t.append(g.T)
            csum_t.append(cs.T)
            m_all.append(m_ref[bi])
            n_all.append(n_ref[bi])
        ca = {c: ca_scr[bis[c[0]], :, cols[c[1]]] for c in chains}
        qk = {c: _dot(ca[c].astype(BF16), wqk_ref[c[1]]) for c in chains}
        q = {c: qk[c][:, :DH] for c in chains}
        k = {c: qk[c][:, DH:] * kscale for c in chains}
        qb = {c: q[c].astype(BF16) for c in chains}
        v = {c: zv_ref[rows[c[0]], cols[c[1]]].astype(BF16) for c in chains}
        c_prev = {c: c_ref[bis[c[0]], c[1]] for c in chains}
        sraw = {c: _dot_nt(qb[c], k[c].astype(BF16)) for c in chains}
        qc = {c: _dot(qb[c], c_prev[c].astype(BF16)) for c in chains}
        s, mt, w_int, m_prev, n_prev, b_col, i_col = {}, {}, {}, {}, {}, {}, {}
        for c in chains:
            i, h = c
            i_col[c] = gates[i][:, h:h + 1]
            b_col[c] = csum[i][:, HEADS + h:HEADS + h + 1]
            i_row = gates_t[i][h:h + 1, :]
            b_row = csum_t[i][HEADS + h:HEADS + h + 1, :]
            m_prev[c] = m_all[i][h:h + 1, :1]
            n_prev[c] = n_all[i][h:h + 1, :]
            dlog = jnp.where(causal, b_col[c] - b_row + i_row, -jnp.inf)
            inter = b_col[c] + m_prev[c]
            mt[c] = jnp.maximum(inter, jnp.max(dlog, axis=-1, keepdims=True))
            s[c] = sraw[c] * jnp.exp(dlog - mt[c])
            w_int[c] = jnp.exp(inter - mt[c])
        sv = {c: _dot(s[c].astype(BF16), v[c]) for c in chains}
        kt, dec, m_new = {}, {}, {}
        for c in chains:
            m_new[c] = mt[c][L - 1:L, :]
            b_last = b_col[c][L - 1:L, :]
            tail = jnp.exp(b_last - b_col[c] + i_col[c] - m_new[c])
            dec[c] = jnp.exp(b_last + m_prev[c] - m_new[c])
            kt[c] = k[c] * tail
        upd = {c: _dot_tn(kt[c].astype(BF16), v[c]) for c in chains}
        n_rows = [[] for _ in range(ns)]
        m_rows = [[] for _ in range(ns)]
        for c in chains:
            i, h = c
            num = sv[c] + w_int[c] * qc[c]
            den = (jnp.sum(s[c], axis=-1, keepdims=True)
                   + w_int[c] * jnp.sum(q[c] * n_prev[c], axis=-1, keepdims=True))
            hh = num / jnp.maximum(jnp.abs(den), jnp.exp(-mt[c]))
            c_ref[bis[i], h] = dec[c] * c_prev[c] + upd[c]
            n_rows[i].append(dec[c] * n_prev[c] + jnp.sum(kt[c], axis=0, keepdims=True))
            m_rows[i].append(jnp.broadcast_to(m_new[c], (1, LANES)))
            hm = jax.nn.sigmoid(zo_ref[rows[i], cols[h]]) * hh
            o_ref[rows[i], cols[h]] = _head_norm(hm) * gn_ref[:, cols[h]] + sk_ref[:, cols[h]] * ca[c]
        for i, bi in enumerate(bis):
            n_ref[bi] = jnp.concatenate(n_rows[i], axis=0)
            m_ref[bi] = jnp.concatenate(m_rows[i], axis=0)

    _for_each_group(bb, MLSTM_GROUP, group)

    @pl.when(ci == last)
    def _():
        conv_ref[...] = tail_scr[:, SUBLANES - (CONV_W - 1):, :]


def _mlstm(z, zg, B, T, c_in, lin, l, depth, c_prev, states, cw, cb, wqk, gb, gn_w, skip):
    L = CHUNK if T % CHUNK == 0 else T
    nC = T // L
    bb = 1 if nC > 1 else min(B, 8)
    R = bb * L
    W = HEADS * DH
    n0, m0, conv0 = states
    zspec = lambda k: pl.BlockSpec((R, W), lambda i, c: (i * nC + c, k))
    full = lambda a: pl.BlockSpec(a.shape, lambda i, c: (0,) * a.ndim)
    nspec = pl.BlockSpec((bb, HEADS, DH), lambda i, c: (i, 0, 0))
    vspec = pl.BlockSpec((bb, CONV_W - 1, W), lambda i, c: (i, 0, 0))
    c_ispec, c_ospec, c_shape, xspecs, xargs, aliases = _layer_state_io(c_in, lin, l, depth, c_prev, bb, 14, 1)
    return pl.pallas_call(
        functools.partial(_mlstm_kernel, bb=bb, L=L, last=nC - 1),
        grid=(B // bb, nC),
        in_specs=[zspec(4), zspec(5), zspec(6), pl.BlockSpec((R, LANES), lambda i, c: (i * nC + c, 0)),
                  full(cw), full(cb), full(wqk), full(gb), full(gn_w), full(skip),
                  c_ispec, nspec, nspec, vspec] + xspecs,
        out_specs=[pl.BlockSpec((R, W), lambda i, c: (i * nC + c, 0)), c_ospec, nspec, nspec, vspec],
        input_output_aliases=aliases,
        out_shape=[jax.ShapeDtypeStruct((B * T, W), F32),
                   c_shape,
                   jax.ShapeDtypeStruct((B, HEADS, DH), F32),
                   jax.ShapeDtypeStruct((B, HEADS, LANES), F32),
                   jax.ShapeDtypeStruct((B, CONV_W - 1, W), F32)],
        scratch_shapes=[pltpu.VMEM((bb, L + SUBLANES, W), F32),
                        pltpu.VMEM((bb, SUBLANES, W), F32),
                        pltpu.VMEM((bb, L, W), F32)],
        compiler_params=_cparams(("parallel", "arbitrary")),
        name="mlstm",
    )(z, z, z, zg, cw, cb, wqk, gb, gn_w, skip, c_in, n0, m0, conv0, *xargs)


def _s5_kernel(zu_ref, bw_ref, ar_ref, ai_ref, pr_ref, pi_ref, cwr_ref, cwi_ref, d_ref, wglu_ref, bglu_ref,
               h0r_ref, h0i_ref, o_ref, hr_ref, hi_ref, xr_scr, xi_scr, *, bb, Lc):
    @pl.when(pl.program_id(1) == 0)
    def _():
        hr_ref[...] = h0r_ref[...]
        hi_ref[...] = h0i_ref[...]

    NB = xr_scr.shape[0]
    KB = bw_ref.shape[0]
    BPK = NB // KB
    u = zu_ref[...]
    ub = u.astype(BF16)
    for kb in range(KB):
        r = _dot(ub[:, kb * LANES:(kb + 1) * LANES], bw_ref[kb])
        for t in range(BPK):
            xr_scr[kb * BPK + t] = r[:, t * LANES:(t + 1) * LANES]
            xi_scr[kb * BPK + t] = r[:, (BPK + t) * LANES:(BPK + t + 1) * LANES]

    def cmul_add(a_r, a_i, x_r, x_i, b_r, b_i):
        return a_r * x_r - a_i * x_i + b_r, a_r * x_i + a_i * x_r + b_i

    def scan8(x_r, x_i, cb):
        cl = slice(cb * LANES, (cb + 1) * LANES)
        for lv in range(3):
            x_r, x_i = cmul_add(ar_ref[lv, :, cl], ai_ref[lv, :, cl], pltpu.roll(x_r, 1 << lv, 0),
                                pltpu.roll(x_i, 1 << lv, 0), x_r, x_i)
        return x_r, x_i

    def bcast(row):
        return jnp.broadcast_to(row, (SUBLANES, LANES))

    def seq(bi):
        def grp(gi, carry):
            rows = _rows(bi * Lc + gi * SUBLANES, SUBLANES)
            for cb in range(NB):
                cl = slice(cb * LANES, (cb + 1) * LANES)
                x_r, x_i = scan8(xr_scr[cb, rows, :], xi_scr[cb, rows, :], cb)
                h_r, h_i = cmul_add(pr_ref[:, cl], pi_ref[:, cl], bcast(hr_ref[bi, :, cl]),
                                    bcast(hi_ref[bi, :, cl]), x_r, x_i)
                xr_scr[cb, rows, :] = h_r
                xi_scr[cb, rows, :] = h_i
                hr_ref[bi, :, cl] = h_r[SUBLANES - 1:, :]
                hi_ref[bi, :, cl] = h_i[SUBLANES - 1:, :]
            return carry
        lax.fori_loop(0, Lc // SUBLANES, grp, 0)

    _for_each_seq(bb, seq)

    ys = []
    for kb in range(KB):
        hrb = jnp.concatenate([xr_scr[kb * BPK + t] for t in range(BPK)], axis=1).astype(BF16)
        hib = jnp.concatenate([xi_scr[kb * BPK + t] for t in range(BPK)], axis=1).astype(BF16)
        ys.append(_dot(hrb, cwr_ref[kb]) + _dot(hib, cwi_ref[kb]))
    y = jnp.concatenate(ys, axis=1) + d_ref[...] * u
    g = jax.nn.gelu(y)
    o_ref[...] = g * jax.nn.sigmoid(_dot(g.astype(BF16), wglu_ref[...]) + bglu_ref[...])


def _s5(z, B, T, h0r, h0i, tabs, d, wglu, bglu):
    Lc = min(T, 256)
    nC = T // Lc
    bb = 1 if nC > 1 else min(B, max(1, 128 // T))
    bw, ar, ai, pr, pi, cwr, cwi = tabs
    R = bb * Lc
    W = d.shape[1]
    NS = h0r.shape[-1]
    full = lambda a: pl.BlockSpec(a.shape, lambda i, c: (0,) * a.ndim)
    hspec = pl.BlockSpec((bb, 1, NS), lambda i, c: (i, 0, 0))
    return pl.pallas_call(
        functools.partial(_s5_kernel, bb=bb, Lc=Lc),
        grid=(B // bb, nC),
        in_specs=[pl.BlockSpec((R, W), lambda i, c: (i * nC + c, 7)),
                  full(bw), full(ar), full(ai), full(pr), full(pi), full(cwr), full(cwi),
                  full(d), full(wglu), full(bglu), hspec, hspec],
        out_specs=[pl.BlockSpec((R, W), lambda i, c: (i * nC + c, 0)), hspec, hspec],
        out_shape=[jax.ShapeDtypeStruct((B * T, W), F32),
                   jax.ShapeDtypeStruct((B, 1, NS), F32),
                   jax.ShapeDtypeStruct((B, 1, NS), F32)],
        scratch_shapes=[pltpu.VMEM((NS // LANES, R, LANES), F32), pltpu.VMEM((NS // LANES, R, LANES), F32)],
        compiler_params=_cparams(("parallel", "arbitrary")),
        name="s5",
    )(z, bw, ar, ai, pr, pi, cwr, cwi, d, wglu, bglu, h0r, h0i)


def _s5_tables(a_re, a_im, log_dt, b_re, b_im, c_re, c_im):
    G, P = a_re.shape
    dt = jnp.exp(log_dt.astype(F32))[:, None]
    lam_re = -jnp.abs(a_re.astype(F32))
    lam_im = a_im.astype(F32)

    def power(n):
        n = jnp.asarray(n, F32)[:, None, None]
        mag = jnp.exp(lam_re * dt * n)
        return ((mag * jnp.cos(lam_im * dt * n)).reshape(-1, G * P),
                (mag * jnp.sin(lam_im * dt * n)).reshape(-1, G * P))

    mag = jnp.exp(lam_re * dt)
    ab_re = mag * jnp.cos(lam_im * dt)
    ab_im = mag * jnp.sin(lam_im * dt)
    den = lam_re * lam_re + lam_im * lam_im
    co_re = ((ab_re - 1.0) * lam_re + ab_im * lam_im) / den
    co_im = (ab_im * lam_re - (ab_re - 1.0) * lam_im) / den
    br, bi = b_re.astype(F32), b_im.astype(F32)
    bb_re = co_re[..., None] * br - co_im[..., None] * bi
    bb_im = co_re[..., None] * bi + co_im[..., None] * br
    gpb = LANES // S5_GROUP
    KB = G // gpb
    eye = jnp.eye(gpb, dtype=F32)
    blk = lambda w: jnp.einsum('kgpc,gh->kgchp', w.reshape(KB, gpb, P, S5_GROUP), eye).reshape(KB, LANES, gpb * P)
    bw = jnp.concatenate([blk(bb_re), blk(bb_im)], axis=-1).astype(BF16)
    cblk = lambda w: jnp.einsum('kgcp,gh->kgphc', w.reshape(KB, gpb, S5_GROUP, P), eye).reshape(KB, gpb * P, LANES)
    cwr = cblk(c_re.astype(F32)).astype(BF16)
    cwi = cblk(-c_im.astype(F32)).astype(BF16)
    rows = jnp.arange(SUBLANES)
    ars, ais = [], []
    for lv in range(3):
        r, i = power([1 << lv])
        keep = (rows >= (1 << lv))[:, None]
        ars.append(jnp.where(keep, r, 0.0))
        ais.append(jnp.where(keep, i, 0.0))
    pr, pi = power(np.arange(1, SUBLANES + 1))
    return bw, jnp.stack(ars), jnp.stack(ais), pr, pi, cwr, cwi


def _branch_kernel(or_ref, om_ref, os_ref, gr_ref, gm_ref, gs_ref, wr_ref, wm_ref, ws_ref, o_ref):
    br = _dot(or_ref[...].astype(BF16), wr_ref[...])
    bm = _dot(om_ref[...].astype(BF16), wm_ref[...])
    bs = _dot(os_ref[...].astype(BF16), ws_ref[...])
    merged = (jax.nn.sigmoid(gr_ref[...]) * br + jax.nn.sigmoid(gm_ref[...]) * bm
              + jax.nn.sigmoid(gs_ref[...]) * bs)
    o_ref[...] = merged.astype(BF16)


def _branch(o_r, o_m, o_s, z, wr, wm, ws, l):
    M, W = o_r.shape
    D = wr.shape[2]
    TM, TN = min(M, 512), 1024
    nJ = D // TN
    ospec = pl.BlockSpec((TM, W), lambda i, j: (i, 0))
    gspec = lambda k: pl.BlockSpec((TM, TN), lambda i, j: (i, 8 + k * nJ + j))
    wspec = pl.BlockSpec((None, W, TN), lambda i, j: (l, 0, j))
    return pl.pallas_call(
        _branch_kernel,
        grid=(M // TM, nJ),
        in_specs=[ospec, ospec, ospec, gspec(0), gspec(1), gspec(2), wspec, wspec, wspec],
        out_specs=pl.BlockSpec((TM, TN), lambda i, j: (i, j)),
        out_shape=jax.ShapeDtypeStruct((M, D), BF16),
        compiler_params=_cparams(("parallel", "parallel")),
        name="branch",
    )(o_r, o_m, o_s, z, z, z, wr, wm, ws)


def _wo_kernel(m_ref, w_ref, x_ref, g_ref, o_ref):
    y = _dot(m_ref[...], w_ref[...])
    o_ref[...] = x_ref[...] + g_ref[...] * y.reshape(o_ref.shape)


def _wo(merged, w_o, x, mod, l, moff):
    B, T, D = x.shape
    bb, tt = _tile_cfg(B, T, 512)
    TM, nT = bb * tt, T // tt
    TN = 1024
    xspec = pl.BlockSpec((bb, tt, TN), lambda i, j: (i // nT, i % nT, j))
    return pl.pallas_call(
        _wo_kernel,
        grid=((B // bb) * nT, D // TN),
        in_specs=[pl.BlockSpec((TM, D), lambda i, j: (i, 0)),
                  pl.BlockSpec((None, D, TN), lambda i, j: (l, 0, j)),
                  xspec,
                  pl.BlockSpec((None, bb, 1, TN), lambda i, j: (l, moff // bb + i // nT, 0, 2 * (D // TN) + j))],
        out_specs=xspec,
        out_shape=jax.ShapeDtypeStruct((B, T, D), F32),
        compiler_params=_cparams(("parallel", "parallel")),
        name="wo",
    )(merged, w_o, x, mod)


def _prep_u_kernel(u_ref, o_ref):
    kq = o_ref.shape[-1]
    for j in range(o_ref.shape[0]):
        o_ref[j] = u_ref[:, j * kq:(j + 1) * kq].astype(BF16)


def _prep_u(peer_u):
    L, NE, D = peer_u.shape
    R = 2 * PEER_MC
    kq = D // PEER_KQ
    return pl.pallas_call(
        _prep_u_kernel,
        grid=(L, NE // R),
        in_specs=[pl.BlockSpec((None, R, D), lambda l, p: (l, p, 0))],
        out_specs=pl.BlockSpec((None, None, PEER_KQ, R, kq), lambda l, p: (l, p, 0, 0, 0)),
        out_shape=jax.ShapeDtypeStruct((L, NE // R, PEER_KQ, R, kq), BF16),
        compiler_params=_cparams(("parallel", "parallel")),
        name="prep_u",
    )(peer_u)


def _prep_v_kernel(v_ref, o_ref):
    kq = o_ref.shape[1]
    for j in range(o_ref.shape[0]):
        o_ref[j] = v_ref[:, j * kq:(j + 1) * kq].T.astype(BF16)


def _prep_v(peer_v):
    L, NE, D = peer_v.shape
    kq = D // PEER_KQ
    return pl.pallas_call(
        _prep_v_kernel,
        grid=(L, NE // PEER_MC),
        in_specs=[pl.BlockSpec((None, PEER_MC, D), lambda l, c: (l, c, 0))],
        out_specs=pl.BlockSpec((None, None, PEER_KQ, kq, PEER_MC), lambda l, c: (l, c, 0, 0, 0)),
        out_shape=jax.ShapeDtypeStruct((L, NE // PEER_MC, PEER_KQ, kq, PEER_MC), BF16),
        compiler_params=_cparams(("parallel", "parallel")),
        name="prep_v",
    )(peer_v)


def _peer_q_kernel(x_ref, nw_ref, sh_ref, sc_ref, wq_ref, k1_ref, k2_ref, h_ref, st_ref):
    h = _norm_mod(x_ref[...], nw_ref[...], sc_ref[...], sh_ref[...])
    hb = h.reshape(-1, h.shape[-1]).astype(BF16)
    kq = h_ref.shape[-1]
    for j in range(h_ref.shape[0]):
        h_ref[j] = hb[:, j * kq:(j + 1) * kq]
    q = _dot(hb, wq_ref[...])
    dq = k1_ref.shape[-1]
    for hd in range(PEER_HEADS):
        q1 = q[:, (2 * hd) * dq:(2 * hd + 1) * dq]
        q2 = q[:, (2 * hd + 1) * dq:(2 * hd + 2) * dq]
        st_ref[2 * hd] = _dot_nt(k1_ref[hd], q1, precision=lax.Precision.HIGHEST)
        st_ref[2 * hd + 1] = _dot_nt(k2_ref[hd], q2, precision=lax.Precision.HIGHEST)


def _peer_q(x, mod, l, moff, nw, wq, k1, k2):
    B, T, D = x.shape
    bb, tt = _tile_cfg(B, T, 256)
    TM, nT = bb * tt, T // tt
    full = lambda a: pl.BlockSpec(a.shape, lambda i: (0,) * a.ndim)
    mmap = lambda k: (lambda i: (l, moff // bb + i // nT, 0, k))
    return pl.pallas_call(
        _peer_q_kernel,
        grid=((B // bb) * nT,),
        in_specs=[pl.BlockSpec((bb, tt, D), lambda i: (i // nT, i % nT, 0)),
                  pl.BlockSpec((1, D), lambda i: (0, 0)),
                  pl.BlockSpec((None, bb, 1, D), mmap(3)),
                  pl.BlockSpec((None, bb, 1, D), mmap(4)),
                  pl.BlockSpec((None,) + wq.shape[1:], lambda i: (l, 0, 0)), full(k1), full(k2)],
        out_specs=[pl.BlockSpec((PEER_KQ, TM, D // PEER_KQ), lambda i: (0, i, 0)),
                   pl.BlockSpec((2 * PEER_HEADS, N_KEYS, TM), lambda i: (0, 0, i))],
        out_shape=[jax.ShapeDtypeStruct((PEER_KQ, B * T, D // PEER_KQ), BF16),
                   jax.ShapeDtypeStruct((2 * PEER_HEADS, N_KEYS, B * T), F32)],
        compiler_params=_cparams(("parallel",)),
        name="peer_q",
    )(x, nw, mod, mod, wq, k1, k2)


def _top_values(cur, n):
    vals = []
    for _ in range(n):
        m = jnp.max(cur, axis=0, keepdims=True)
        vals.append(m)
        cur = jnp.where(cur == m, -jnp.inf, cur)
    return vals


def _peer_sel_kernel(st_ref, e1_ref, e2_ref, th_ref):
    ts = st_ref.shape[-1]
    neg = jnp.full((1, ts), -jnp.inf, F32)
    n = PEER_TOPK + 1
    ths = []
    for hd in range(PEER_HEADS):
        s1 = st_ref[2 * hd]
        s2 = st_ref[2 * hd + 1]
        v1 = _top_values(s1, n)
        v2 = _top_values(s2, n)
        rows = [v1[r] + v2[c] for r in range(n) for c in range(n // (r + 1))]
        rows += [neg] * (-len(rows) % SUBLANES)
        top = _top_values(jnp.concatenate(rows, axis=0), n)
        den = jnp.exp(top[0] - top[0])
        for t in top[1:PEER_TOPK]:
            den = den + jnp.exp(t - top[0])
        e1_ref[hd] = jnp.exp(s1 - v1[0]) / den
        e2_ref[hd] = jnp.exp(s2 - v2[0])
        ths.append(jnp.exp(0.5 * (top[PEER_TOPK - 1] + top[PEER_TOPK]) - top[0]) / den)
    th_ref[...] = jnp.concatenate(ths, axis=0)


def _peer_sel(st):
    H2, K, M = st.shape
    TS = min(M, 256)
    hspec = pl.BlockSpec((PEER_HEADS, K, TS), lambda i: (0, 0, i))
    return pl.pallas_call(
        _peer_sel_kernel,
        grid=(M // TS,),
        in_specs=[pl.BlockSpec((H2, K, TS), lambda i: (0, 0, i))],
        out_specs=[hspec, hspec, pl.BlockSpec((PEER_HEADS, TS), lambda i: (0, i))],
        out_shape=[jax.ShapeDtypeStruct((PEER_HEADS, K, M), F32),
                   jax.ShapeDtypeStruct((PEER_HEADS, K, M), F32),
                   jax.ShapeDtypeStruct((PEER_HEADS, M), F32)],
        compiler_params=_cparams(("parallel",)),
        name="peer_sel",
    )(st)


def _peer_exp_kernel(h_ref, u_ref, vta_ref, vtb_ref, e1_ref, e2_ref, th_ref, x_ref, g_ref, o_ref,
                     acc_scr, sc0, sc1, zt0, zt1, *, mc, nc):
    s = pl.program_id(1)
    slabs = mc // N_KEYS

    @pl.when(s == 0)
    def _():
        acc_scr[...] = jnp.zeros(acc_scr.shape, F32)
        sc1[...] = jnp.zeros(sc1.shape, F32)
        zt0[...] = jnp.zeros(zt0.shape, BF16)
        zt1[...] = jnp.zeros(zt1.shape, BF16)

    nq = u_ref.shape[0]
    spp = slabs // nq
    assert spp * nq == slabs

    def substep(half, sc_a, sc_b, zt_b, zt_c, chunk_b):
        c = jnp.clip(chunk_b, 0, nc - 1)
        cols = slice(half * mc, (half + 1) * mc)
        sc_a[...] = jnp.zeros(sc_a.shape, F32)

        def piece(j, carry):
            acc_scr[j] += _dot((vta_ref, vtb_ref)[half][j], zt_c[...])
            for sl in range(spp):
                rs = pl.ds(pl.multiple_of((j * spp + sl) * N_KEYS, N_KEYS), N_KEYS)
                i1 = c * slabs + j * spp + sl
                for tc in range(sc_b.shape[1] // LANES):
                    tl = slice(tc * LANES, (tc + 1) * LANES)
                    w = jnp.zeros((N_KEYS, LANES), F32)
                    for hd in range(PEER_HEADS):
                        p = e1_ref[hd, pl.ds(i1, 1), :][:, tl] * e2_ref[hd, :, tl]
                        w = jnp.where(p >= th_ref[pl.ds(hd, 1), tl], w + p, w)
                    zt_b[rs, tl] = (jax.nn.gelu(sc_b[rs, tl]) * w).astype(BF16)
            sc_a[...] += _dot_nt(u_ref[j, cols, :], h_ref[j])
            return carry

        lax.fori_loop(0, nq, piece, 0)

    substep(0, sc0, sc1, zt1, zt0, 2 * s - 1)
    substep(1, sc1, sc0, zt0, zt1, 2 * s)

    @pl.when(s == pl.num_programs(1) - 1)
    def _():
        o_ref[...] = x_ref[...] + g_ref[...] * acc_scr[...].reshape(-1, acc_scr.shape[-1]).T.reshape(o_ref.shape)


def _peer_exp(hb, u, vt, e1, e2, th, x, mod, l, moff):
    B, T, D = x.shape
    bb, tt = _tile_cfg(B, T, PEER_TB)
    TB, nT = bb * tt, T // tt
    MC = PEER_MC
    NP = u.shape[1]
    NC = vt.shape[1]
    assert NC == 2 * NP and u.shape[3] == 2 * MC and vt.shape[4] == MC
    xspec = pl.BlockSpec((bb, tt, D), lambda i, s: (i // nT, i % nT, 0))
    hspec = pl.BlockSpec((PEER_HEADS, N_KEYS, TB), lambda i, s: (0, 0, i))
    return pl.pallas_call(
        functools.partial(_peer_exp_kernel, mc=MC, nc=NC),
        grid=((B // bb) * nT, NP + 1),
        in_specs=[pl.BlockSpec((PEER_KQ, TB, D // PEER_KQ), lambda i, s: (0, i, 0)),
                  pl.BlockSpec((None, None, PEER_KQ, 2 * MC, D // PEER_KQ),
                               lambda i, s: (l, jnp.minimum(s, NP - 1), 0, 0, 0)),
                  pl.BlockSpec((None, None, PEER_KQ, D // PEER_KQ, MC),
                               lambda i, s: (l, 2 * jnp.maximum(s - 1, 0), 0, 0, 0)),
                  pl.BlockSpec((None, None, PEER_KQ, D // PEER_KQ, MC),
                               lambda i, s: (l, 2 * jnp.maximum(s - 1, 0) + 1, 0, 0, 0)),
                  hspec, hspec,
                  pl.BlockSpec((PEER_HEADS, TB), lambda i, s: (0, i)),
                  xspec,
                  pl.BlockSpec((None, bb, 1, D), lambda i, s: (l, moff // bb + i // nT, 0, 5))],
        out_specs=xspec,
        out_shape=jax.ShapeDtypeStruct((B, T, D), F32),
        scratch_shapes=[pltpu.VMEM((PEER_KQ, D // PEER_KQ, TB), F32),
                        pltpu.VMEM((MC, TB), F32), pltpu.VMEM((MC, TB), F32),
                        pltpu.VMEM((MC, TB), BF16), pltpu.VMEM((MC, TB), BF16)],
        compiler_params=_cparams(("parallel", "arbitrary")),
        name="peer_exp",
    )(hb, u, vt, vt, e1, e2, th, x, mod)


def _final_kernel(x_ref, w_ref, o_ref):
    x = x_ref[...]
    ms = jnp.mean(x * x, axis=-1, keepdims=True)
    o_ref[...] = x * lax.rsqrt(ms + EPS) * w_ref[...]


def _final_norm(x, w):
    B, T, D = x.shape
    bb, tt = _tile_cfg(B, T, 512)
    nT = T // tt
    xspec = pl.BlockSpec((bb, tt, D), lambda i: (i // nT, i % nT, 0))
    return pl.pallas_call(
        _final_kernel,
        grid=((B // bb) * nT,),
        in_specs=[xspec, pl.BlockSpec((1, D), lambda i: (0, 0))],
        out_specs=xspec,
        out_shape=jax.ShapeDtypeStruct((B, T, D), F32),
        compiler_params=_cparams(("parallel",)),
        name="final_norm",
    )(x, w)


def _trunk(x, mod, moff, pos0, big_states, states, lw, final_w):
    B, T, D = x.shape
    depth = len(lw)
    rtabs = _ret_tables(T, pos0)
    ret_in, c_in = big_states
    n_ret = n_c = None
    new = []
    for l, (st, w) in enumerate(zip(states, lw)):
        s_n, s_m, s_conv, s_hr, s_hi = st
        z, zg = _in_proj(x, mod, l, moff, w['n1'], w['w_a'], w['w_b'], w['w_gate'])
        o_r, n_ret = _retention(z, B, T, ret_in, min(l, ret_in.shape[0] - 1), l, depth, n_ret, rtabs, w['ret_gn'])
        o_m, n_c, n_n, n_m, n_conv = _mlstm(z, zg, B, T, c_in, min(l, c_in.shape[0] - 1), l, depth, n_c,
                                            (s_n, s_m, s_conv), w['conv_w'], w['conv_b'],
                                            w['wqk'], w['gate_b'], w['m_gn'], w['m_skip'])
        o_s, n_hr, n_hi = _s5(z, B, T, s_hr, s_hi, w['s5_tabs'], w['s5_d'], w['s5_wglu'], w['s5_bglu'])
        merged = _branch(o_r, o_m, o_s, z, w['w_ret_out'], w['w_mlstm_out'], w['w_s5_out'], l)
        x = _wo(merged, w['w_o'], x, mod, l, moff)
        hb, sc = _peer_q(x, mod, l, moff, w['n2'], w['peer_wq'], w['peer_k1'], w['peer_k2'])
        e1, e2, th = _peer_sel(sc)
        x = _peer_exp(hb, w['peer_u'], w['peer_vt'], e1, e2, th, x, mod, l, moff)
        new.append((n_n, n_m[..., 0], n_conv, n_hr, n_hi))
    y = _final_norm(x, final_w)
    return y, [n_ret, n_c] + [jnp.stack([s[i] for s in new]) for i in range(5)]


def kernel(x_prompt, x_sample, state_ret, state_mlstm_c, state_mlstm_n, state_mlstm_m, state_mlstm_conv,
           state_s5_re, state_s5_im, c_prompt, c_sample, ada_w, ada_b, norm1_w, norm2_w, final_norm_w,
           w_in, ret_gn_w, w_ret_out, mlstm_conv_w, mlstm_conv_b, mlstm_wq, mlstm_wk, mlstm_b_i, mlstm_b_f,
           mlstm_gn_w, mlstm_skip, w_mlstm_out, s5_a_re, s5_a_im, s5_log_dt, s5_b_re, s5_b_im, s5_c_re,
           s5_c_im, s5_d, s5_w_glu, s5_b_glu, w_s5_out, w_o, peer_wq, peer_k1, peer_k2, peer_u, peer_v):
    depth = w_in.shape[0]
    Bp, Tp, D = x_prompt.shape
    Bs, Ts, _ = x_sample.shape
    W = HEADS * DH
    G, P = s5_a_re.shape[1:]
    NS = G * P
    past_len = 16384

    pad = -(Bs + Bp) % SUBLANES
    c_all = jnp.concatenate([c_sample, c_prompt, jnp.zeros((pad, D), F32)], axis=0)
    mod = _ada(c_all, ada_w, ada_b)
    mod = mod.reshape(depth, c_all.shape[0], 1, ada_w.shape[2])

    a_end = 7 * W
    g_end = a_end + 2 * HEADS
    w_ret_out_b, w_mlstm_out_b, w_s5_out_b = w_ret_out.astype(BF16), w_mlstm_out.astype(BF16), w_s5_out.astype(BF16)
    w_o_b, peer_wq_b = w_o.astype(BF16), peer_wq.astype(BF16)
    u_tiles = _prep_u(peer_u)
    vt_tiles = _prep_v(peer_v)
    lw = []
    for l in range(depth):
        w_gate = jnp.pad(w_in[l, :, a_end:g_end], ((0, 0), (0, LANES - 2 * HEADS))).astype(BF16)
        gate_b = jnp.pad(jnp.concatenate([mlstm_b_i[l], mlstm_b_f[l]]), (0, LANES - 2 * HEADS)).reshape(1, LANES)
        lw.append(dict(
            n1=norm1_w[l].reshape(1, D), n2=norm2_w[l].reshape(1, D),
            w_a=w_in[l, :, :a_end].astype(BF16), w_b=w_in[l, :, g_end:].astype(BF16), w_gate=w_gate, gate_b=gate_b,
            ret_gn=ret_gn_w[l].reshape(1, W),
            conv_w=mlstm_conv_w[l], conv_b=mlstm_conv_b[l].reshape(1, W),
            wqk=jnp.concatenate([mlstm_wq[l], mlstm_wk[l]], axis=-1).astype(BF16),
            m_gn=mlstm_gn_w[l].reshape(1, W), m_skip=mlstm_skip[l].reshape(1, W),
            s5_tabs=_s5_tables(s5_a_re[l], s5_a_im[l], s5_log_dt[l], s5_b_re[l], s5_b_im[l], s5_c_re[l], s5_c_im[l]),
            s5_d=s5_d[l].reshape(1, -1), s5_wglu=s5_w_glu[l].astype(BF16), s5_bglu=s5_b_glu[l].reshape(1, -1),
            w_ret_out=w_ret_out_b, w_mlstm_out=w_mlstm_out_b, w_s5_out=w_s5_out_b, w_o=w_o_b,
            peer_wq=peer_wq_b, peer_k1=peer_k1[l], peer_k2=peer_k2[l],
            peer_u=u_tiles, peer_vt=vt_tiles))

    def zero_states(B):
        return (jnp.zeros((B, HEADS, DH), F32), jnp.zeros((B, HEADS, LANES), F32),
                jnp.zeros((B, CONV_W - 1, W), F32), jnp.zeros((B, 1, NS), F32), jnp.zeros((B, 1, NS), F32))

    zero_big = jnp.zeros((1, Bp, HEADS, DH, DH), F32)
    prompt_states = [zero_states(Bp) for _ in range(depth)]
    sample_states = [(state_mlstm_n[l],
                      jnp.broadcast_to(state_mlstm_m[l][..., None], (Bs, HEADS, LANES)),
                      state_mlstm_conv[l], state_s5_re[l].reshape(Bs, 1, NS), state_s5_im[l].reshape(Bs, 1, NS))
                     for l in range(depth)]

    y_p, ps = _trunk(x_prompt, mod, Bs, 0, (zero_big, zero_big), prompt_states, lw, final_norm_w.reshape(1, D))
    y_s, ss = _trunk(x_sample, mod, 0, past_len, (state_ret, state_mlstm_c), sample_states, lw,
                     final_norm_w.reshape(1, D))

    def unpack(st, B):
        r, c, n, m, conv, hr, hi = st
        return (r, c, n, m, conv, hr.reshape(depth, B, G, P), hi.reshape(depth, B, G, P))

    return (y_p, y_s) + unpack(ps, Bp) + unpack(ss, Bs)
```

```python
import functools
import math

import jax
import jax.numpy as jnp
import numpy as np
from jax import lax
from jax.experimental import pallas as pl
from jax.experimental.pallas import tpu as pltpu

F32 = jnp.float32
BF16 = jnp.bfloat16

EPS = 1e-6
ROPE_BASE = 10000.0
HEADS = 8
DH = 128
CONV_W = 4
S5_GROUP = 16
S5_STATE = 64
PEER_HEADS = 8
PEER_TOPK = 16
N_KEYS = 128
CHUNK = 128
LANES = 128
SUBLANES = 8
VMEM_LIMIT = 56 * 1024 * 1024
RET_GROUP = 8
MLSTM_GROUP = 2
PEER_TB = 512
PEER_MC = 512
PEER_KQ = 4


def _cparams(sem):
    return pltpu.CompilerParams(dimension_semantics=sem, vmem_limit_bytes=VMEM_LIMIT)


def _tile_cfg(B, T, target):
    if T >= target:
        bb, tt = 1, target
    else:
        bb, tt = min(B, target // T), T
    assert T % tt == 0 and B % bb == 0
    return bb, tt


def _silu(x):
    return x * jax.nn.sigmoid(x)


def _norm_mod(x, w, sc, sh):
    ms = jnp.mean(x * x, axis=-1, keepdims=True)
    y = x * lax.rsqrt(ms + EPS) * w
    return y * (1.0 + sc) + sh


def _head_norm(x):
    mu = jnp.mean(x, axis=-1, keepdims=True)
    xc = x - mu
    var = jnp.mean(xc * xc, axis=-1, keepdims=True)
    return xc * lax.rsqrt(var + EPS)


def _dot(a, b):
    return jnp.dot(a, b, preferred_element_type=F32)


def _dot_nt(a, b, **kw):
    return lax.dot_general(a, b, (((1,), (1,)), ((), ())), preferred_element_type=F32, **kw)


def _dot_tn(a, b):
    return lax.dot_general(a, b, (((0,), (0,)), ((), ())), preferred_element_type=F32)


def _rows(start, n):
    if isinstance(start, int):
        return pl.ds(start, n)
    return pl.ds(pl.multiple_of(start, SUBLANES), n)


def _for_each_group(bb, gs, body):
    gs = min(gs, bb)
    assert bb % gs == 0
    if bb == gs:
        body(list(range(bb)))
    else:
        def f(gi, c):
            body([gi * gs + i for i in range(gs)])
            return c
        lax.fori_loop(0, bb // gs, f, 0)


def _for_each_seq(bb, body, unroll=1):
    if bb == 1:
        body(0)
    else:
        def f(bi, c):
            body(bi)
            return c
        lax.fori_loop(0, bb, f, 0, unroll=unroll)


def _ada_kernel(c_ref, w_ref, b_ref, o_ref):
    a = _silu(c_ref[...]).astype(BF16)
    o_ref[...] = _dot(a, w_ref[...].astype(BF16)) + b_ref[...]


def _ada(c_all, ada_w, ada_b):
    L, D, N = ada_w.shape
    Bc = c_all.shape[0]
    TN = 1024
    return pl.pallas_call(
        _ada_kernel,
        grid=(L, N // TN),
        in_specs=[pl.BlockSpec((Bc, D), lambda l, j: (0, 0)),
                  pl.BlockSpec((None, D, TN), lambda l, j: (l, 0, j)),
                  pl.BlockSpec((None, 1, TN), lambda l, j: (l, 0, j))],
        out_specs=pl.BlockSpec((None, Bc, TN), lambda l, j: (l, 0, j)),
        out_shape=jax.ShapeDtypeStruct((L, Bc, N), F32),
        compiler_params=_cparams(("parallel", "parallel")),
        name="ada",
    )(c_all, ada_w, ada_b.reshape(L, 1, N))


def _in_kernel(x_ref, nw_ref, sh_ref, sc_ref, wa_ref, wb_ref, wg_ref, z_ref, zg_ref, h_scr, *, na):
    j = pl.program_id(1)

    @pl.when(j == 0)
    def _():
        h = _norm_mod(x_ref[...], nw_ref[...], sc_ref[...], sh_ref[...])
        hb = h.reshape(h_scr.shape).astype(BF16)
        h_scr[...] = hb
        zg_ref[...] = _dot(hb, wg_ref[...])

    @pl.when(j < na)
    def _():
        z_ref[...] = _dot(h_scr[...], wa_ref[...])

    @pl.when(j >= na)
    def _():
        z_ref[...] = _dot(h_scr[...], wb_ref[...])


def _in_proj(x, mod, l, moff, nw, wa, wb, wg):
    B, T, D = x.shape
    bb, tt = _tile_cfg(B, T, 1024)
    TM, nT = bb * tt, T // tt
    nI = (B // bb) * nT
    TN = 1024
    na = wa.shape[1] // TN
    N = wa.shape[1] + wb.shape[1]
    xmap = lambda i, j: (i // nT, i % nT, 0)
    mmap = lambda k: (lambda i, j: (l, moff // bb + i // nT, 0, k))
    return pl.pallas_call(
        functools.partial(_in_kernel, na=na),
        grid=(nI, N // TN),
        in_specs=[pl.BlockSpec((bb, tt, D), xmap),
                  pl.BlockSpec((1, D), lambda i, j: (0, 0)),
                  pl.BlockSpec((None, bb, 1, D), mmap(0)),
                  pl.BlockSpec((None, bb, 1, D), mmap(1)),
                  pl.BlockSpec((D, TN), lambda i, j: (0, jnp.minimum(j, na - 1))),
                  pl.BlockSpec((D, TN), lambda i, j: (0, jnp.maximum(j - na, 0))),
                  pl.BlockSpec((D, LANES), lambda i, j: (0, 0))],
        out_specs=[pl.BlockSpec((TM, TN), lambda i, j: (i, j)),
                   pl.BlockSpec((TM, LANES), lambda i, j: (i, 0))],
        out_shape=[jax.ShapeDtypeStruct((B * T, N), F32),
                   jax.ShapeDtypeStruct((B * T, LANES), F32)],
        scratch_shapes=[pltpu.VMEM((TM, D), BF16)],
        compiler_params=_cparams(("parallel", "arbitrary")),
        name="in_proj",
    )(x, nw, mod, mod, wa, wb, wg)


def _ret_kernel(*refs, bb, L):
    (zq_ref, zk_ref, zv_ref, zg_ref, cq_ref, sq_ref, ck_ref, sk_ref, intra_ref, cross_ref,
     kdec_ref, cdec_ref, gn_ref, s0_ref) = refs[:14]
    o_ref, s_ref = refs[-2:]

    @pl.when(pl.program_id(1) == 0)
    def _():
        s_ref[...] = s0_ref[...]

    cq, sq, ck, sk = cq_ref[...], sq_ref[...], ck_ref[...], sk_ref[...]

    def group(bis):
        chains = [(i, h) for i in range(len(bis)) for h in range(HEADS)]
        rows = [_rows(bi * L, L) for bi in bis]
        cols = [slice(h * DH, (h + 1) * DH) for h in range(HEADS)]
        q, k, v, s = {}, {}, {}, {}
        for c in chains:
            i, h = c
            qf = zq_ref[rows[i], cols[h]]
            kf = zk_ref[rows[i], cols[h]]
            q[c] = (qf * cq + pltpu.roll(qf, DH // 2, 1) * sq).astype(BF16)
            k[c] = kf * ck + pltpu.roll(kf, DH // 2, 1) * sk
            v[c] = zv_ref[rows[i], cols[h]].astype(BF16)
            s[c] = s_ref[bis[i], h]
        att = {c: _dot_nt(q[c], k[c].astype(BF16)) for c in chains}
        inter = {c: _dot(q[c], s[c].astype(BF16)) for c in chains}
        upd = {c: _dot_tn((k[c] * kdec_ref[c[1]]).astype(BF16), v[c]) for c in chains}
        o = {c: _dot((att[c] * intra_ref[c[1]]).astype(BF16), v[c]) + inter[c] * cross_ref[c[1]] for c in chains}
        for c in chains:
            i, h = c
            s_ref[bis[i], h] = cdec_ref[h] * s[c] + upd[c]
            o_ref[rows[i], cols[h]] = _head_norm(o[c]) * gn_ref[:, cols[h]] * _silu(zg_ref[rows[i], cols[h]])

    _for_each_group(bb, RET_GROUP, group)


def _layer_state_io(s_in, lin, l, depth, prev, bb, n_inputs, out_index):
    tail = s_in.shape[2:]
    zeros = (0,) * len(tail)
    in_spec = pl.BlockSpec((None, bb) + tail, lambda i, c: (lin, i) + zeros)
    out_spec = pl.BlockSpec((None, bb) + tail, lambda i, c: (l, i) + zeros)
    out_shape = jax.ShapeDtypeStruct((depth,) + s_in.shape[1:], s_in.dtype)
    extra_specs, extra_args, aliases = [], [], {}
    if prev is not None:
        extra_specs = [pl.BlockSpec(memory_space=pl.ANY)]
        extra_args = [prev]
        aliases = {n_inputs: out_index}
    return in_spec, out_spec, out_shape, extra_specs, extra_args, aliases


def _retention(z, B, T, s_in, lin, l, depth, s_prev, tabs, gn_w):
    L = CHUNK if T % CHUNK == 0 else T
    nC = T // L
    bb = 1 if nC > 1 else min(B, 8)
    R = bb * L
    W = HEADS * DH
    cq, sq, ck, sk, intra, cross, kdec, cdec = tabs
    zspec = lambda k: pl.BlockSpec((R, W), lambda i, c: (i * nC + c, k))
    tspec = pl.BlockSpec((L, DH), lambda i, c: (c, 0))
    full = lambda a: pl.BlockSpec(a.shape, lambda i, c: (0,) * a.ndim)
    s_ispec, s_ospec, s_shape, xspecs, xargs, aliases = _layer_state_io(s_in, lin, l, depth, s_prev, bb, 14, 1)
    return pl.pallas_call(
        functools.partial(_ret_kernel, bb=bb, L=L),
        grid=(B // bb, nC),
        in_specs=[zspec(0), zspec(1), zspec(2), zspec(3), tspec, tspec, tspec, tspec,
                  full(intra), full(cross), full(kdec), full(cdec), full(gn_w), s_ispec] + xspecs,
        out_specs=[pl.BlockSpec((R, W), lambda i, c: (i * nC + c, 0)), s_ospec],
        out_shape=[jax.ShapeDtypeStruct((B * T, W), F32), s_shape],
        input_output_aliases=aliases,
        compiler_params=_cparams(("parallel", "arbitrary")),
        name="ret",
    )(z, z, z, z, cq, sq, ck, sk, intra, cross, kdec, cdec, gn_w, s_in, *xargs)


def _ret_tables(T, pos0):
    L = CHUNK if T % CHUNK == 0 else T
    half = DH // 2
    inv = jnp.exp(-math.log(ROPE_BASE) * jnp.arange(half, dtype=F32) / half)
    pos = jnp.arange(T, dtype=F32) + pos0
    ang = pos[:, None] * inv[None, :]
    cos, sin = jnp.cos(ang), jnp.sin(ang)
    c = jnp.concatenate([cos, cos], axis=-1)
    s = jnp.concatenate([-sin, sin], axis=-1)
    kscale = DH ** -0.5
    lg = jnp.log1p(-jnp.exp2(-5.0 - jnp.arange(HEADS, dtype=F32)))
    j = jnp.arange(L, dtype=F32)
    diff = j[:, None] - j[None, :]
    intra = jnp.where(diff >= 0, jnp.exp(lg[:, None, None] * jnp.maximum(diff, 0.0)), 0.0)
    cross = jnp.exp(lg[:, None] * (j + 1.0))
    kdec = jnp.exp(lg[:, None] * (L - 1.0 - j))
    cdec = jnp.exp(lg * L)
    bl = lambda a: jnp.broadcast_to(a[..., None], a.shape + (DH,))
    return (c, s, c * kscale, s * kscale, intra, bl(cross), bl(kdec), bl(cdec[:, None]))


def _mlstm_kernel(*refs, bb, L, last):
    (zu_ref, zv_ref, zo_ref, zg_ref, cw_ref, cb_ref, wqk_ref, gb_ref, gn_ref, sk_ref,
     c0_ref, n0_ref, m0_ref, conv0_ref) = refs[:14]
    o_ref, c_ref, n_ref, m_ref, conv_ref, xp_scr, tail_scr, ca_scr = refs[-8:]
    ci = pl.program_id(1)

    @pl.when(ci == 0)
    def _():
        c_ref[...] = c0_ref[...]
        n_ref[...] = n0_ref[...]
        m_ref[...] = m0_ref[...]
        tail_scr[...] = jnp.zeros(tail_scr.shape, F32)
        tail_scr[:, SUBLANES - (CONV_W - 1):, :] = conv0_ref[...]

    ri = lax.broadcasted_iota(jnp.int32, (L, L), 0)
    rj = lax.broadcasted_iota(jnp.int32, (L, L), 1)
    causal = rj <= ri
    tri = causal.astype(F32)
    lane = lax.broadcasted_iota(jnp.int32, (L, LANES), 1)
    kscale = DH ** -0.5

    def group(bis):
        ns = len(bis)
        rows = [_rows(bi * L, L) for bi in bis]
        cols = [slice(h * DH, (h + 1) * DH) for h in range(HEADS)]
        chains = [(i, h) for i in range(ns) for h in range(HEADS)]
        gates, csum, gates_t, csum_t, m_all, n_all = [], [], [], [], [], []
        for i, bi in enumerate(bis):
            xp_scr[bi, 0:SUBLANES, :] = tail_scr[bi]
            xp_scr[bi, SUBLANES:, :] = zu_ref[rows[i], :]
            cu = cb_ref[...]
            for t in range(CONV_W):
                cu = cu + xp_scr[bi, pl.ds(SUBLANES - (CONV_W - 1) + t, L), :] * cw_ref[pl.ds(t, 1), :]
            tail_scr[bi] = xp_scr[bi, L:L + SUBLANES, :]
            ca_scr[bi] = _silu(cu)
            gz = zg_ref[rows[i], :] + gb_ref[...]
            g = jnp.where(lane < HEADS, gz, jax.nn.log_sigmoid(gz))
            cs = jnp.dot(tri, g, preferred_element_type=F32, precision=lax.Precision.HIGHEST)
            gates.append(g)
            csum.append(cs)
            gates_t.append(g.T)
            csum_t.append(cs.T)
            m_all.append(m_ref[bi])
            n_all.append(n_ref[bi])
        ca = {c: ca_scr[bis[c[0]], :, cols[c[1]]] for c in chains}
        qk = {c: _dot(ca[c].astype(BF16), wqk_ref[c[1]]) for c in chains}
        q = {c: qk[c][:, :DH] for c in chains}
        k = {c: qk[c][:, DH:] * kscale for c in chains}
        qb = {c: q[c].astype(BF16) for c in chains}
        v = {c: zv_ref[rows[c[0]], cols[c[1]]].astype(BF16) for c in chains}
        c_prev = {c: c_ref[bis[c[0]], c[1]] for c in chains}
        sraw = {c: _dot_nt(qb[c], k[c].astype(BF16)) for c in chains}
        qc = {c: _dot(qb[c], c_prev[c].astype(BF16)) for c in chains}
        s, mt, w_int, m_prev, n_prev, b_col, i_col = {}, {}, {}, {}, {}, {}, {}
        for c in chains:
            i, h = c
            i_col[c] = gates[i][:, h:h + 1]
            b_col[c] = csum[i][:, HEADS + h:HEADS + h + 1]
            i_row = gates_t[i][h:h + 1, :]
            b_row = csum_t[i][HEADS + h:HEADS + h + 1, :]
            m_prev[c] = m_all[i][h:h + 1, :1]
            n_prev[c] = n_all[i][h:h + 1, :]
            dlog = jnp.where(causal, b_col[c] - b_row + i_row, -jnp.inf)
            inter = b_col[c] + m_prev[c]
            mt[c] = jnp.maximum(inter, jnp.max(dlog, axis=-1, keepdims=True))
            s[c] = sraw[c] * jnp.exp(dlog - mt[c])
            w_int[c] = jnp.exp(inter - mt[c])
        sv = {c: _dot(s[c].astype(BF16), v[c]) for c in chains}
        kt, dec, m_new = {}, {}, {}
        for c in chains:
            m_new[c] = mt[c][L - 1:L, :]
            b_last = b_col[c][L - 1:L, :]
            tail = jnp.exp(b_last - b_col[c] + i_col[c] - m_new[c])
            dec[c] = jnp.exp(b_last + m_prev[c] - m_new[c])
            kt[c] = k[c] * tail
        upd = {c: _dot_tn(kt[c].astype(BF16), v[c]) for c in chains}
        n_rows = [[] for _ in range(ns)]
        m_rows = [[] for _ in range(ns)]
        for c in chains:
            i, h = c
            num = sv[c] + w_int[c] * qc[c]
            den = (jnp.sum(s[c], axis=-1, keepdims=True)
                   + w_int[c] * jnp.sum(q[c] * n_prev[c], axis=-1, keepdims=True))
            hh = num / jnp.maximum(jnp.abs(den), jnp.exp(-mt[c]))
            c_ref[bis[i], h] = dec[c] * c_prev[c] + upd[c]
            n_rows[i].append(dec[c] * n_prev[c] + jnp.sum(kt[c], axis=0, keepdims=True))
            m_rows[i].append(jnp.broadcast_to(m_new[c], (1, LANES)))
            hm = jax.nn.sigmoid(zo_ref[rows[i], cols[h]]) * hh
            o_ref[rows[i], cols[h]] = _head_norm(hm) * gn_ref[:, cols[h]] + sk_ref[:, cols[h]] * ca[c]
        for i, bi in enumerate(bis):
            n_ref[bi] = jnp.concatenate(n_rows[i], axis=0)
            m_ref[bi] = jnp.concatenate(m_rows[i], axis=0)

    _for_each_group(bb, MLSTM_GROUP, group)

    @pl.when(ci == last)
    def _():
        conv_ref[...] = tail_scr[:, SUBLANES - (CONV_W - 1):, :]


def _mlstm(z, zg, B, T, c_in, lin, l, depth, c_prev, states, cw, cb, wqk, gb, gn_w, skip):
    L = CHUNK if T % CHUNK == 0 else T
    nC = T // L
    bb = 1 if nC > 1 else min(B, 8)
    R = bb * L
    W = HEADS * DH
    n0, m0, conv0 = states
    zspec = lambda k: pl.BlockSpec((R, W), lambda i, c: (i * nC + c, k))
    full = lambda a: pl.BlockSpec(a.shape, lambda i, c: (0,) * a.ndim)
    nspec = pl.BlockSpec((bb, HEADS, DH), lambda i, c: (i, 0, 0))
    vspec = pl.BlockSpec((bb, CONV_W - 1, W), lambda i, c: (i, 0, 0))
    c_ispec, c_ospec, c_shape, xspecs, xargs, aliases = _layer_state_io(c_in, lin, l, depth, c_prev, bb, 14, 1)
    return pl.pallas_call(
        functools.partial(_mlstm_kernel, bb=bb, L=L, last=nC - 1),
        grid=(B // bb, nC),
        in_specs=[zspec(4), zspec(5), zspec(6), pl.BlockSpec((R, LANES), lambda i, c: (i * nC + c, 0)),
                  full(cw), full(cb), full(wqk), full(gb), full(gn_w), full(skip),
                  c_ispec, nspec, nspec, vspec] + xspecs,
        out_specs=[pl.BlockSpec((R, W), lambda i, c: (i * nC + c, 0)), c_ospec, nspec, nspec, vspec],
        input_output_aliases=aliases,
        out_shape=[jax.ShapeDtypeStruct((B * T, W), F32),
                   c_shape,
                   jax.ShapeDtypeStruct((B, HEADS, DH), F32),
                   jax.ShapeDtypeStruct((B, HEADS, LANES), F32),
                   jax.ShapeDtypeStruct((B, CONV_W - 1, W), F32)],
        scratch_shapes=[pltpu.VMEM((bb, L + SUBLANES, W), F32),
                        pltpu.VMEM((bb, SUBLANES, W), F32),
                        pltpu.VMEM((bb, L, W), F32)],
        compiler_params=_cparams(("parallel", "arbitrary")),
        name="mlstm",
    )(z, z, z, zg, cw, cb, wqk, gb, gn_w, skip, c_in, n0, m0, conv0, *xargs)


def _s5_kernel(zu_ref, bw_ref, ar_ref, ai_ref, pr_ref, pi_ref, cwr_ref, cwi_ref, d_ref, wglu_ref, bglu_ref,
               h0r_ref, h0i_ref, o_ref, hr_ref, hi_ref, xr_scr, xi_scr, *, bb, Lc):
    @pl.when(pl.program_id(1) == 0)
    def _():
        hr_ref[...] = h0r_ref[...]
        hi_ref[...] = h0i_ref[...]

    NB = xr_scr.shape[0]
    KB = bw_ref.shape[0]
    BPK = NB // KB
    u = zu_ref[...]
    ub = u.astype(BF16)
    for kb in range(KB):
        r = _dot(ub[:, kb * LANES:(kb + 1) * LANES], bw_ref[kb])
        for t in range(BPK):
            xr_scr[kb * BPK + t] = r[:, t * LANES:(t + 1) * LANES]
            xi_scr[kb * BPK + t] = r[:, (BPK + t) * LANES:(BPK + t + 1) * LANES]

    def cmul_add(a_r, a_i, x_r, x_i, b_r, b_i):
        return a_r * x_r - a_i * x_i + b_r, a_r * x_i + a_i * x_r + b_i

    def scan8(x_r, x_i, cb):
        cl = slice(cb * LANES, (cb + 1) * LANES)
        for lv in range(3):
            x_r, x_i = cmul_add(ar_ref[lv, :, cl], ai_ref[lv, :, cl], pltpu.roll(x_r, 1 << lv, 0),
                                pltpu.roll(x_i, 1 << lv, 0), x_r, x_i)
        return x_r, x_i

    def bcast(row):
        return jnp.broadcast_to(row, (SUBLANES, LANES))

    def seq(bi):
        def grp(gi, carry):
            rows = _rows(bi * Lc + gi * SUBLANES, SUBLANES)
            for cb in range(NB):
                cl = slice(cb * LANES, (cb + 1) * LANES)
                x_r, x_i = scan8(xr_scr[cb, rows, :], xi_scr[cb, rows, :], cb)
                h_r, h_i = cmul_add(pr_ref[:, cl], pi_ref[:, cl], bcast(hr_ref[bi, :, cl]),
                                    bcast(hi_ref[bi, :, cl]), x_r, x_i)
                xr_scr[cb, rows, :] = h_r
                xi_scr[cb, rows, :] = h_i
                hr_ref[bi, :, cl] = h_r[SUBLANES - 1:, :]
                hi_ref[bi, :, cl] = h_i[SUBLANES - 1:, :]
            return carry
        lax.fori_loop(0, Lc // SUBLANES, grp, 0)

    _for_each_seq(bb, seq)

    ys = []
    for kb in range(KB):
        hrb = jnp.concatenate([xr_scr[kb * BPK + t] for t in range(BPK)], axis=1).astype(BF16)
        hib = jnp.concatenate([xi_scr[kb * BPK + t] for t in range(BPK)], axis=1).astype(BF16)
        ys.append(_dot(hrb, cwr_ref[kb]) + _dot(hib, cwi_ref[kb]))
    y = jnp.concatenate(ys, axis=1) + d_ref[...] * u
    g = jax.nn.gelu(y)
    o_ref[...] = g * jax.nn.sigmoid(_dot(g.astype(BF16), wglu_ref[...]) + bglu_ref[...])


def _s5(z, B, T, h0r, h0i, tabs, d, wglu, bglu):
    Lc = min(T, 256)
    nC = T // Lc
    bb = 1 if nC > 1 else min(B, max(1, 128 // T))
    bw, ar, ai, pr, pi, cwr, cwi = tabs
    R = bb * Lc
    W = d.shape[1]
    NS = h0r.shape[-1]
    full = lambda a: pl.BlockSpec(a.shape, lambda i, c: (0,) * a.ndim)
    hspec = pl.BlockSpec((bb, 1, NS), lambda i, c: (i, 0, 0))
    return pl.pallas_call(
        functools.partial(_s5_kernel, bb=bb, Lc=Lc),
        grid=(B // bb, nC),
        in_specs=[pl.BlockSpec((R, W), lambda i, c: (i * nC + c, 7)),
                  full(bw), full(ar), full(ai), full(pr), full(pi), full(cwr), full(cwi),
                  full(d), full(wglu), full(bglu), hspec, hspec],
        out_specs=[pl.BlockSpec((R, W), lambda i, c: (i * nC + c, 0)), hspec, hspec],
        out_shape=[jax.ShapeDtypeStruct((B * T, W), F32),
                   jax.ShapeDtypeStruct((B, 1, NS), F32),
                   jax.ShapeDtypeStruct((B, 1, NS), F32)],
        scratch_shapes=[pltpu.VMEM((NS // LANES, R, LANES), F32), pltpu.VMEM((NS // LANES, R, LANES), F32)],
        compiler_params=_cparams(("parallel", "arbitrary")),
        name="s5",
    )(z, bw, ar, ai, pr, pi, cwr, cwi, d, wglu, bglu, h0r, h0i)


def _s5_tables(a_re, a_im, log_dt, b_re, b_im, c_re, c_im):
    G, P = a_re.shape
    dt = jnp.exp(log_dt.astype(F32))[:, None]
    lam_re = -jnp.abs(a_re.astype(F32))
    lam_im = a_im.astype(F32)

    def power(n):
        n = jnp.asarray(n, F32)[:, None, None]
        mag = jnp.exp(lam_re * dt * n)
        return ((mag * jnp.cos(lam_im * dt * n)).reshape(-1, G * P),
                (mag * jnp.sin(lam_im * dt * n)).reshape(-1, G * P))

    mag = jnp.exp(lam_re * dt)
    ab_re = mag * jnp.cos(lam_im * dt)
    ab_im = mag * jnp.sin(lam_im * dt)
    den = lam_re * lam_re + lam_im * lam_im
    co_re = ((ab_re - 1.0) * lam_re + ab_im * lam_im) / den
    co_im = (ab_im * lam_re - (ab_re - 1.0) * lam_im) / den
    br, bi = b_re.astype(F32), b_im.astype(F32)
    bb_re = co_re[..., None] * br - co_im[..., None] * bi
    bb_im = co_re[..., None] * bi + co_im[..., None] * br
    gpb = LANES // S5_GROUP
    KB = G // gpb
    eye = jnp.eye(gpb, dtype=F32)
    blk = lambda w: jnp.einsum('kgpc,gh->kgchp', w.reshape(KB, gpb, P, S5_GROUP), eye).reshape(KB, LANES, gpb * P)
    bw = jnp.concatenate([blk(bb_re), blk(bb_im)], axis=-1).astype(BF16)
    cblk = lambda w: jnp.einsum('kgcp,gh->kgphc', w.reshape(KB, gpb, S5_GROUP, P), eye).reshape(KB, gpb * P, LANES)
    cwr = cblk(c_re.astype(F32)).astype(BF16)
    cwi = cblk(-c_im.astype(F32)).astype(BF16)
    rows = jnp.arange(SUBLANES)
    ars, ais = [], []
    for lv in range(3):
        r, i = power([1 << lv])
        keep = (rows >= (1 << lv))[:, None]
        ars.append(jnp.where(keep, r, 0.0))
        ais.append(jnp.where(keep, i, 0.0))
    pr, pi = power(np.arange(1, SUBLANES + 1))
    return bw, jnp.stack(ars), jnp.stack(ais), pr, pi, cwr, cwi


def _branch_kernel(or_ref, om_ref, os_ref, gr_ref, gm_ref, gs_ref, wr_ref, wm_ref, ws_ref, o_ref):
    br = _dot(or_ref[...].astype(BF16), wr_ref[...])
    bm = _dot(om_ref[...].astype(BF16), wm_ref[...])
    bs = _dot(os_ref[...].astype(BF16), ws_ref[...])
    merged = (jax.nn.sigmoid(gr_ref[...]) * br + jax.nn.sigmoid(gm_ref[...]) * bm
              + jax.nn.sigmoid(gs_ref[...]) * bs)
    o_ref[...] = merged.astype(BF16)


def _branch(o_r, o_m, o_s, z, wr, wm, ws, l):
    M, W = o_r.shape
    D = wr.shape[2]
    TM, TN = min(M, 512), 1024
    nJ = D // TN
    ospec = pl.BlockSpec((TM, W), lambda i, j: (i, 0))
    gspec = lambda k: pl.BlockSpec((TM, TN), lambda i, j: (i, 8 + k * nJ + j))
    wspec = pl.BlockSpec((None, W, TN), lambda i, j: (l, 0, j))
    return pl.pallas_call(
        _branch_kernel,
        grid=(M // TM, nJ),
        in_specs=[ospec, ospec, ospec, gspec(0), gspec(1), gspec(2), wspec, wspec, wspec],
        out_specs=pl.BlockSpec((TM, TN), lambda i, j: (i, j)),
        out_shape=jax.ShapeDtypeStruct((M, D), BF16),
        compiler_params=_cparams(("parallel", "parallel")),
        name="branch",
    )(o_r, o_m, o_s, z, z, z, wr, wm, ws)


def _wo_kernel(m_ref, w_ref, x_ref, g_ref, o_ref):
    y = _dot(m_ref[...], w_ref[...])
    o_ref[...] = x_ref[...] + g_ref[...] * y.reshape(o_ref.shape)


def _wo(merged, w_o, x, mod, l, moff):
    B, T, D = x.shape
    bb, tt = _tile_cfg(B, T, 512)
    TM, nT = bb * tt, T // tt
    TN = 1024
    xspec = pl.BlockSpec((bb, tt, TN), lambda i, j: (i // nT, i % nT, j))
    return pl.pallas_call(
        _wo_kernel,
        grid=((B // bb) * nT, D // TN),
        in_specs=[pl.BlockSpec((TM, D), lambda i, j: (i, 0)),
                  pl.BlockSpec((None, D, TN), lambda i, j: (l, 0, j)),
                  xspec,
                  pl.BlockSpec((None, bb, 1, TN), lambda i, j: (l, moff // bb + i // nT, 0, 2 * (D // TN) + j))],
        out_specs=xspec,
        out_shape=jax.ShapeDtypeStruct((B, T, D), F32),
        compiler_params=_cparams(("parallel", "parallel")),
        name="wo",
    )(merged, w_o, x, mod)


def _prep_u_kernel(u_ref, o_ref):
    kq = o_ref.shape[-1]
    for j in range(o_ref.shape[0]):
        o_ref[j] = u_ref[:, j * kq:(j + 1) * kq].astype(BF16)


def _prep_u(peer_u):
    L, NE, D = peer_u.shape
    R = 2 * PEER_MC
    kq = D // PEER_KQ
    return pl.pallas_call(
        _prep_u_kernel,
        grid=(L, NE // R),
        in_specs=[pl.BlockSpec((None, R, D), lambda l, p: (l, p, 0))],
        out_specs=pl.BlockSpec((None, None, PEER_KQ, R, kq), lambda l, p: (l, p, 0, 0, 0)),
        out_shape=jax.ShapeDtypeStruct((L, NE // R, PEER_KQ, R, kq), BF16),
        compiler_params=_cparams(("parallel", "parallel")),
        name="prep_u",
    )(peer_u)


def _prep_v_kernel(v_ref, o_ref):
    kq = o_ref.shape[1]
    for j in range(o_ref.shape[0]):
        o_ref[j] = v_ref[:, j * kq:(j + 1) * kq].T.astype(BF16)


def _prep_v(peer_v):
    L, NE, D = peer_v.shape
    kq = D // PEER_KQ
    return pl.pallas_call(
        _prep_v_kernel,
        grid=(L, NE // PEER_MC),
        in_specs=[pl.BlockSpec((None, PEER_MC, D), lambda l, c: (l, c, 0))],
        out_specs=pl.BlockSpec((None, None, PEER_KQ, kq, PEER_MC), lambda l, c: (l, c, 0, 0, 0)),
        out_shape=jax.ShapeDtypeStruct((L, NE // PEER_MC, PEER_KQ, kq, PEER_MC), BF16),
        compiler_params=_cparams(("parallel", "parallel")),
        name="prep_v",
    )(peer_v)


def _peer_q_kernel(x_ref, nw_ref, sh_ref, sc_ref, wq_ref, k1_ref, k2_ref, h_ref, st_ref):
    h = _norm_mod(x_ref[...], nw_ref[...], sc_ref[...], sh_ref[...])
    hb = h.reshape(-1, h.shape[-1]).astype(BF16)
    kq = h_ref.shape[-1]
    for j in range(h_ref.shape[0]):
        h_ref[j] = hb[:, j * kq:(j + 1) * kq]
    q = _dot(hb, wq_ref[...])
    dq = k1_ref.shape[-1]
    for hd in range(PEER_HEADS):
        q1 = q[:, (2 * hd) * dq:(2 * hd + 1) * dq]
        q2 = q[:, (2 * hd + 1) * dq:(2 * hd + 2) * dq]
        st_ref[2 * hd] = _dot_nt(k1_ref[hd], q1, precision=lax.Precision.HIGHEST)
        st_ref[2 * hd + 1] = _dot_nt(k2_ref[hd], q2, precision=lax.Precision.HIGHEST)


def _peer_q(x, mod, l, moff, nw, wq, k1, k2):
    B, T, D = x.shape
    bb, tt = _tile_cfg(B, T, 256)
    TM, nT = bb * tt, T // tt
    full = lambda a: pl.BlockSpec(a.shape, lambda i: (0,) * a.ndim)
    mmap = lambda k: (lambda i: (l, moff // bb + i // nT, 0, k))
    return pl.pallas_call(
        _peer_q_kernel,
        grid=((B // bb) * nT,),
        in_specs=[pl.BlockSpec((bb, tt, D), lambda i: (i // nT, i % nT, 0)),
                  pl.BlockSpec((1, D), lambda i: (0, 0)),
                  pl.BlockSpec((None, bb, 1, D), mmap(3)),
                  pl.BlockSpec((None, bb, 1, D), mmap(4)),
                  pl.BlockSpec((None,) + wq.shape[1:], lambda i: (l, 0, 0)), full(k1), full(k2)],
        out_specs=[pl.BlockSpec((PEER_KQ, TM, D // PEER_KQ), lambda i: (0, i, 0)),
                   pl.BlockSpec((2 * PEER_HEADS, N_KEYS, TM), lambda i: (0, 0, i))],
        out_shape=[jax.ShapeDtypeStruct((PEER_KQ, B * T, D // PEER_KQ), BF16),
                   jax.ShapeDtypeStruct((2 * PEER_HEADS, N_KEYS, B * T), F32)],
        compiler_params=_cparams(("parallel",)),
        name="peer_q",
    )(x, nw, mod, mod, wq, k1, k2)


def _top_values(cur, n):
    vals = []
    for _ in range(n):
        m = jnp.max(cur, axis=0, keepdims=True)
        vals.append(m)
        cur = jnp.where(cur == m, -jnp.inf, cur)
    return vals


def _peer_sel_kernel(st_ref, e1_ref, e2_ref, th_ref):
    ts = st_ref.shape[-1]
    neg = jnp.full((1, ts), -jnp.inf, F32)
    n = PEER_TOPK + 1
    ths = []
    for hd in range(PEER_HEADS):
        s1 = st_ref[2 * hd]
        s2 = st_ref[2 * hd + 1]
        v1 = _top_values(s1, n)
        v2 = _top_values(s2, n)
        rows = [v1[r] + v2[c] for r in range(n) for c in range(n // (r + 1))]
        rows += [neg] * (-len(rows) % SUBLANES)
        top = _top_values(jnp.concatenate(rows, axis=0), n)
        den = jnp.exp(top[0] - top[0])
        for t in top[1:PEER_TOPK]:
            den = den + jnp.exp(t - top[0])
        e1_ref[hd] = jnp.exp(s1 - v1[0]) / den
        e2_ref[hd] = jnp.exp(s2 - v2[0])
        ths.append(jnp.exp(0.5 * (top[PEER_TOPK - 1] + top[PEER_TOPK]) - top[0]) / den)
    th_ref[...] = jnp.concatenate(ths, axis=0)


def _peer_sel(st):
    H2, K, M = st.shape
    TS = min(M, 256)
    hspec = pl.BlockSpec((PEER_HEADS, K, TS), lambda i: (0, 0, i))
    return pl.pallas_call(
        _peer_sel_kernel,
        grid=(M // TS,),
        in_specs=[pl.BlockSpec((H2, K, TS), lambda i: (0, 0, i))],
        out_specs=[hspec, hspec, pl.BlockSpec((PEER_HEADS, TS), lambda i: (0, i))],
        out_shape=[jax.ShapeDtypeStruct((PEER_HEADS, K, M), F32),
                   jax.ShapeDtypeStruct((PEER_HEADS, K, M), F32),
                   jax.ShapeDtypeStruct((PEER_HEADS, M), F32)],
        compiler_params=_cparams(("parallel",)),
        name="peer_sel",
    )(st)


def _peer_exp_kernel(h_ref, u_ref, vta_ref, vtb_ref, e1_ref, e2_ref, th_ref, x_ref, g_ref, o_ref,
                     acc_scr, sc0, sc1, zt0, zt1, *, mc, nc):
    s = pl.program_id(1)
    slabs = mc // N_KEYS

    @pl.when(s == 0)
    def _():
        acc_scr[...] = jnp.zeros(acc_scr.shape, F32)
        sc1[...] = jnp.zeros(sc1.shape, F32)
        zt0[...] = jnp.zeros(zt0.shape, BF16)
        zt1[...] = jnp.zeros(zt1.shape, BF16)

    nq = u_ref.shape[0]
    spp = slabs // nq
    assert spp * nq == slabs

    def substep(half, sc_a, sc_b, zt_b, zt_c, chunk_b):
        c = jnp.clip(chunk_b, 0, nc - 1)
        cols = slice(half * mc, (half + 1) * mc)
        sc_a[...] = jnp.zeros(sc_a.shape, F32)

        def piece(j, carry):
            acc_scr[j] += _dot((vta_ref, vtb_ref)[half][j], zt_c[...])
            for sl in range(spp):
                rs = pl.ds(pl.multiple_of((j * spp + sl) * N_KEYS, N_KEYS), N_KEYS)
                i1 = c * slabs + j * spp + sl
                for tc in range(sc_b.shape[1] // LANES):
                    tl = slice(tc * LANES, (tc + 1) * LANES)
                    w = jnp.zeros((N_KEYS, LANES), F32)
                    for hd in range(PEER_HEADS):
                        p = e1_ref[hd, pl.ds(i1, 1), :][:, tl] * e2_ref[hd, :, tl]
                        w = jnp.where(p >= th_ref[pl.ds(hd, 1), tl], w + p, w)
                    zt_b[rs, tl] = (jax.nn.gelu(sc_b[rs, tl]) * w).astype(BF16)
            sc_a[...] += _dot_nt(u_ref[j, cols, :], h_ref[j])
            return carry

        lax.fori_loop(0, nq, piece, 0)

    substep(0, sc0, sc1, zt1, zt0, 2 * s - 1)
    substep(1, sc1, sc0, zt0, zt1, 2 * s)

    @pl.when(s == pl.num_programs(1) - 1)
    def _():
        o_ref[...] = x_ref[...] + g_ref[...] * acc_scr[...].reshape(-1, acc_scr.shape[-1]).T.reshape(o_ref.shape)


def _peer_exp(hb, u, vt, e1, e2, th, x, mod, l, moff):
    B, T, D = x.shape
    bb, tt = _tile_cfg(B, T, PEER_TB)
    TB, nT = bb * tt, T // tt
    MC = PEER_MC
    NP = u.shape[1]
    NC = vt.shape[1]
    assert NC == 2 * NP and u.shape[3] == 2 * MC and vt.shape[4] == MC
    xspec = pl.BlockSpec((bb, tt, D), lambda i, s: (i // nT, i % nT, 0))
    hspec = pl.BlockSpec((PEER_HEADS, N_KEYS, TB), lambda i, s: (0, 0, i))
    return pl.pallas_call(
        functools.partial(_peer_exp_kernel, mc=MC, nc=NC),
        grid=((B // bb) * nT, NP + 1),
        in_specs=[pl.BlockSpec((PEER_KQ, TB, D // PEER_KQ), lambda i, s: (0, i, 0)),
                  pl.BlockSpec((None, None, PEER_KQ, 2 * MC, D // PEER_KQ),
                               lambda i, s: (l, jnp.minimum(s, NP - 1), 0, 0, 0)),
                  pl.BlockSpec((None, None, PEER_KQ, D // PEER_KQ, MC),
                               lambda i, s: (l, 2 * jnp.maximum(s - 1, 0), 0, 0, 0)),
                  pl.BlockSpec((None, None, PEER_KQ, D // PEER_KQ, MC),
                               lambda i, s: (l, 2 * jnp.maximum(s - 1, 0) + 1, 0, 0, 0)),
                  hspec, hspec,
                  pl.BlockSpec((PEER_HEADS, TB), lambda i, s: (0, i)),
                  xspec,
                  pl.BlockSpec((None, bb, 1, D), lambda i, s: (l, moff // bb + i // nT, 0, 5))],
        out_specs=xspec,
        out_shape=jax.ShapeDtypeStruct((B, T, D), F32),
        scratch_shapes=[pltpu.VMEM((PEER_KQ, D // PEER_KQ, TB), F32),
                        pltpu.VMEM((MC, TB), F32), pltpu.VMEM((MC, TB), F32),
                        pltpu.VMEM((MC, TB), BF16), pltpu.VMEM((MC, TB), BF16)],
        compiler_params=_cparams(("parallel", "arbitrary")),
        name="peer_exp",
    )(hb, u, vt, vt, e1, e2, th, x, mod)


def _final_kernel(x_ref, w_ref, o_ref):
    x = x_ref[...]
    ms = jnp.mean(x * x, axis=-1, keepdims=True)
    o_ref[...] = x * lax.rsqrt(ms + EPS) * w_ref[...]


def _final_norm(x, w):
    B, T, D = x.shape
    bb, tt = _tile_cfg(B, T, 512)
    nT = T // tt
    xspec = pl.BlockSpec((bb, tt, D), lambda i: (i // nT, i % nT, 0))
    return pl.pallas_call(
        _final_kernel,
        grid=((B // bb) * nT,),
        in_specs=[xspec, pl.BlockSpec((1, D), lambda i: (0, 0))],
        out_specs=xspec,
        out_shape=jax.ShapeDtypeStruct((B, T, D), F32),
        compiler_params=_cparams(("parallel",)),
        name="final_norm",
    )(x, w)


def _trunk(x, mod, moff, pos0, big_states, states, lw, final_w):
    B, T, D = x.shape
    depth = len(lw)
    rtabs = _ret_tables(T, pos0)
    ret_in, c_in = big_states
    n_ret = n_c = None
    new = []
    for l, (st, w) in enumerate(zip(states, lw)):
        s_n, s_m, s_conv, s_hr, s_hi = st
        z, zg = _in_proj(x, mod, l, moff, w['n1'], w['w_a'], w['w_b'], w['w_gate'])
        o_r, n_ret = _retention(z, B, T, ret_in, min(l, ret_in.shape[0] - 1), l, depth, n_ret, rtabs, w['ret_gn'])
        o_m, n_c, n_n, n_m, n_conv = _mlstm(z, zg, B, T, c_in, min(l, c_in.shape[0] - 1), l, depth, n_c,
                                            (s_n, s_m, s_conv), w['conv_w'], w['conv_b'],
                                            w['wqk'], w['gate_b'], w['m_gn'], w['m_skip'])
        o_s, n_hr, n_hi = _s5(z, B, T, s_hr, s_hi, w['s5_tabs'], w['s5_d'], w['s5_wglu'], w['s5_bglu'])
        merged = _branch(o_r, o_m, o_s, z, w['w_ret_out'], w['w_mlstm_out'], w['w_s5_out'], l)
        x = _wo(merged, w['w_o'], x, mod, l, moff)
        hb, sc = _peer_q(x, mod, l, moff, w['n2'], w['peer_wq'], w['peer_k1'], w['peer_k2'])
        e1, e2, th = _peer_sel(sc)
        x = _peer_exp(hb, w['peer_u'], w['peer_vt'], e1, e2, th, x, mod, l, moff)
        new.append((n_n, n_m[..., 0], n_conv, n_hr, n_hi))
    y = _final_norm(x, final_w)
    return y, [n_ret, n_c] + [jnp.stack([s[i] for s in new]) for i in range(5)]


def kernel(x_prompt, x_sample, state_ret, state_mlstm_c, state_mlstm_n, state_mlstm_m, state_mlstm_conv,
           state_s5_re, state_s5_im, c_prompt, c_sample, ada_w, ada_b, norm1_w, norm2_w, final_norm_w,
           w_in, ret_gn_w, w_ret_out, mlstm_conv_w, mlstm_conv_b, mlstm_wq, mlstm_wk, mlstm_b_i, mlstm_b_f,
           mlstm_gn_w, mlstm_skip, w_mlstm_out, s5_a_re, s5_a_im, s5_log_dt, s5_b_re, s5_b_im, s5_c_re,
           s5_c_im, s5_d, s5_w_glu, s5_b_glu, w_s5_out, w_o, peer_wq, peer_k1, peer_k2, peer_u, peer_v):
    depth = w_in.shape[0]
    Bp, Tp, D = x_prompt.shape
    Bs, Ts, _ = x_sample.shape
    W = HEADS * DH
    G, P = s5_a_re.shape[1:]
    NS = G * P
    past_len = 16384

    pad = -(Bs + Bp) % SUBLANES
    c_all = jnp.concatenate([c_sample, c_prompt, jnp.zeros((pad, D), F32)], axis=0)
    mod = _ada(c_all, ada_w, ada_b)
    mod = mod.reshape(depth, c_all.shape[0], 1, ada_w.shape[2])

    a_end = 7 * W
    g_end = a_end + 2 * HEADS
    w_ret_out_b, w_mlstm_out_b, w_s5_out_b = w_ret_out.astype(BF16), w_mlstm_out.astype(BF16), w_s5_out.astype(BF16)
    w_o_b, peer_wq_b = w_o.astype(BF16), peer_wq.astype(BF16)
    u_tiles = _prep_u(peer_u)
    vt_tiles = _prep_v(peer_v)
    lw = []
    for l in range(depth):
        w_gate = jnp.pad(w_in[l, :, a_end:g_end], ((0, 0), (0, LANES - 2 * HEADS))).astype(BF16)
        gate_b = jnp.pad(jnp.concatenate([mlstm_b_i[l], mlstm_b_f[l]]), (0, LANES - 2 * HEADS)).reshape(1, LANES)
        lw.append(dict(
            n1=norm1_w[l].reshape(1, D), n2=norm2_w[l].reshape(1, D),
            w_a=w_in[l, :, :a_end].astype(BF16), w_b=w_in[l, :, g_end:].astype(BF16), w_gate=w_gate, gate_b=gate_b,
            ret_gn=ret_gn_w[l].reshape(1, W),
            conv_w=mlstm_conv_w[l], conv_b=mlstm_conv_b[l].reshape(1, W),
            wqk=jnp.concatenate([mlstm_wq[l], mlstm_wk[l]], axis=-1).astype(BF16),
            m_gn=mlstm_gn_w[l].reshape(1, W), m_skip=mlstm_skip[l].reshape(1, W),
            s5_tabs=_s5_tables(s5_a_re[l], s5_a_im[l], s5_log_dt[l], s5_b_re[l], s5_b_im[l], s5_c_re[l], s5_c_im[l]),
            s5_d=s5_d[l].reshape(1, -1), s5_wglu=s5_w_glu[l].astype(BF16), s5_bglu=s5_b_glu[l].reshape(1, -1),
            w_ret_out=w_ret_out_b, w_mlstm_out=w_mlstm_out_b, w_s5_out=w_s5_out_b, w_o=w_o_b,
            peer_wq=peer_wq_b, peer_k1=peer_k1[l], peer_k2=peer_k2[l],
            peer_u=u_tiles, peer_vt=vt_tiles))

    def zero_states(B):
        return (jnp.zeros((B, HEADS, DH), F32), jnp.zeros((B, HEADS, LANES), F32),
                jnp.zeros((B, CONV_W - 1, W), F32), jnp.zeros((B, 1, NS), F32), jnp.zeros((B, 1, NS), F32))

    zero_big = jnp.zeros((1, Bp, HEADS, DH, DH), F32)
    prompt_states = [zero_states(Bp) for _ in range(depth)]
    sample_states = [(state_mlstm_n[l],
                      jnp.broadcast_to(state_mlstm_m[l][..., None], (Bs, HEADS, LANES)),
                      state_mlstm_conv[l], state_s5_re[l].reshape(Bs, 1, NS), state_s5_im[l].reshape(Bs, 1, NS))
                     for l in range(depth)]

    y_p, ps = _trunk(x_prompt, mod, Bs, 0, (zero_big, zero_big), prompt_states, lw, final_norm_w.reshape(1, D))
    y_s, ss = _trunk(x_sample, mod, 0, past_len, (state_ret, state_mlstm_c), sample_states, lw,
                     final_norm_w.reshape(1, D))

    def unpack(st, B):
        r, c, n, m, conv, hr, hi = st
        return (r, c, n, m, conv, hr.reshape(depth, B, G, P), hi.reshape(depth, B, G, P))

    return (y_p, y_s) + unpack(ps, Bp) + unpack(ss, Bs)
```

```python
import functools
import math

import jax
import jax.numpy as jnp
import numpy as np
from jax import lax
from jax.experimental import pallas as pl
from jax.experimental.pallas import tpu as pltpu

F32 = jnp.float32
BF16 = jnp.bfloat16

EPS = 1e-6
ROPE_BASE = 10000.0
HEADS = 8
DH = 128
CONV_W = 4
S5_GROUP = 16
S5_STATE = 64
PEER_HEADS = 8
PEER_TOPK = 16
N_KEYS = 128
CHUNK = 128
LANES = 128
SUBLANES = 8
VMEM_LIMIT = 56 * 1024 * 1024
RET_GROUP = 8
MLSTM_GROUP = 2
PEER_TB = 512
PEER_MC = 512
PEER_KQ = 4


def _cparams(sem):
    return pltpu.CompilerParams(dimension_semantics=sem, vmem_limit_bytes=VMEM_LIMIT)


def _tile_cfg(B, T, target):
    if T >= target:
        bb, tt = 1, target
    else:
        bb, tt = min(B, target // T), T
    assert T % tt == 0 and B % bb == 0
    return bb, tt


def _act_dtype(T):
    return BF16 if T % CHUNK == 0 else F32


def _silu(x):
    return x * jax.nn.sigmoid(x)


def _norm_mod(x, w, sc, sh):
    ms = jnp.mean(x * x, axis=-1, keepdims=True)
    y = x * lax.rsqrt(ms + EPS) * w
    return y * (1.0 + sc) + sh


def _head_norm(x):
    mu = jnp.mean(x, axis=-1, keepdims=True)
    xc = x - mu
    var = jnp.mean(xc * xc, axis=-1, keepdims=True)
    return xc * lax.rsqrt(var + EPS)


def _dot(a, b):
    return jnp.dot(a, b, preferred_element_type=F32)


def _dot_nt(a, b, **kw):
    return lax.dot_general(a, b, (((1,), (1,)), ((), ())), preferred_element_type=F32, **kw)


def _dot_tn(a, b):
    return lax.dot_general(a, b, (((0,), (0,)), ((), ())), preferred_element_type=F32)


def _rows(start, n):
    if isinstance(start, int):
        return pl.ds(start, n)
    return pl.ds(pl.multiple_of(start, SUBLANES), n)


def _for_each_group(bb, gs, body):
    gs = min(gs, bb)
    assert bb % gs == 0
    if bb == gs:
        body(list(range(bb)))
    else:
        def f(gi, c):
            body([gi * gs + i for i in range(gs)])
            return c
        lax.fori_loop(0, bb // gs, f, 0)


def _for_each_seq(bb, body, unroll=1):
    if bb == 1:
        body(0)
    else:
        def f(bi, c):
            body(bi)
            return c
        lax.fori_loop(0, bb, f, 0, unroll=unroll)


def _ada_kernel(c_ref, w_ref, b_ref, o_ref):
    a = _silu(c_ref[...]).astype(BF16)
    o_ref[...] = _dot(a, w_ref[...].astype(BF16)) + b_ref[...]


def _ada(c_all, ada_w, ada_b):
    L, D, N = ada_w.shape
    Bc = c_all.shape[0]
    TN = 1024
    return pl.pallas_call(
        _ada_kernel,
        grid=(L, N // TN),
        in_specs=[pl.BlockSpec((Bc, D), lambda l, j: (0, 0)),
                  pl.BlockSpec((None, D, TN), lambda l, j: (l, 0, j)),
                  pl.BlockSpec((None, 1, TN), lambda l, j: (l, 0, j))],
        out_specs=pl.BlockSpec((None, Bc, TN), lambda l, j: (l, 0, j)),
        out_shape=jax.ShapeDtypeStruct((L, Bc, N), F32),
        compiler_params=_cparams(("parallel", "parallel")),
        name="ada",
    )(c_all, ada_w, ada_b.reshape(L, 1, N))


def _in_kernel(x_ref, nw_ref, sh_ref, sc_ref, wa_ref, wb_ref, wg_ref, z_ref, zg_ref, h_scr, *, na):
    j = pl.program_id(1)

    @pl.when(j == 0)
    def _():
        h = _norm_mod(x_ref[...], nw_ref[...], sc_ref[...], sh_ref[...])
        hb = h.reshape(h_scr.shape).astype(BF16)
        h_scr[...] = hb
        zg_ref[...] = _dot(hb, wg_ref[...])

    @pl.when(j < na)
    def _():
        z_ref[...] = _dot(h_scr[...], wa_ref[...]).astype(z_ref.dtype)

    @pl.when(j >= na)
    def _():
        z_ref[...] = _dot(h_scr[...], wb_ref[...]).astype(z_ref.dtype)


def _in_proj(x, mod, l, moff, nw, wa, wb, wg):
    B, T, D = x.shape
    bb, tt = _tile_cfg(B, T, 1024)
    TM, nT = bb * tt, T // tt
    nI = (B // bb) * nT
    TN = 1024
    na = wa.shape[1] // TN
    N = wa.shape[1] + wb.shape[1]
    xmap = lambda i, j: (i // nT, i % nT, 0)
    mmap = lambda k: (lambda i, j: (l, moff // bb + i // nT, 0, k))
    return pl.pallas_call(
        functools.partial(_in_kernel, na=na),
        grid=(nI, N // TN),
        in_specs=[pl.BlockSpec((bb, tt, D), xmap),
                  pl.BlockSpec((1, D), lambda i, j: (0, 0)),
                  pl.BlockSpec((None, bb, 1, D), mmap(0)),
                  pl.BlockSpec((None, bb, 1, D), mmap(1)),
                  pl.BlockSpec((D, TN), lambda i, j: (0, jnp.minimum(j, na - 1))),
                  pl.BlockSpec((D, TN), lambda i, j: (0, jnp.maximum(j - na, 0))),
                  pl.BlockSpec((D, LANES), lambda i, j: (0, 0))],
        out_specs=[pl.BlockSpec((TM, TN), lambda i, j: (i, j)),
                   pl.BlockSpec((TM, LANES), lambda i, j: (i, 0))],
        out_shape=[jax.ShapeDtypeStruct((B * T, N), _act_dtype(T)),
                   jax.ShapeDtypeStruct((B * T, LANES), F32)],
        scratch_shapes=[pltpu.VMEM((TM, D), BF16)],
        compiler_params=_cparams(("parallel", "arbitrary")),
        name="in_proj",
    )(x, nw, mod, mod, wa, wb, wg)


def _ret_kernel(*refs, bb, L):
    (zq_ref, zk_ref, zv_ref, zg_ref, cq_ref, sq_ref, ck_ref, sk_ref, intra_ref, cross_ref,
     kdec_ref, cdec_ref, gn_ref, s0_ref) = refs[:14]
    o_ref, s_ref = refs[-2:]

    @pl.when(pl.program_id(1) == 0)
    def _():
        s_ref[...] = s0_ref[...]

    cq, sq, ck, sk = cq_ref[...], sq_ref[...], ck_ref[...], sk_ref[...]

    def group(bis):
        chains = [(i, h) for i in range(len(bis)) for h in range(HEADS)]
        rows = [_rows(bi * L, L) for bi in bis]
        cols = [slice(h * DH, (h + 1) * DH) for h in range(HEADS)]
        q, k, v, s = {}, {}, {}, {}
        for c in chains:
            i, h = c
            qf = zq_ref[rows[i], cols[h]].astype(F32)
            kf = zk_ref[rows[i], cols[h]].astype(F32)
            q[c] = (qf * cq + pltpu.roll(qf, DH // 2, 1) * sq).astype(BF16)
            k[c] = kf * ck + pltpu.roll(kf, DH // 2, 1) * sk
            v[c] = zv_ref[rows[i], cols[h]].astype(BF16)
            s[c] = s_ref[bis[i], h]
        att = {c: _dot_nt(q[c], k[c].astype(BF16)) for c in chains}
        inter = {c: _dot(q[c], s[c].astype(BF16)) for c in chains}
        upd = {c: _dot_tn((k[c] * kdec_ref[c[1]]).astype(BF16), v[c]) for c in chains}
        o = {c: _dot((att[c] * intra_ref[c[1]]).astype(BF16), v[c]) + inter[c] * cross_ref[c[1]] for c in chains}
        for c in chains:
            i, h = c
            s_ref[bis[i], h] = cdec_ref[h] * s[c] + upd[c]
            g = zg_ref[rows[i], cols[h]].astype(F32)
            o_ref[rows[i], cols[h]] = (_head_norm(o[c]) * gn_ref[:, cols[h]] * _silu(g)).astype(o_ref.dtype)

    _for_each_group(bb, RET_GROUP, group)


def _layer_state_io(s_in, lin, l, depth, prev, bb, n_inputs, out_index):
    tail = s_in.shape[2:]
    zeros = (0,) * len(tail)
    in_spec = pl.BlockSpec((None, bb) + tail, lambda i, c: (lin, i) + zeros)
    out_spec = pl.BlockSpec((None, bb) + tail, lambda i, c: (l, i) + zeros)
    out_shape = jax.ShapeDtypeStruct((depth,) + s_in.shape[1:], s_in.dtype)
    extra_specs, extra_args, aliases = [], [], {}
    if prev is not None:
        extra_specs = [pl.BlockSpec(memory_space=pl.ANY)]
        extra_args = [prev]
        aliases = {n_inputs: out_index}
    return in_spec, out_spec, out_shape, extra_specs, extra_args, aliases


def _retention(z, B, T, s_in, lin, l, depth, s_prev, tabs, gn_w):
    L = CHUNK if T % CHUNK == 0 else T
    nC = T // L
    bb = 1 if nC > 1 else min(B, 8)
    R = bb * L
    W = HEADS * DH
    cq, sq, ck, sk, intra, cross, kdec, cdec = tabs
    zspec = lambda k: pl.BlockSpec((R, W), lambda i, c: (i * nC + c, k))
    tspec = pl.BlockSpec((L, DH), lambda i, c: (c, 0))
    full = lambda a: pl.BlockSpec(a.shape, lambda i, c: (0,) * a.ndim)
    s_ispec, s_ospec, s_shape, xspecs, xargs, aliases = _layer_state_io(s_in, lin, l, depth, s_prev, bb, 14, 1)
    return pl.pallas_call(
        functools.partial(_ret_kernel, bb=bb, L=L),
        grid=(B // bb, nC),
        in_specs=[zspec(0), zspec(1), zspec(2), zspec(3), tspec, tspec, tspec, tspec,
                  full(intra), full(cross), full(kdec), full(cdec), full(gn_w), s_ispec] + xspecs,
        out_specs=[pl.BlockSpec((R, W), lambda i, c: (i * nC + c, 0)), s_ospec],
        out_shape=[jax.ShapeDtypeStruct((B * T, W), _act_dtype(T)), s_shape],
        input_output_aliases=aliases,
        compiler_params=_cparams(("parallel", "arbitrary")),
        name="ret",
    )(z, z, z, z, cq, sq, ck, sk, intra, cross, kdec, cdec, gn_w, s_in, *xargs)


def _ret_tables(T, pos0):
    L = CHUNK if T % CHUNK == 0 else T
    half = DH // 2
    inv = jnp.exp(-math.log(ROPE_BASE) * jnp.arange(half, dtype=F32) / half)
    pos = jnp.arange(T, dtype=F32) + pos0
    ang = pos[:, None] * inv[None, :]
    cos, sin = jnp.cos(ang), jnp.sin(ang)
    c = jnp.concatenate([cos, cos], axis=-1)
    s = jnp.concatenate([-sin, sin], axis=-1)
    kscale = DH ** -0.5
    lg = jnp.log1p(-jnp.exp2(-5.0 - jnp.arange(HEADS, dtype=F32)))
    j = jnp.arange(L, dtype=F32)
    diff = j[:, None] - j[None, :]
    intra = jnp.where(diff >= 0, jnp.exp(lg[:, None, None] * jnp.maximum(diff, 0.0)), 0.0)
    cross = jnp.exp(lg[:, None] * (j + 1.0))
    kdec = jnp.exp(lg[:, None] * (L - 1.0 - j))
    cdec = jnp.exp(lg * L)
    bl = lambda a: jnp.broadcast_to(a[..., None], a.shape + (DH,))
    return (c, s, c * kscale, s * kscale, intra, bl(cross), bl(kdec), bl(cdec[:, None]))


def _mlstm_kernel(*refs, bb, L, last):
    (zu_ref, zv_ref, zo_ref, zg_ref, cw_ref, cb_ref, wqk_ref, gb_ref, gn_ref, sk_ref,
     c0_ref, n0_ref, m0_ref, conv0_ref) = refs[:14]
    o_ref, c_ref, n_ref, m_ref, conv_ref, xp_scr, tail_scr, ca_scr = refs[-8:]
    ci = pl.program_id(1)

    @pl.when(ci == 0)
    def _():
        c_ref[...] = c0_ref[...]
        n_ref[...] = n0_ref[...]
        m_ref[...] = m0_ref[...]
        tail_scr[...] = jnp.zeros(tail_scr.shape, F32)
        tail_scr[:, SUBLANES - (CONV_W - 1):, :] = conv0_ref[...]

    ri = lax.broadcasted_iota(jnp.int32, (L, L), 0)
    rj = lax.broadcasted_iota(jnp.int32, (L, L), 1)
    causal = rj <= ri
    tri = causal.astype(F32)
    lane = lax.broadcasted_iota(jnp.int32, (L, LANES), 1)
    kscale = DH ** -0.5

    def group(bis):
        ns = len(bis)
        rows = [_rows(bi * L, L) for bi in bis]
        cols = [slice(h * DH, (h + 1) * DH) for h in range(HEADS)]
        chains = [(i, h) for i in range(ns) for h in range(HEADS)]
        gates, csum, gates_t, csum_t, m_all, n_all = [], [], [], [], [], []
        for i, bi in enumerate(bis):
            xp_scr[bi, 0:SUBLANES, :] = tail_scr[bi]
            xp_scr[bi, SUBLANES:, :] = zu_ref[rows[i], :].astype(F32)
            cu = cb_ref[...]
            for t in range(CONV_W):
                cu = cu + xp_scr[bi, pl.ds(SUBLANES - (CONV_W - 1) + t, L), :] * cw_ref[pl.ds(t, 1), :]
            tail_scr[bi] = xp_scr[bi, L:L + SUBLANES, :]
            ca_scr[bi] = _silu(cu)
            gz = zg_ref[rows[i], :] + gb_ref[...]
            g = jnp.where(lane < HEADS, gz, jax.nn.log_sigmoid(gz))
            cs = jnp.dot(tri, g, preferred_element_type=F32, precision=lax.Precision.HIGHEST)
            gates.append(g)
            csum.append(cs)
            gates_t.append(g.T)
            csum_t.append(cs.T)
            m_all.append(m_ref[bi])
            n_all.append(n_ref[bi])
        ca = {c: ca_scr[bis[c[0]], :, cols[c[1]]] for c in chains}
        qk = {c: _dot(ca[c].astype(BF16), wqk_ref[c[1]]) for c in chains}
        q = {c: qk[c][:, :DH] for c in chains}
        k = {c: qk[c][:, DH:] * kscale for c in chains}
        qb = {c: q[c].astype(BF16) for c in chains}
        v = {c: zv_ref[rows[c[0]], cols[c[1]]].astype(BF16) for c in chains}
        c_prev = {c: c_ref[bis[c[0]], c[1]] for c in chains}
        sraw = {c: _dot_nt(qb[c], k[c].astype(BF16)) for c in chains}
        qc = {c: _dot(qb[c], c_prev[c].astype(BF16)) for c in chains}
        s, mt, w_int, m_prev, n_prev, b_col, i_col = {}, {}, {}, {}, {}, {}, {}
        for c in chains:
            i, h = c
            i_col[c] = gates[i][:, h:h + 1]
            b_col[c] = csum[i][:, HEADS + h:HEADS + h + 1]
            i_row = gates_t[i][h:h + 1, :]
            b_row = csum_t[i][HEADS + h:HEADS + h + 1, :]
            m_prev[c] = m_all[i][h:h + 1, :1]
            n_prev[c] = n_all[i][h:h + 1, :]
            dlog = jnp.where(causal, b_col[c] - b_row + i_row, -jnp.inf)
            inter = b_col[c] + m_prev[c]
            mt[c] = jnp.maximum(inter, jnp.max(dlog, axis=-1, keepdims=True))
            s[c] = sraw[c] * jnp.exp(dlog - mt[c])
            w_int[c] = jnp.exp(inter - mt[c])
        sv = {c: _dot(s[c].astype(BF16), v[c]) for c in chains}
        kt, dec, m_new = {}, {}, {}
        for c in chains:
            m_new[c] = mt[c][L - 1:L, :]
            b_last = b_col[c][L - 1:L, :]
            tail = jnp.exp(b_last - b_col[c] + i_col[c] - m_new[c])
            dec[c] = jnp.exp(b_last + m_prev[c] - m_new[c])
            kt[c] = k[c] * tail
        upd = {c: _dot_tn(kt[c].astype(BF16), v[c]) for c in chains}
        n_rows = [[] for _ in range(ns)]
        m_rows = [[] for _ in range(ns)]
        for c in chains:
            i, h = c
            num = sv[c] + w_int[c] * qc[c]
            den = (jnp.sum(s[c], axis=-1, keepdims=True)
                   + w_int[c] * jnp.sum(q[c] * n_prev[c], axis=-1, keepdims=True))
            hh = num / jnp.maximum(jnp.abs(den), jnp.exp(-mt[c]))
            c_ref[bis[i], h] = dec[c] * c_prev[c] + upd[c]
            n_rows[i].append(dec[c] * n_prev[c] + jnp.sum(kt[c], axis=0, keepdims=True))
            m_rows[i].append(jnp.broadcast_to(m_new[c], (1, LANES)))
            hm = jax.nn.sigmoid(zo_ref[rows[i], cols[h]].astype(F32)) * hh
            o_ref[rows[i], cols[h]] = (_head_norm(hm) * gn_ref[:, cols[h]]
                                       + sk_ref[:, cols[h]] * ca[c]).astype(o_ref.dtype)
        for i, bi in enumerate(bis):
            n_ref[bi] = jnp.concatenate(n_rows[i], axis=0)
            m_ref[bi] = jnp.concatenate(m_rows[i], axis=0)

    _for_each_group(bb, MLSTM_GROUP, group)

    @pl.when(ci == last)
    def _():
        conv_ref[...] = tail_scr[:, SUBLANES - (CONV_W - 1):, :]


def _mlstm(z, zg, B, T, c_in, lin, l, depth, c_prev, states, cw, cb, wqk, gb, gn_w, skip):
    L = CHUNK if T % CHUNK == 0 else T
    nC = T // L
    bb = 1 if nC > 1 else min(B, 8)
    R = bb * L
    W = HEADS * DH
    n0, m0, conv0 = states
    zspec = lambda k: pl.BlockSpec((R, W), lambda i, c: (i * nC + c, k))
    full = lambda a: pl.BlockSpec(a.shape, lambda i, c: (0,) * a.ndim)
    nspec = pl.BlockSpec((bb, HEADS, DH), lambda i, c: (i, 0, 0))
    vspec = pl.BlockSpec((bb, CONV_W - 1, W), lambda i, c: (i, 0, 0))
    c_ispec, c_ospec, c_shape, xspecs, xargs, aliases = _layer_state_io(c_in, lin, l, depth, c_prev, bb, 14, 1)
    return pl.pallas_call(
        functools.partial(_mlstm_kernel, bb=bb, L=L, last=nC - 1),
        grid=(B // bb, nC),
        in_specs=[zspec(4), zspec(5), zspec(6), pl.BlockSpec((R, LANES), lambda i, c: (i * nC + c, 0)),
                  full(cw), full(cb), full(wqk), full(gb), full(gn_w), full(skip),
                  c_ispec, nspec, nspec, vspec] + xspecs,
        out_specs=[pl.BlockSpec((R, W), lambda i, c: (i * nC + c, 0)), c_ospec, nspec, nspec, vspec],
        input_output_aliases=aliases,
        out_shape=[jax.ShapeDtypeStruct((B * T, W), _act_dtype(T)),
                   c_shape,
                   jax.ShapeDtypeStruct((B, HEADS, DH), F32),
                   jax.ShapeDtypeStruct((B, HEADS, LANES), F32),
                   jax.ShapeDtypeStruct((B, CONV_W - 1, W), F32)],
        scratch_shapes=[pltpu.VMEM((bb, L + SUBLANES, W), F32),
                        pltpu.VMEM((bb, SUBLANES, W), F32),
                        pltpu.VMEM((bb, L, W), F32)],
        compiler_params=_cparams(("parallel", "arbitrary")),
        name="mlstm",
    )(z, z, z, zg, cw, cb, wqk, gb, gn_w, skip, c_in, n0, m0, conv0, *xargs)


def _s5_kernel(zu_ref, bw_ref, ar_ref, ai_ref, pr_ref, pi_ref, cwr_ref, cwi_ref, d_ref, wglu_ref, bglu_ref,
               h0r_ref, h0i_ref, o_ref, hr_ref, hi_ref, xr_scr, xi_scr, *, bb, Lc):
    @pl.when(pl.program_id(1) == 0)
    def _():
        hr_ref[...] = h0r_ref[...]
        hi_ref[...] = h0i_ref[...]

    NB = xr_scr.shape[0]
    KB = bw_ref.shape[0]
    BPK = NB // KB
    u = zu_ref[...].astype(F32)
    ub = u.astype(BF16)
    for kb in range(KB):
        r = _dot(ub[:, kb * LANES:(kb + 1) * LANES], bw_ref[kb])
        for t in range(BPK):
            xr_scr[kb * BPK + t] = r[:, t * LANES:(t + 1) * LANES]
            xi_scr[kb * BPK + t] = r[:, (BPK + t) * LANES:(BPK + t + 1) * LANES]

    def cmul_add(a_r, a_i, x_r, x_i, b_r, b_i):
        return a_r * x_r - a_i * x_i + b_r, a_r * x_i + a_i * x_r + b_i

    def scan8(x_r, x_i, cb):
        cl = slice(cb * LANES, (cb + 1) * LANES)
        for lv in range(3):
            x_r, x_i = cmul_add(ar_ref[lv, :, cl], ai_ref[lv, :, cl], pltpu.roll(x_r, 1 << lv, 0),
                                pltpu.roll(x_i, 1 << lv, 0), x_r, x_i)
        return x_r, x_i

    def bcast(row):
        return jnp.broadcast_to(row, (SUBLANES, LANES))

    def seq(bi):
        def grp(gi, carry):
            rows = _rows(bi * Lc + gi * SUBLANES, SUBLANES)
            for cb in range(NB):
                cl = slice(cb * LANES, (cb + 1) * LANES)
                x_r, x_i = scan8(xr_scr[cb, rows, :], xi_scr[cb, rows, :], cb)
                h_r, h_i = cmul_add(pr_ref[:, cl], pi_ref[:, cl], bcast(hr_ref[bi, :, cl]),
                                    bcast(hi_ref[bi, :, cl]), x_r, x_i)
                xr_scr[cb, rows, :] = h_r
                xi_scr[cb, rows, :] = h_i
                hr_ref[bi, :, cl] = h_r[SUBLANES - 1:, :]
                hi_ref[bi, :, cl] = h_i[SUBLANES - 1:, :]
            return carry
        lax.fori_loop(0, Lc // SUBLANES, grp, 0)

    _for_each_seq(bb, seq)

    ys = []
    for kb in range(KB):
        hrb = jnp.concatenate([xr_scr[kb * BPK + t] for t in range(BPK)], axis=1).astype(BF16)
        hib = jnp.concatenate([xi_scr[kb * BPK + t] for t in range(BPK)], axis=1).astype(BF16)
        ys.append(_dot(hrb, cwr_ref[kb]) + _dot(hib, cwi_ref[kb]))
    y = jnp.concatenate(ys, axis=1) + d_ref[...] * u
    g = jax.nn.gelu(y)
    o_ref[...] = (g * jax.nn.sigmoid(_dot(g.astype(BF16), wglu_ref[...]) + bglu_ref[...])).astype(o_ref.dtype)


def _s5(z, B, T, h0r, h0i, tabs, d, wglu, bglu):
    Lc = min(T, 256)
    nC = T // Lc
    bb = 1 if nC > 1 else min(B, max(1, 128 // T))
    bw, ar, ai, pr, pi, cwr, cwi = tabs
    R = bb * Lc
    W = d.shape[1]
    NS = h0r.shape[-1]
    full = lambda a: pl.BlockSpec(a.shape, lambda i, c: (0,) * a.ndim)
    hspec = pl.BlockSpec((bb, 1, NS), lambda i, c: (i, 0, 0))
    return pl.pallas_call(
        functools.partial(_s5_kernel, bb=bb, Lc=Lc),
        grid=(B // bb, nC),
        in_specs=[pl.BlockSpec((R, W), lambda i, c: (i * nC + c, 7)),
                  full(bw), full(ar), full(ai), full(pr), full(pi), full(cwr), full(cwi),
                  full(d), full(wglu), full(bglu), hspec, hspec],
        out_specs=[pl.BlockSpec((R, W), lambda i, c: (i * nC + c, 0)), hspec, hspec],
        out_shape=[jax.ShapeDtypeStruct((B * T, W), _act_dtype(T)),
                   jax.ShapeDtypeStruct((B, 1, NS), F32),
                   jax.ShapeDtypeStruct((B, 1, NS), F32)],
        scratch_shapes=[pltpu.VMEM((NS // LANES, R, LANES), F32), pltpu.VMEM((NS // LANES, R, LANES), F32)],
        compiler_params=_cparams(("parallel", "arbitrary")),
        name="s5",
    )(z, bw, ar, ai, pr, pi, cwr, cwi, d, wglu, bglu, h0r, h0i)


def _s5_tables(a_re, a_im, log_dt, b_re, b_im, c_re, c_im):
    G, P = a_re.shape
    dt = jnp.exp(log_dt.astype(F32))[:, None]
    lam_re = -jnp.abs(a_re.astype(F32))
    lam_im = a_im.astype(F32)

    def power(n):
        n = jnp.asarray(n, F32)[:, None, None]
        mag = jnp.exp(lam_re * dt * n)
        return ((mag * jnp.cos(lam_im * dt * n)).reshape(-1, G * P),
                (mag * jnp.sin(lam_im * dt * n)).reshape(-1, G * P))

    mag = jnp.exp(lam_re * dt)
    ab_re = mag * jnp.cos(lam_im * dt)
    ab_im = mag * jnp.sin(lam_im * dt)
    den = lam_re * lam_re + lam_im * lam_im
    co_re = ((ab_re - 1.0) * lam_re + ab_im * lam_im) / den
    co_im = (ab_im * lam_re - (ab_re - 1.0) * lam_im) / den
    br, bi = b_re.astype(F32), b_im.astype(F32)
    bb_re = co_re[..., None] * br - co_im[..., None] * bi
    bb_im = co_re[..., None] * bi + co_im[..., None] * br
    gpb = LANES // S5_GROUP
    KB = G // gpb
    eye = jnp.eye(gpb, dtype=F32)
    blk = lambda w: jnp.einsum('kgpc,gh->kgchp', w.reshape(KB, gpb, P, S5_GROUP), eye).reshape(KB, LANES, gpb * P)
    bw = jnp.concatenate([blk(bb_re), blk(bb_im)], axis=-1).astype(BF16)
    cblk = lambda w: jnp.einsum('kgcp,gh->kgphc', w.reshape(KB, gpb, S5_GROUP, P), eye).reshape(KB, gpb * P, LANES)
    cwr = cblk(c_re.astype(F32)).astype(BF16)
    cwi = cblk(-c_im.astype(F32)).astype(BF16)
    rows = jnp.arange(SUBLANES)
    ars, ais = [], []
    for lv in range(3):
        r, i = power([1 << lv])
        keep = (rows >= (1 << lv))[:, None]
        ars.append(jnp.where(keep, r, 0.0))
        ais.append(jnp.where(keep, i, 0.0))
    pr, pi = power(np.arange(1, SUBLANES + 1))
    return bw, jnp.stack(ars), jnp.stack(ais), pr, pi, cwr, cwi


def _branch_kernel(or_ref, om_ref, os_ref, gr_ref, gm_ref, gs_ref, wr_ref, wm_ref, ws_ref, o_ref):
    br = _dot(or_ref[...].astype(BF16), wr_ref[...])
    bm = _dot(om_ref[...].astype(BF16), wm_ref[...])
    bs = _dot(os_ref[...].astype(BF16), ws_ref[...])
    merged = (jax.nn.sigmoid(gr_ref[...].astype(F32)) * br + jax.nn.sigmoid(gm_ref[...].astype(F32)) * bm
              + jax.nn.sigmoid(gs_ref[...].astype(F32)) * bs)
    o_ref[...] = merged.astype(BF16)


def _branch(o_r, o_m, o_s, z, wr, wm, ws, l):
    M, W = o_r.shape
    D = wr.shape[2]
    TM, TN = min(M, 512), 1024
    nJ = D // TN
    ospec = pl.BlockSpec((TM, W), lambda i, j: (i, 0))
    gspec = lambda k: pl.BlockSpec((TM, TN), lambda i, j: (i, 8 + k * nJ + j))
    wspec = pl.BlockSpec((None, W, TN), lambda i, j: (l, 0, j))
    return pl.pallas_call(
        _branch_kernel,
        grid=(M // TM, nJ),
        in_specs=[ospec, ospec, ospec, gspec(0), gspec(1), gspec(2), wspec, wspec, wspec],
        out_specs=pl.BlockSpec((TM, TN), lambda i, j: (i, j)),
        out_shape=jax.ShapeDtypeStruct((M, D), BF16),
        compiler_params=_cparams(("parallel", "parallel")),
        name="branch",
    )(o_r, o_m, o_s, z, z, z, wr, wm, ws)


def _wo_kernel(m_ref, w_ref, x_ref, g_ref, o_ref):
    y = _dot(m_ref[...], w_ref[...])
    o_ref[...] = x_ref[...] + g_ref[...] * y.reshape(o_ref.shape)


def _wo(merged, w_o, x, mod, l, moff):
    B, T, D = x.shape
    bb, tt = _tile_cfg(B, T, 512)
    TM, nT = bb * tt, T // tt
    TN = 1024
    xspec = pl.BlockSpec((bb, tt, TN), lambda i, j: (i // nT, i % nT, j))
    return pl.pallas_call(
        _wo_kernel,
        grid=((B // bb) * nT, D // TN),
        in_specs=[pl.BlockSpec((TM, D), lambda i, j: (i, 0)),
                  pl.BlockSpec((None, D, TN), lambda i, j: (l, 0, j)),
                  xspec,
                  pl.BlockSpec((None, bb, 1, TN), lambda i, j: (l, moff // bb + i // nT, 0, 2 * (D // TN) + j))],
        out_specs=xspec,
        out_shape=jax.ShapeDtypeStruct((B, T, D), F32),
        compiler_params=_cparams(("parallel", "parallel")),
        name="wo",
    )(merged, w_o, x, mod)


def _prep_u_kernel(u_ref, o_ref):
    kq = o_ref.shape[-1]
    for j in range(o_ref.shape[0]):
        o_ref[j] = u_ref[:, j * kq:(j + 1) * kq].astype(BF16)


def _prep_u(peer_u):
    L, NE, D = peer_u.shape
    R = 2 * PEER_MC
    kq = D // PEER_KQ
    return pl.pallas_call(
        _prep_u_kernel,
        grid=(L, NE // R),
        in_specs=[pl.BlockSpec((None, R, D), lambda l, p: (l, p, 0))],
        out_specs=pl.BlockSpec((None, None, PEER_KQ, R, kq), lambda l, p: (l, p, 0, 0, 0)),
        out_shape=jax.ShapeDtypeStruct((L, NE // R, PEER_KQ, R, kq), BF16),
        compiler_params=_cparams(("parallel", "parallel")),
        name="prep_u",
    )(peer_u)


def _prep_v_kernel(v_ref, o_ref):
    kq = o_ref.shape[1]
    for j in range(o_ref.shape[0]):
        o_ref[j] = v_ref[:, j * kq:(j + 1) * kq].T.astype(BF16)


def _prep_v(peer_v):
    L, NE, D = peer_v.shape
    kq = D // PEER_KQ
    return pl.pallas_call(
        _prep_v_kernel,
        grid=(L, NE // PEER_MC),
        in_specs=[pl.BlockSpec((None, PEER_MC, D), lambda l, c: (l, c, 0))],
        out_specs=pl.BlockSpec((None, None, PEER_KQ, kq, PEER_MC), lambda l, c: (l, c, 0, 0, 0)),
        out_shape=jax.ShapeDtypeStruct((L, NE // PEER_MC, PEER_KQ, kq, PEER_MC), BF16),
        compiler_params=_cparams(("parallel", "parallel")),
        name="prep_v",
    )(peer_v)


def _peer_q_kernel(x_ref, nw_ref, sh_ref, sc_ref, wq_ref, k1_ref, k2_ref, h_ref, st_ref):
    h = _norm_mod(x_ref[...], nw_ref[...], sc_ref[...], sh_ref[...])
    hb = h.reshape(-1, h.shape[-1]).astype(BF16)
    kq = h_ref.shape[-1]
    for j in range(h_ref.shape[0]):
        h_ref[j] = hb[:, j * kq:(j + 1) * kq]
    q = _dot(hb, wq_ref[...])
    dq = k1_ref.shape[-1]
    for hd in range(PEER_HEADS):
        q1 = q[:, (2 * hd) * dq:(2 * hd + 1) * dq]
        q2 = q[:, (2 * hd + 1) * dq:(2 * hd + 2) * dq]
        st_ref[2 * hd] = _dot_nt(k1_ref[hd], q1, precision=lax.Precision.HIGHEST)
        st_ref[2 * hd + 1] = _dot_nt(k2_ref[hd], q2, precision=lax.Precision.HIGHEST)


def _peer_q(x, mod, l, moff, nw, wq, k1, k2):
    B, T, D = x.shape
    bb, tt = _tile_cfg(B, T, 256)
    TM, nT = bb * tt, T // tt
    full = lambda a: pl.BlockSpec(a.shape, lambda i: (0,) * a.ndim)
    mmap = lambda k: (lambda i: (l, moff // bb + i // nT, 0, k))
    return pl.pallas_call(
        _peer_q_kernel,
        grid=((B // bb) * nT,),
        in_specs=[pl.BlockSpec((bb, tt, D), lambda i: (i // nT, i % nT, 0)),
                  pl.BlockSpec((1, D), lambda i: (0, 0)),
                  pl.BlockSpec((None, bb, 1, D), mmap(3)),
                  pl.BlockSpec((None, bb, 1, D), mmap(4)),
                  pl.BlockSpec((None,) + wq.shape[1:], lambda i: (l, 0, 0)), full(k1), full(k2)],
        out_specs=[pl.BlockSpec((PEER_KQ, TM, D // PEER_KQ), lambda i: (0, i, 0)),
                   pl.BlockSpec((2 * PEER_HEADS, N_KEYS, TM), lambda i: (0, 0, i))],
        out_shape=[jax.ShapeDtypeStruct((PEER_KQ, B * T, D // PEER_KQ), BF16),
                   jax.ShapeDtypeStruct((2 * PEER_HEADS, N_KEYS, B * T), F32)],
        compiler_params=_cparams(("parallel",)),
        name="peer_q",
    )(x, nw, mod, mod, wq, k1, k2)


def _top_values(cur, n):
    vals = []
    for _ in range(n):
        m = jnp.max(cur, axis=0, keepdims=True)
        vals.append(m)
        cur = jnp.where(cur == m, -jnp.inf, cur)
    return vals


def _peer_sel_kernel(st_ref, e1_ref, e2_ref, th_ref):
    ts = st_ref.shape[-1]
    neg = jnp.full((1, ts), -jnp.inf, F32)
    n = PEER_TOPK + 1
    ths = []
    for hd in range(PEER_HEADS):
        s1 = st_ref[2 * hd]
        s2 = st_ref[2 * hd + 1]
        v1 = _top_values(s1, n)
        v2 = _top_values(s2, n)
        rows = [v1[r] + v2[c] for r in range(n) for c in range(n // (r + 1))]
        rows += [neg] * (-len(rows) % SUBLANES)
        top = _top_values(jnp.concatenate(rows, axis=0), n)
        den = jnp.exp(top[0] - top[0])
        for t in top[1:PEER_TOPK]:
            den = den + jnp.exp(t - top[0])
        e1_ref[hd] = jnp.exp(s1 - v1[0]) / den
        e2_ref[hd] = jnp.exp(s2 - v2[0])
        ths.append(jnp.exp(0.5 * (top[PEER_TOPK - 1] + top[PEER_TOPK]) - top[0]) / den)
    th_ref[...] = jnp.concatenate(ths, axis=0)


def _peer_sel(st):
    H2, K, M = st.shape
    TS = min(M, 256)
    hspec = pl.BlockSpec((PEER_HEADS, K, TS), lambda i: (0, 0, i))
    return pl.pallas_call(
        _peer_sel_kernel,
        grid=(M // TS,),
        in_specs=[pl.BlockSpec((H2, K, TS), lambda i: (0, 0, i))],
        out_specs=[hspec, hspec, pl.BlockSpec((PEER_HEADS, TS), lambda i: (0, i))],
        out_shape=[jax.ShapeDtypeStruct((PEER_HEADS, K, M), F32),
                   jax.ShapeDtypeStruct((PEER_HEADS, K, M), F32),
                   jax.ShapeDtypeStruct((PEER_HEADS, M), F32)],
        compiler_params=_cparams(("parallel",)),
        name="peer_sel",
    )(st)


def _peer_exp_kernel(h_ref, u_ref, vta_ref, vtb_ref, e1_ref, e2_ref, th_ref, x_ref, g_ref, o_ref,
                     acc_scr, sc0, sc1, zt0, zt1, *, mc, nc):
    s = pl.program_id(1)
    slabs = mc // N_KEYS

    @pl.when(s == 0)
    def _():
        acc_scr[...] = jnp.zeros(acc_scr.shape, F32)
        sc1[...] = jnp.zeros(sc1.shape, F32)
        zt0[...] = jnp.zeros(zt0.shape, BF16)
        zt1[...] = jnp.zeros(zt1.shape, BF16)

    nq = u_ref.shape[0]
    spp = slabs // nq
    assert spp * nq == slabs

    def substep(half, sc_a, sc_b, zt_b, zt_c, chunk_b):
        c = jnp.clip(chunk_b, 0, nc - 1)
        cols = slice(half * mc, (half + 1) * mc)
        sc_a[...] = jnp.zeros(sc_a.shape, F32)

        def piece(j, carry):
            acc_scr[j] += _dot((vta_ref, vtb_ref)[half][j], zt_c[...])
            for sl in range(spp):
                rs = pl.ds(pl.multiple_of((j * spp + sl) * N_KEYS, N_KEYS), N_KEYS)
                i1 = c * slabs + j * spp + sl
                for tc in range(sc_b.shape[1] // LANES):
                    tl = slice(tc * LANES, (tc + 1) * LANES)
                    w = jnp.zeros((N_KEYS, LANES), F32)
                    for hd in range(PEER_HEADS):
                        p = e1_ref[hd, pl.ds(i1, 1), :][:, tl] * e2_ref[hd, :, tl]
                        w = jnp.where(p >= th_ref[pl.ds(hd, 1), tl], w + p, w)
                    zt_b[rs, tl] = (jax.nn.gelu(sc_b[rs, tl]) * w).astype(BF16)
            sc_a[...] += _dot_nt(u_ref[j, cols, :], h_ref[j])
            return carry

        lax.fori_loop(0, nq, piece, 0)

    substep(0, sc0, sc1, zt1, zt0, 2 * s - 1)
    substep(1, sc1, sc0, zt0, zt1, 2 * s)

    @pl.when(s == pl.num_programs(1) - 1)
    def _():
        o_ref[...] = x_ref[...] + g_ref[...] * acc_scr[...].reshape(-1, acc_scr.shape[-1]).T.reshape(o_ref.shape)


def _peer_exp(hb, u, vt, e1, e2, th, x, mod, l, moff):
    B, T, D = x.shape
    bb, tt = _tile_cfg(B, T, PEER_TB)
    TB, nT = bb * tt, T // tt
    MC = PEER_MC
    NP = u.shape[1]
    NC = vt.shape[1]
    assert NC == 2 * NP and u.shape[3] == 2 * MC and vt.shape[4] == MC
    xspec = pl.BlockSpec((bb, tt, D), lambda i, s: (i // nT, i % nT, 0))
    hspec = pl.BlockSpec((PEER_HEADS, N_KEYS, TB), lambda i, s: (0, 0, i))
    return pl.pallas_call(
        functools.partial(_peer_exp_kernel, mc=MC, nc=NC),
        grid=((B // bb) * nT, NP + 1),
        in_specs=[pl.BlockSpec((PEER_KQ, TB, D // PEER_KQ), lambda i, s: (0, i, 0)),
                  pl.BlockSpec((None, None, PEER_KQ, 2 * MC, D // PEER_KQ),
                               lambda i, s: (l, jnp.minimum(s, NP - 1), 0, 0, 0)),
                  pl.BlockSpec((None, None, PEER_KQ, D // PEER_KQ, MC),
                               lambda i, s: (l, 2 * jnp.maximum(s - 1, 0), 0, 0, 0)),
                  pl.BlockSpec((None, None, PEER_KQ, D // PEER_KQ, MC),
                               lambda i, s: (l, 2 * jnp.maximum(s - 1, 0) + 1, 0, 0, 0)),
                  hspec, hspec,
                  pl.BlockSpec((PEER_HEADS, TB), lambda i, s: (0, i)),
                  xspec,
                  pl.BlockSpec((None, bb, 1, D), lambda i, s: (l, moff // bb + i // nT, 0, 5))],
        out_specs=xspec,
        out_shape=jax.ShapeDtypeStruct((B, T, D), F32),
        scratch_shapes=[pltpu.VMEM((PEER_KQ, D // PEER_KQ, TB), F32),
                        pltpu.VMEM((MC, TB), F32), pltpu.VMEM((MC, TB), F32),
                        pltpu.VMEM((MC, TB), BF16), pltpu.VMEM((MC, TB), BF16)],
        compiler_params=_cparams(("parallel", "arbitrary")),
        name="peer_exp",
    )(hb, u, vt, vt, e1, e2, th, x, mod)


def _final_kernel(x_ref, w_ref, o_ref):
    x = x_ref[...]
    ms = jnp.mean(x * x, axis=-1, keepdims=True)
    o_ref[...] = x * lax.rsqrt(ms + EPS) * w_ref[...]


def _final_norm(x, w):
    B, T, D = x.shape
    bb, tt = _tile_cfg(B, T, 512)
    nT = T // tt
    xspec = pl.BlockSpec((bb, tt, D), lambda i: (i // nT, i % nT, 0))
    return pl.pallas_call(
        _final_kernel,
        grid=((B // bb) * nT,),
        in_specs=[xspec, pl.BlockSpec((1, D), lambda i: (0, 0))],
        out_specs=xspec,
        out_shape=jax.ShapeDtypeStruct((B, T, D), F32),
        compiler_params=_cparams(("parallel",)),
        name="final_norm",
    )(x, w)


def _trunk(x, mod, moff, pos0, big_states, states, lw, final_w):
    B, T, D = x.shape
    depth = len(lw)
    rtabs = _ret_tables(T, pos0)
    ret_in, c_in = big_states
    n_ret = n_c = None
    new = []
    for l, (st, w) in enumerate(zip(states, lw)):
        s_n, s_m, s_conv, s_hr, s_hi = st
        z, zg = _in_proj(x, mod, l, moff, w['n1'], w['w_a'], w['w_b'], w['w_gate'])
        o_r, n_ret = _retention(z, B, T, ret_in, min(l, ret_in.shape[0] - 1), l, depth, n_ret, rtabs, w['ret_gn'])
        o_m, n_c, n_n, n_m, n_conv = _mlstm(z, zg, B, T, c_in, min(l, c_in.shape[0] - 1), l, depth, n_c,
                                            (s_n, s_m, s_conv), w['conv_w'], w['conv_b'],
                                            w['wqk'], w['gate_b'], w['m_gn'], w['m_skip'])
        o_s, n_hr, n_hi = _s5(z, B, T, s_hr, s_hi, w['s5_tabs'], w['s5_d'], w['s5_wglu'], w['s5_bglu'])
        merged = _branch(o_r, o_m, o_s, z, w['w_ret_out'], w['w_mlstm_out'], w['w_s5_out'], l)
        x = _wo(merged, w['w_o'], x, mod, l, moff)
        hb, sc = _peer_q(x, mod, l, moff, w['n2'], w['peer_wq'], w['peer_k1'], w['peer_k2'])
        e1, e2, th = _peer_sel(sc)
        x = _peer_exp(hb, w['peer_u'], w['peer_vt'], e1, e2, th, x, mod, l, moff)
        new.append((n_n, n_m[..., 0], n_conv, n_hr, n_hi))
    y = _final_norm(x, final_w)
    return y, [n_ret, n_c] + [jnp.stack([s[i] for s in new]) for i in range(5)]


def kernel(x_prompt, x_sample, state_ret, state_mlstm_c, state_mlstm_n, state_mlstm_m, state_mlstm_conv,
           state_s5_re, state_s5_im, c_prompt, c_sample, ada_w, ada_b, norm1_w, norm2_w, final_norm_w,
           w_in, ret_gn_w, w_ret_out, mlstm_conv_w, mlstm_conv_b, mlstm_wq, mlstm_wk, mlstm_b_i, mlstm_b_f,
           mlstm_gn_w, mlstm_skip, w_mlstm_out, s5_a_re, s5_a_im, s5_log_dt, s5_b_re, s5_b_im, s5_c_re,
           s5_c_im, s5_d, s5_w_glu, s5_b_glu, w_s5_out, w_o, peer_wq, peer_k1, peer_k2, peer_u, peer_v):
    depth = w_in.shape[0]
    Bp, Tp, D = x_prompt.shape
    Bs, Ts, _ = x_sample.shape
    W = HEADS * DH
    G, P = s5_a_re.shape[1:]
    NS = G * P
    past_len = 16384

    pad = -(Bs + Bp) % SUBLANES
    c_all = jnp.concatenate([c_sample, c_prompt, jnp.zeros((pad, D), F32)], axis=0)
    mod = _ada(c_all, ada_w, ada_b)
    mod = mod.reshape(depth, c_all.shape[0], 1, ada_w.shape[2])

    a_end = 7 * W
    g_end = a_end + 2 * HEADS
    w_ret_out_b, w_mlstm_out_b, w_s5_out_b = w_ret_out.astype(BF16), w_mlstm_out.astype(BF16), w_s5_out.astype(BF16)
    w_o_b, peer_wq_b = w_o.astype(BF16), peer_wq.astype(BF16)
    u_tiles = _prep_u(peer_u)
    vt_tiles = _prep_v(peer_v)
    lw = []
    for l in range(depth):
        w_gate = jnp.pad(w_in[l, :, a_end:g_end], ((0, 0), (0, LANES - 2 * HEADS))).astype(BF16)
        gate_b = jnp.pad(jnp.concatenate([mlstm_b_i[l], mlstm_b_f[l]]), (0, LANES - 2 * HEADS)).reshape(1, LANES)
        lw.append(dict(
            n1=norm1_w[l].reshape(1, D), n2=norm2_w[l].reshape(1, D),
            w_a=w_in[l, :, :a_end].astype(BF16), w_b=w_in[l, :, g_end:].astype(BF16), w_gate=w_gate, gate_b=gate_b,
            ret_gn=ret_gn_w[l].reshape(1, W),
            conv_w=mlstm_conv_w[l], conv_b=mlstm_conv_b[l].reshape(1, W),
            wqk=jnp.concatenate([mlstm_wq[l], mlstm_wk[l]], axis=-1).astype(BF16),
            m_gn=mlstm_gn_w[l].reshape(1, W), m_skip=mlstm_skip[l].reshape(1, W),
            s5_tabs=_s5_tables(s5_a_re[l], s5_a_im[l], s5_log_dt[l], s5_b_re[l], s5_b_im[l], s5_c_re[l], s5_c_im[l]),
            s5_d=s5_d[l].reshape(1, -1), s5_wglu=s5_w_glu[l].astype(BF16), s5_bglu=s5_b_glu[l].reshape(1, -1),
            w_ret_out=w_ret_out_b, w_mlstm_out=w_mlstm_out_b, w_s5_out=w_s5_out_b, w_o=w_o_b,
            peer_wq=peer_wq_b, peer_k1=peer_k1[l], peer_k2=peer_k2[l],
            peer_u=u_tiles, peer_vt=vt_tiles))

    def zero_states(B):
        return (jnp.zeros((B, HEADS, DH), F32), jnp.zeros((B, HEADS, LANES), F32),
                jnp.zeros((B, CONV_W - 1, W), F32), jnp.zeros((B, 1, NS), F32), jnp.zeros((B, 1, NS), F32))

    zero_big = jnp.zeros((1, Bp, HEADS, DH, DH), F32)
    prompt_states = [zero_states(Bp) for _ in range(depth)]
    sample_states = [(state_mlstm_n[l],
                      jnp.broadcast_to(state_mlstm_m[l][..., None], (Bs, HEADS, LANES)),
                      state_mlstm_conv[l], state_s5_re[l].reshape(Bs, 1, NS), state_s5_im[l].reshape(Bs, 1, NS))
                     for l in range(depth)]

    y_p, ps = _trunk(x_prompt, mod, Bs, 0, (zero_big, zero_big), prompt_states, lw, final_norm_w.reshape(1, D))
    y_s, ss = _trunk(x_sample, mod, 0, past_len, (state_ret, state_mlstm_c), sample_states, lw,
                     final_norm_w.reshape(1, D))

    def unpack(st, B):
        r, c, n, m, conv, hr, hi = st
        return (r, c, n, m, conv, hr.reshape(depth, B, G, P), hi.reshape(depth, B, G, P))

    return (y_p, y_s) + unpack(ps, Bp) + unpack(ss, Bs)
```

```python
import functools
import math

import jax
import jax.numpy as jnp
import numpy as np
from jax import lax
from jax.experimental import pallas as pl
from jax.experimental.pallas import tpu as pltpu

F32 = jnp.float32
BF16 = jnp.bfloat16

EPS = 1e-6
ROPE_BASE = 10000.0
HEADS = 8
DH = 128
CONV_W = 4
S5_GROUP = 16
S5_STATE = 64
PEER_HEADS = 8
PEER_TOPK = 16
N_KEYS = 128
CHUNK = 128
LANES = 128
SUBLANES = 8
VMEM_LIMIT = 56 * 1024 * 1024
RET_GROUP = 8
MLSTM_GROUP = 1
MLSTM_HEADS_PER_PASS = 4
PEER_TB = 512
PEER_MC = 512
PEER_KQ = 4


def _cparams(sem):
    return pltpu.CompilerParams(dimension_semantics=sem, vmem_limit_bytes=VMEM_LIMIT)


def _tile_cfg(B, T, target):
    if T >= target:
        bb, tt = 1, target
    else:
        bb, tt = min(B, target // T), T
    assert T % tt == 0 and B % bb == 0
    return bb, tt


def _act_dtype(T):
    return BF16 if T % CHUNK == 0 else F32


def _silu(x):
    return x * jax.nn.sigmoid(x)


def _norm_mod(x, w, sc, sh):
    ms = jnp.mean(x * x, axis=-1, keepdims=True)
    y = x * lax.rsqrt(ms + EPS) * w
    return y * (1.0 + sc) + sh


def _head_norm(x):
    mu = jnp.mean(x, axis=-1, keepdims=True)
    xc = x - mu
    var = jnp.mean(xc * xc, axis=-1, keepdims=True)
    return xc * lax.rsqrt(var + EPS)


def _dot(a, b):
    return jnp.dot(a, b, preferred_element_type=F32)


def _dot_nt(a, b, **kw):
    return lax.dot_general(a, b, (((1,), (1,)), ((), ())), preferred_element_type=F32, **kw)


def _dot_nt_split(a, b):
    ah, bh = a.astype(BF16), b.astype(BF16)
    al = (a - ah.astype(F32)).astype(BF16)
    bl = (b - bh.astype(F32)).astype(BF16)
    return _dot_nt(ah, bh) + (_dot_nt(ah, bl) + _dot_nt(al, bh))


def _dot_tn(a, b):
    return lax.dot_general(a, b, (((0,), (0,)), ((), ())), preferred_element_type=F32)


def _rows(start, n):
    if isinstance(start, int):
        return pl.ds(start, n)
    return pl.ds(pl.multiple_of(start, SUBLANES), n)


def _for_each_group(bb, gs, body):
    gs = min(gs, bb)
    assert bb % gs == 0
    if bb == gs:
        body(list(range(bb)))
    else:
        def f(gi, c):
            body([gi * gs + i for i in range(gs)])
            return c
        lax.fori_loop(0, bb // gs, f, 0)


def _for_each_seq(bb, body, unroll=1):
    if bb == 1:
        body(0)
    else:
        def f(bi, c):
            body(bi)
            return c
        lax.fori_loop(0, bb, f, 0, unroll=unroll)


def _ada_kernel(c_ref, w_ref, b_ref, o_ref):
    a = _silu(c_ref[...]).astype(BF16)
    o_ref[...] = _dot(a, w_ref[...].astype(BF16)) + b_ref[...]


def _ada(c_all, ada_w, ada_b):
    L, D, N = ada_w.shape
    Bc = c_all.shape[0]
    TN = 1024
    return pl.pallas_call(
        _ada_kernel,
        grid=(L, N // TN),
        in_specs=[pl.BlockSpec((Bc, D), lambda l, j: (0, 0)),
                  pl.BlockSpec((None, D, TN), lambda l, j: (l, 0, j)),
                  pl.BlockSpec((None, 1, TN), lambda l, j: (l, 0, j))],
        out_specs=pl.BlockSpec((None, Bc, TN), lambda l, j: (l, 0, j)),
        out_shape=jax.ShapeDtypeStruct((L, Bc, N), F32),
        compiler_params=_cparams(("parallel", "parallel")),
        name="ada",
    )(c_all, ada_w, ada_b.reshape(L, 1, N))


def _in_kernel(x_ref, nw_ref, sh_ref, sc_ref, wa_ref, wb_ref, wg_ref, z_ref, zg_ref, h_scr, *, na):
    j = pl.program_id(1)

    @pl.when(j == 0)
    def _():
        h = _norm_mod(x_ref[...], nw_ref[...], sc_ref[...], sh_ref[...])
        hb = h.reshape(h_scr.shape).astype(BF16)
        h_scr[...] = hb
        zg_ref[...] = _dot(hb, wg_ref[...])

    @pl.when(j < na)
    def _():
        z_ref[...] = _dot(h_scr[...], wa_ref[...]).astype(z_ref.dtype)

    @pl.when(j >= na)
    def _():
        z_ref[...] = _dot(h_scr[...], wb_ref[...]).astype(z_ref.dtype)


def _in_proj(x, mod, l, moff, nw, wa, wb, wg):
    B, T, D = x.shape
    bb, tt = _tile_cfg(B, T, 1024)
    TM, nT = bb * tt, T // tt
    nI = (B // bb) * nT
    TN = 1024
    na = wa.shape[1] // TN
    N = wa.shape[1] + wb.shape[1]
    xmap = lambda i, j: (i // nT, i % nT, 0)
    mmap = lambda k: (lambda i, j: (l, moff // bb + i // nT, 0, k))
    return pl.pallas_call(
        functools.partial(_in_kernel, na=na),
        grid=(nI, N // TN),
        in_specs=[pl.BlockSpec((bb, tt, D), xmap),
                  pl.BlockSpec((1, D), lambda i, j: (0, 0)),
                  pl.BlockSpec((None, bb, 1, D), mmap(0)),
                  pl.BlockSpec((None, bb, 1, D), mmap(1)),
                  pl.BlockSpec((D, TN), lambda i, j: (0, jnp.minimum(j, na - 1))),
                  pl.BlockSpec((D, TN), lambda i, j: (0, jnp.maximum(j - na, 0))),
                  pl.BlockSpec((D, LANES), lambda i, j: (0, 0))],
        out_specs=[pl.BlockSpec((TM, TN), lambda i, j: (i, j)),
                   pl.BlockSpec((TM, LANES), lambda i, j: (i, 0))],
        out_shape=[jax.ShapeDtypeStruct((B * T, N), _act_dtype(T)),
                   jax.ShapeDtypeStruct((B * T, LANES), F32)],
        scratch_shapes=[pltpu.VMEM((TM, D), BF16)],
        compiler_params=_cparams(("parallel", "arbitrary")),
        name="in_proj",
    )(x, nw, mod, mod, wa, wb, wg)


def _ret_kernel(*refs, bb, L):
    (zq_ref, zk_ref, zv_ref, zg_ref, cq_ref, sq_ref, ck_ref, sk_ref, intra_ref, cross_ref,
     kdec_ref, cdec_ref, gn_ref, s0_ref) = refs[:14]
    o_ref, s_ref = refs[-2:]

    @pl.when(pl.program_id(1) == 0)
    def _():
        s_ref[...] = s0_ref[...]

    cq, sq, ck, sk = cq_ref[...], sq_ref[...], ck_ref[...], sk_ref[...]

    def group(bis):
        chains = [(i, h) for i in range(len(bis)) for h in range(HEADS)]
        rows = [_rows(bi * L, L) for bi in bis]
        cols = [slice(h * DH, (h + 1) * DH) for h in range(HEADS)]
        q, k, v, s = {}, {}, {}, {}
        for c in chains:
            i, h = c
            qf = zq_ref[rows[i], cols[h]].astype(F32)
            kf = zk_ref[rows[i], cols[h]].astype(F32)
            q[c] = (qf * cq + pltpu.roll(qf, DH // 2, 1) * sq).astype(BF16)
            k[c] = kf * ck + pltpu.roll(kf, DH // 2, 1) * sk
            v[c] = zv_ref[rows[i], cols[h]].astype(BF16)
            s[c] = s_ref[bis[i], h]
        att = {c: _dot_nt(q[c], k[c].astype(BF16)) for c in chains}
        inter = {c: _dot(q[c], s[c].astype(BF16)) for c in chains}
        upd = {c: _dot_tn((k[c] * kdec_ref[c[1]]).astype(BF16), v[c]) for c in chains}
        o = {c: _dot((att[c] * intra_ref[c[1]]).astype(BF16), v[c]) + inter[c] * cross_ref[c[1]] for c in chains}
        for c in chains:
            i, h = c
            s_ref[bis[i], h] = cdec_ref[h] * s[c] + upd[c]
            g = zg_ref[rows[i], cols[h]].astype(F32)
            o_ref[rows[i], cols[h]] = (_head_norm(o[c]) * gn_ref[:, cols[h]] * _silu(g)).astype(o_ref.dtype)

    _for_each_group(bb, RET_GROUP, group)


def _layer_state_io(s_in, lin, l, depth, prev, bb, n_inputs, out_index):
    tail = s_in.shape[2:]
    zeros = (0,) * len(tail)
    in_spec = pl.BlockSpec((None, bb) + tail, lambda i, c: (lin, i) + zeros)
    out_spec = pl.BlockSpec((None, bb) + tail, lambda i, c: (l, i) + zeros)
    out_shape = jax.ShapeDtypeStruct((depth,) + s_in.shape[1:], s_in.dtype)
    extra_specs, extra_args, aliases = [], [], {}
    if prev is not None:
        extra_specs = [pl.BlockSpec(memory_space=pl.ANY)]
        extra_args = [prev]
        aliases = {n_inputs: out_index}
    return in_spec, out_spec, out_shape, extra_specs, extra_args, aliases


def _retention(z, B, T, s_in, lin, l, depth, s_prev, tabs, gn_w):
    L = CHUNK if T % CHUNK == 0 else T
    nC = T // L
    bb = 1 if nC > 1 else min(B, 8)
    R = bb * L
    W = HEADS * DH
    cq, sq, ck, sk, intra, cross, kdec, cdec = tabs
    zspec = lambda k: pl.BlockSpec((R, W), lambda i, c: (i * nC + c, k))
    tspec = pl.BlockSpec((L, DH), lambda i, c: (c, 0))
    full = lambda a: pl.BlockSpec(a.shape, lambda i, c: (0,) * a.ndim)
    s_ispec, s_ospec, s_shape, xspecs, xargs, aliases = _layer_state_io(s_in, lin, l, depth, s_prev, bb, 14, 1)
    return pl.pallas_call(
        functools.partial(_ret_kernel, bb=bb, L=L),
        grid=(B // bb, nC),
        in_specs=[zspec(0), zspec(1), zspec(2), zspec(3), tspec, tspec, tspec, tspec,
                  full(intra), full(cross), full(kdec), full(cdec), full(gn_w), s_ispec] + xspecs,
        out_specs=[pl.BlockSpec((R, W), lambda i, c: (i * nC + c, 0)), s_ospec],
        out_shape=[jax.ShapeDtypeStruct((B * T, W), _act_dtype(T)), s_shape],
        input_output_aliases=aliases,
        compiler_params=_cparams(("parallel", "arbitrary")),
        name="ret",
    )(z, z, z, z, cq, sq, ck, sk, intra, cross, kdec, cdec, gn_w, s_in, *xargs)


def _ret_tables(T, pos0):
    L = CHUNK if T % CHUNK == 0 else T
    half = DH // 2
    inv = jnp.exp(-math.log(ROPE_BASE) * jnp.arange(half, dtype=F32) / half)
    pos = jnp.arange(T, dtype=F32) + pos0
    ang = pos[:, None] * inv[None, :]
    cos, sin = jnp.cos(ang), jnp.sin(ang)
    c = jnp.concatenate([cos, cos], axis=-1)
    s = jnp.concatenate([-sin, sin], axis=-1)
    kscale = DH ** -0.5
    lg = jnp.log1p(-jnp.exp2(-5.0 - jnp.arange(HEADS, dtype=F32)))
    j = jnp.arange(L, dtype=F32)
    diff = j[:, None] - j[None, :]
    intra = jnp.where(diff >= 0, jnp.exp(lg[:, None, None] * jnp.maximum(diff, 0.0)), 0.0)
    cross = jnp.exp(lg[:, None] * (j + 1.0))
    kdec = jnp.exp(lg[:, None] * (L - 1.0 - j))
    cdec = jnp.exp(lg * L)
    bl = lambda a: jnp.broadcast_to(a[..., None], a.shape + (DH,))
    return (c, s, c * kscale, s * kscale, intra, bl(cross), bl(kdec), bl(cdec[:, None]))


def _mlstm_kernel(*refs, bb, L, last, hp):
    (zu_ref, zv_ref, zo_ref, zg_ref, cw_ref, cb_ref, wqk_ref, gb_ref, gn_ref, sk_ref,
     c0_ref, n0_ref, m0_ref, conv0_ref) = refs[:14]
    o_ref, c_ref, n_ref, m_ref, conv_ref, xp_scr, tail_scr, ca_scr = refs[-8:]
    ci = pl.program_id(1)

    @pl.when(ci == 0)
    def _():
        c_ref[...] = c0_ref[...]
        n_ref[...] = n0_ref[...]
        m_ref[...] = m0_ref[...]
        tail_scr[...] = jnp.zeros(tail_scr.shape, F32)
        tail_scr[:, SUBLANES - (CONV_W - 1):, :] = conv0_ref[...]

    ri = lax.broadcasted_iota(jnp.int32, (L, L), 0)
    rj = lax.broadcasted_iota(jnp.int32, (L, L), 1)
    causal = rj <= ri
    tri = causal.astype(F32)
    lane = lax.broadcasted_iota(jnp.int32, (L, LANES), 1)
    kscale = DH ** -0.5

    def group(bis):
        ns = len(bis)
        rows = [_rows(bi * L, L) for bi in bis]
        cols = [slice(h * DH, (h + 1) * DH) for h in range(HEADS)]
        gates, csum, gates_t, csum_t, m_all, n_all = [], [], [], [], [], []
        for i, bi in enumerate(bis):
            xp_scr[bi, 0:SUBLANES, :] = tail_scr[bi]
            xp_scr[bi, SUBLANES:, :] = zu_ref[rows[i], :].astype(F32)
            cu = cb_ref[...]
            for t in range(CONV_W):
                cu = cu + xp_scr[bi, pl.ds(SUBLANES - (CONV_W - 1) + t, L), :] * cw_ref[pl.ds(t, 1), :]
            tail_scr[bi] = xp_scr[bi, L:L + SUBLANES, :]
            ca_scr[bi] = _silu(cu)
            gz = zg_ref[rows[i], :] + gb_ref[...]
            g = jnp.where(lane < HEADS, gz, jax.nn.log_sigmoid(gz))
            cs = jnp.dot(tri, g, preferred_element_type=F32, precision=lax.Precision.HIGHEST)
            gates.append(g)
            csum.append(cs)
            gates_t.append(g.T)
            csum_t.append(cs.T)
            m_all.append(m_ref[bi])
            n_all.append(n_ref[bi])
        n_rows = [[] for _ in range(ns)]
        m_rows = [[] for _ in range(ns)]

        def run(chains):
            ca = {c: ca_scr[bis[c[0]], :, cols[c[1]]] for c in chains}
            qk = {c: _dot(ca[c].astype(BF16), wqk_ref[c[1]]) for c in chains}
            q = {c: qk[c][:, :DH] for c in chains}
            k = {c: qk[c][:, DH:] * kscale for c in chains}
            qb = {c: q[c].astype(BF16) for c in chains}
            v = {c: zv_ref[rows[c[0]], cols[c[1]]].astype(BF16) for c in chains}
            c_prev = {c: c_ref[bis[c[0]], c[1]] for c in chains}
            sraw = {c: _dot_nt(qb[c], k[c].astype(BF16)) for c in chains}
            qc = {c: _dot(qb[c], c_prev[c].astype(BF16)) for c in chains}
            s, mt, w_int, m_prev, n_prev, b_col, i_col = {}, {}, {}, {}, {}, {}, {}
            for c in chains:
                i, h = c
                i_col[c] = gates[i][:, h:h + 1]
                b_col[c] = csum[i][:, HEADS + h:HEADS + h + 1]
                i_row = gates_t[i][h:h + 1, :]
                b_row = csum_t[i][HEADS + h:HEADS + h + 1, :]
                m_prev[c] = m_all[i][h:h + 1, :1]
                n_prev[c] = n_all[i][h:h + 1, :]
                dlog = jnp.where(causal, b_col[c] - b_row + i_row, -jnp.inf)
                inter = b_col[c] + m_prev[c]
                mt[c] = jnp.maximum(inter, jnp.max(dlog, axis=-1, keepdims=True))
                s[c] = sraw[c] * jnp.exp(dlog - mt[c])
                w_int[c] = jnp.exp(inter - mt[c])
            sv = {c: _dot(s[c].astype(BF16), v[c]) for c in chains}
            kt, dec, m_new = {}, {}, {}
            for c in chains:
                m_new[c] = mt[c][L - 1:L, :]
                b_last = b_col[c][L - 1:L, :]
                tail = jnp.exp(b_last - b_col[c] + i_col[c] - m_new[c])
                dec[c] = jnp.exp(b_last + m_prev[c] - m_new[c])
                kt[c] = k[c] * tail
            upd = {c: _dot_tn(kt[c].astype(BF16), v[c]) for c in chains}
            for c in chains:
                i, h = c
                num = sv[c] + w_int[c] * qc[c]
                den = (jnp.sum(s[c], axis=-1, keepdims=True)
                       + w_int[c] * jnp.sum(q[c] * n_prev[c], axis=-1, keepdims=True))
                hh = num / jnp.maximum(jnp.abs(den), jnp.exp(-mt[c]))
                c_ref[bis[i], h] = dec[c] * c_prev[c] + upd[c]
                n_rows[i].append(dec[c] * n_prev[c] + jnp.sum(kt[c], axis=0, keepdims=True))
                m_rows[i].append(jnp.broadcast_to(m_new[c], (1, LANES)))
                hm = jax.nn.sigmoid(zo_ref[rows[i], cols[h]].astype(F32)) * hh
                o_ref[rows[i], cols[h]] = (_head_norm(hm) * gn_ref[:, cols[h]]
                                           + sk_ref[:, cols[h]] * ca[c]).astype(o_ref.dtype)

        for h0 in range(0, HEADS, hp):
            run([(i, h) for i in range(ns) for h in range(h0, h0 + hp)])
        for i, bi in enumerate(bis):
            n_ref[bi] = jnp.concatenate(n_rows[i], axis=0)
            m_ref[bi] = jnp.concatenate(m_rows[i], axis=0)

    _for_each_group(bb, MLSTM_GROUP, group)

    @pl.when(ci == last)
    def _():
        conv_ref[...] = tail_scr[:, SUBLANES - (CONV_W - 1):, :]


def _mlstm(z, zg, B, T, c_in, lin, l, depth, c_prev, states, cw, cb, wqk, gb, gn_w, skip):
    L = CHUNK if T % CHUNK == 0 else T
    nC = T // L
    bb = 1 if nC > 1 else min(B, 8)
    R = bb * L
    W = HEADS * DH
    n0, m0, conv0 = states
    zspec = lambda k: pl.BlockSpec((R, W), lambda i, c: (i * nC + c, k))
    full = lambda a: pl.BlockSpec(a.shape, lambda i, c: (0,) * a.ndim)
    nspec = pl.BlockSpec((bb, HEADS, DH), lambda i, c: (i, 0, 0))
    vspec = pl.BlockSpec((bb, CONV_W - 1, W), lambda i, c: (i, 0, 0))
    c_ispec, c_ospec, c_shape, xspecs, xargs, aliases = _layer_state_io(c_in, lin, l, depth, c_prev, bb, 14, 1)
    return pl.pallas_call(
        functools.partial(_mlstm_kernel, bb=bb, L=L, last=nC - 1,
                          hp=MLSTM_HEADS_PER_PASS if nC > 1 else HEADS),
        grid=(B // bb, nC),
        in_specs=[zspec(4), zspec(5), zspec(6), pl.BlockSpec((R, LANES), lambda i, c: (i * nC + c, 0)),
                  full(cw), full(cb), full(wqk), full(gb), full(gn_w), full(skip),
                  c_ispec, nspec, nspec, vspec] + xspecs,
        out_specs=[pl.BlockSpec((R, W), lambda i, c: (i * nC + c, 0)), c_ospec, nspec, nspec, vspec],
        input_output_aliases=aliases,
        out_shape=[jax.ShapeDtypeStruct((B * T, W), _act_dtype(T)),
                   c_shape,
                   jax.ShapeDtypeStruct((B, HEADS, DH), F32),
                   jax.ShapeDtypeStruct((B, HEADS, LANES), F32),
                   jax.ShapeDtypeStruct((B, CONV_W - 1, W), F32)],
        scratch_shapes=[pltpu.VMEM((bb, L + SUBLANES, W), F32),
                        pltpu.VMEM((bb, SUBLANES, W), F32),
                        pltpu.VMEM((bb, L, W), F32)],
        compiler_params=_cparams(("parallel", "arbitrary")),
        name="mlstm",
    )(z, z, z, zg, cw, cb, wqk, gb, gn_w, skip, c_in, n0, m0, conv0, *xargs)


def _s5_kernel(zu_ref, bw_ref, ar_ref, ai_ref, pr_ref, pi_ref, cwr_ref, cwi_ref, d_ref, wglu_ref, bglu_ref,
               h0r_ref, h0i_ref, o_ref, hr_ref, hi_ref, xr_scr, xi_scr, *, bb, Lc):
    @pl.when(pl.program_id(1) == 0)
    def _():
        hr_ref[...] = h0r_ref[...]
        hi_ref[...] = h0i_ref[...]

    NB = xr_scr.shape[0]
    KB = bw_ref.shape[0]
    BPK = NB // KB
    u = zu_ref[...].astype(F32)
    ub = u.astype(BF16)
    for kb in range(KB):
        r = _dot(ub[:, kb * LANES:(kb + 1) * LANES], bw_ref[kb])
        for t in range(BPK):
            xr_scr[kb * BPK + t] = r[:, t * LANES:(t + 1) * LANES]
            xi_scr[kb * BPK + t] = r[:, (BPK + t) * LANES:(BPK + t + 1) * LANES]

    def cmul_add(a_r, a_i, x_r, x_i, b_r, b_i):
        return a_r * x_r - a_i * x_i + b_r, a_r * x_i + a_i * x_r + b_i

    def scan8(x_r, x_i, cb):
        cl = slice(cb * LANES, (cb + 1) * LANES)
        for lv in range(3):
            x_r, x_i = cmul_add(ar_ref[lv, :, cl], ai_ref[lv, :, cl], pltpu.roll(x_r, 1 << lv, 0),
                                pltpu.roll(x_i, 1 << lv, 0), x_r, x_i)
        return x_r, x_i

    def bcast(row):
        return jnp.broadcast_to(row, (SUBLANES, LANES))

    def seq(bi):
        def grp(gi, carry):
            rows = _rows(bi * Lc + gi * SUBLANES, SUBLANES)
            for cb in range(NB):
                cl = slice(cb * LANES, (cb + 1) * LANES)
                x_r, x_i = scan8(xr_scr[cb, rows, :], xi_scr[cb, rows, :], cb)
                h_r, h_i = cmul_add(pr_ref[:, cl], pi_ref[:, cl], bcast(hr_ref[bi, :, cl]),
                                    bcast(hi_ref[bi, :, cl]), x_r, x_i)
                xr_scr[cb, rows, :] = h_r
                xi_scr[cb, rows, :] = h_i
                hr_ref[bi, :, cl] = h_r[SUBLANES - 1:, :]
                hi_ref[bi, :, cl] = h_i[SUBLANES - 1:, :]
            return carry
        lax.fori_loop(0, Lc // SUBLANES, grp, 0)

    _for_each_seq(bb, seq)

    ys = []
    for kb in range(KB):
        hrb = jnp.concatenate([xr_scr[kb * BPK + t] for t in range(BPK)], axis=1).astype(BF16)
        hib = jnp.concatenate([xi_scr[kb * BPK + t] for t in range(BPK)], axis=1).astype(BF16)
        ys.append(_dot(hrb, cwr_ref[kb]) + _dot(hib, cwi_ref[kb]))
    y = jnp.concatenate(ys, axis=1) + d_ref[...] * u
    g = jax.nn.gelu(y)
    o_ref[...] = (g * jax.nn.sigmoid(_dot(g.astype(BF16), wglu_ref[...]) + bglu_ref[...])).astype(o_ref.dtype)


def _s5(z, B, T, h0r, h0i, tabs, d, wglu, bglu):
    Lc = min(T, 256)
    nC = T // Lc
    bb = 1 if nC > 1 else min(B, max(1, 128 // T))
    bw, ar, ai, pr, pi, cwr, cwi = tabs
    R = bb * Lc
    W = d.shape[1]
    NS = h0r.shape[-1]
    full = lambda a: pl.BlockSpec(a.shape, lambda i, c: (0,) * a.ndim)
    hspec = pl.BlockSpec((bb, 1, NS), lambda i, c: (i, 0, 0))
    return pl.pallas_call(
        functools.partial(_s5_kernel, bb=bb, Lc=Lc),
        grid=(B // bb, nC),
        in_specs=[pl.BlockSpec((R, W), lambda i, c: (i * nC + c, 7)),
                  full(bw), full(ar), full(ai), full(pr), full(pi), full(cwr), full(cwi),
                  full(d), full(wglu), full(bglu), hspec, hspec],
        out_specs=[pl.BlockSpec((R, W), lambda i, c: (i * nC + c, 0)), hspec, hspec],
        out_shape=[jax.ShapeDtypeStruct((B * T, W), _act_dtype(T)),
                   jax.ShapeDtypeStruct((B, 1, NS), F32),
                   jax.ShapeDtypeStruct((B, 1, NS), F32)],
        scratch_shapes=[pltpu.VMEM((NS // LANES, R, LANES), F32), pltpu.VMEM((NS // LANES, R, LANES), F32)],
        compiler_params=_cparams(("parallel", "arbitrary")),
        name="s5",
    )(z, bw, ar, ai, pr, pi, cwr, cwi, d, wglu, bglu, h0r, h0i)


def _s5_tables(a_re, a_im, log_dt, b_re, b_im, c_re, c_im):
    G, P = a_re.shape
    dt = jnp.exp(log_dt.astype(F32))[:, None]
    lam_re = -jnp.abs(a_re.astype(F32))
    lam_im = a_im.astype(F32)

    def power(n):
        n = jnp.asarray(n, F32)[:, None, None]
        mag = jnp.exp(lam_re * dt * n)
        return ((mag * jnp.cos(lam_im * dt * n)).reshape(-1, G * P),
                (mag * jnp.sin(lam_im * dt * n)).reshape(-1, G * P))

    mag = jnp.exp(lam_re * dt)
    ab_re = mag * jnp.cos(lam_im * dt)
    ab_im = mag * jnp.sin(lam_im * dt)
    den = lam_re * lam_re + lam_im * lam_im
    co_re = ((ab_re - 1.0) * lam_re + ab_im * lam_im) / den
    co_im = (ab_im * lam_re - (ab_re - 1.0) * lam_im) / den
    br, bi = b_re.astype(F32), b_im.astype(F32)
    bb_re = co_re[..., None] * br - co_im[..., None] * bi
    bb_im = co_re[..., None] * bi + co_im[..., None] * br
    gpb = LANES // S5_GROUP
    KB = G // gpb
    eye = jnp.eye(gpb, dtype=F32)
    blk = lambda w: jnp.einsum('kgpc,gh->kgchp', w.reshape(KB, gpb, P, S5_GROUP), eye).reshape(KB, LANES, gpb * P)
    bw = jnp.concatenate([blk(bb_re), blk(bb_im)], axis=-1).astype(BF16)
    cblk = lambda w: jnp.einsum('kgcp,gh->kgphc', w.reshape(KB, gpb, S5_GROUP, P), eye).reshape(KB, gpb * P, LANES)
    cwr = cblk(c_re.astype(F32)).astype(BF16)
    cwi = cblk(-c_im.astype(F32)).astype(BF16)
    rows = jnp.arange(SUBLANES)
    ars, ais = [], []
    for lv in range(3):
        r, i = power([1 << lv])
        keep = (rows >= (1 << lv))[:, None]
        ars.append(jnp.where(keep, r, 0.0))
        ais.append(jnp.where(keep, i, 0.0))
    pr, pi = power(np.arange(1, SUBLANES + 1))
    return bw, jnp.stack(ars), jnp.stack(ais), pr, pi, cwr, cwi


def _branch_kernel(or_ref, om_ref, os_ref, gr_ref, gm_ref, gs_ref, wr_ref, wm_ref, ws_ref, o_ref):
    br = _dot(or_ref[...].astype(BF16), wr_ref[...])
    bm = _dot(om_ref[...].astype(BF16), wm_ref[...])
    bs = _dot(os_ref[...].astype(BF16), ws_ref[...])
    merged = (jax.nn.sigmoid(gr_ref[...].astype(F32)) * br + jax.nn.sigmoid(gm_ref[...].astype(F32)) * bm
              + jax.nn.sigmoid(gs_ref[...].astype(F32)) * bs)
    o_ref[...] = merged.astype(BF16)


def _branch(o_r, o_m, o_s, z, wr, wm, ws, l):
    M, W = o_r.shape
    D = wr.shape[2]
    TM, TN = min(M, 512), 1024
    nJ = D // TN
    ospec = pl.BlockSpec((TM, W), lambda i, j: (i, 0))
    gspec = lambda k: pl.BlockSpec((TM, TN), lambda i, j: (i, 8 + k * nJ + j))
    wspec = pl.BlockSpec((None, W, TN), lambda i, j: (l, 0, j))
    return pl.pallas_call(
        _branch_kernel,
        grid=(M // TM, nJ),
        in_specs=[ospec, ospec, ospec, gspec(0), gspec(1), gspec(2), wspec, wspec, wspec],
        out_specs=pl.BlockSpec((TM, TN), lambda i, j: (i, j)),
        out_shape=jax.ShapeDtypeStruct((M, D), BF16),
        compiler_params=_cparams(("parallel", "parallel")),
        name="branch",
    )(o_r, o_m, o_s, z, z, z, wr, wm, ws)


def _wo_kernel(m_ref, w_ref, x_ref, g_ref, o_ref):
    y = _dot(m_ref[...], w_ref[...])
    o_ref[...] = x_ref[...] + g_ref[...] * y.reshape(o_ref.shape)


def _wo(merged, w_o, x, mod, l, moff):
    B, T, D = x.shape
    bb, tt = _tile_cfg(B, T, 512)
    TM, nT = bb * tt, T // tt
    TN = 1024
    xspec = pl.BlockSpec((bb, tt, TN), lambda i, j: (i // nT, i % nT, j))
    return pl.pallas_call(
        _wo_kernel,
        grid=((B // bb) * nT, D // TN),
        in_specs=[pl.BlockSpec((TM, D), lambda i, j: (i, 0)),
                  pl.BlockSpec((None, D, TN), lambda i, j: (l, 0, j)),
                  xspec,
                  pl.BlockSpec((None, bb, 1, TN), lambda i, j: (l, moff // bb + i // nT, 0, 2 * (D // TN) + j))],
        out_specs=xspec,
        out_shape=jax.ShapeDtypeStruct((B, T, D), F32),
        compiler_params=_cparams(("parallel", "parallel")),
        name="wo",
    )(merged, w_o, x, mod)


def _prep_u_kernel(u_ref, o_ref):
    kq = o_ref.shape[-1]
    for j in range(o_ref.shape[0]):
        o_ref[j] = u_ref[:, j * kq:(j + 1) * kq].astype(BF16)


def _prep_u(peer_u):
    L, NE, D = peer_u.shape
    R = 2 * PEER_MC
    kq = D // PEER_KQ
    return pl.pallas_call(
        _prep_u_kernel,
        grid=(L, NE // R),
        in_specs=[pl.BlockSpec((None, R, D), lambda l, p: (l, p, 0))],
        out_specs=pl.BlockSpec((None, None, PEER_KQ, R, kq), lambda l, p: (l, p, 0, 0, 0)),
        out_shape=jax.ShapeDtypeStruct((L, NE // R, PEER_KQ, R, kq), BF16),
        compiler_params=_cparams(("parallel", "parallel")),
        name="prep_u",
    )(peer_u)


def _prep_v_kernel(v_ref, o_ref):
    kq = o_ref.shape[1]
    for j in range(o_ref.shape[0]):
        o_ref[j] = v_ref[:, j * kq:(j + 1) * kq].T.astype(BF16)


def _prep_v(peer_v):
    L, NE, D = peer_v.shape
    kq = D // PEER_KQ
    return pl.pallas_call(
        _prep_v_kernel,
        grid=(L, NE // PEER_MC),
        in_specs=[pl.BlockSpec((None, PEER_MC, D), lambda l, c: (l, c, 0))],
        out_specs=pl.BlockSpec((None, None, PEER_KQ, kq, PEER_MC), lambda l, c: (l, c, 0, 0, 0)),
        out_shape=jax.ShapeDtypeStruct((L, NE // PEER_MC, PEER_KQ, kq, PEER_MC), BF16),
        compiler_params=_cparams(("parallel", "parallel")),
        name="prep_v",
    )(peer_v)


def _peer_q_kernel(x_ref, nw_ref, sh_ref, sc_ref, wq_ref, k1_ref, k2_ref, h_ref, st_ref):
    h = _norm_mod(x_ref[...], nw_ref[...], sc_ref[...], sh_ref[...])
    hb = h.reshape(-1, h.shape[-1]).astype(BF16)
    kq = h_ref.shape[-1]
    for j in range(h_ref.shape[0]):
        h_ref[j] = hb[:, j * kq:(j + 1) * kq]
    q = _dot(hb, wq_ref[...])
    dq = k1_ref.shape[-1]
    for hd in range(PEER_HEADS):
        q1 = q[:, (2 * hd) * dq:(2 * hd + 1) * dq]
        q2 = q[:, (2 * hd + 1) * dq:(2 * hd + 2) * dq]
        st_ref[2 * hd] = _dot_nt_split(k1_ref[hd], q1)
        st_ref[2 * hd + 1] = _dot_nt_split(k2_ref[hd], q2)


def _peer_q(x, mod, l, moff, nw, wq, k1, k2):
    B, T, D = x.shape
    bb, tt = _tile_cfg(B, T, 256)
    TM, nT = bb * tt, T // tt
    full = lambda a: pl.BlockSpec(a.shape, lambda i: (0,) * a.ndim)
    mmap = lambda k: (lambda i: (l, moff // bb + i // nT, 0, k))
    return pl.pallas_call(
        _peer_q_kernel,
        grid=((B // bb) * nT,),
        in_specs=[pl.BlockSpec((bb, tt, D), lambda i: (i // nT, i % nT, 0)),
                  pl.BlockSpec((1, D), lambda i: (0, 0)),
                  pl.BlockSpec((None, bb, 1, D), mmap(3)),
                  pl.BlockSpec((None, bb, 1, D), mmap(4)),
                  pl.BlockSpec((None,) + wq.shape[1:], lambda i: (l, 0, 0)), full(k1), full(k2)],
        out_specs=[pl.BlockSpec((PEER_KQ, TM, D // PEER_KQ), lambda i: (0, i, 0)),
                   pl.BlockSpec((2 * PEER_HEADS, N_KEYS, TM), lambda i: (0, 0, i))],
        out_shape=[jax.ShapeDtypeStruct((PEER_KQ, B * T, D // PEER_KQ), BF16),
                   jax.ShapeDtypeStruct((2 * PEER_HEADS, N_KEYS, B * T), F32)],
        compiler_params=_cparams(("parallel",)),
        name="peer_q",
    )(x, nw, mod, mod, wq, k1, k2)


def _top_values(cur, n):
    vals = []
    for _ in range(n):
        m = jnp.max(cur, axis=0, keepdims=True)
        vals.append(m)
        cur = jnp.where(cur == m, -jnp.inf, cur)
    return vals


def _peer_sel_kernel(st_ref, e1_ref, e2_ref, th_ref):
    ts = st_ref.shape[-1]
    neg = jnp.full((1, ts), -jnp.inf, F32)
    n = PEER_TOPK + 1
    ths = []
    for hd in range(PEER_HEADS):
        s1 = st_ref[2 * hd]
        s2 = st_ref[2 * hd + 1]
        v1 = _top_values(s1, n)
        v2 = _top_values(s2, n)
        rows = [v1[r] + v2[c] for r in range(n) for c in range(n // (r + 1))]
        rows += [neg] * (-len(rows) % SUBLANES)
        top = _top_values(jnp.concatenate(rows, axis=0), n)
        den = jnp.exp(top[0] - top[0])
        for t in top[1:PEER_TOPK]:
            den = den + jnp.exp(t - top[0])
        e1_ref[hd] = jnp.exp(s1 - v1[0]) / den
        e2_ref[hd] = jnp.exp(s2 - v2[0])
        ths.append(jnp.exp(0.5 * (top[PEER_TOPK - 1] + top[PEER_TOPK]) - top[0]) / den)
    th_ref[...] = jnp.concatenate(ths, axis=0)


def _peer_sel(st):
    H2, K, M = st.shape
    TS = min(M, 256)
    hspec = pl.BlockSpec((PEER_HEADS, K, TS), lambda i: (0, 0, i))
    return pl.pallas_call(
        _peer_sel_kernel,
        grid=(M // TS,),
        in_specs=[pl.BlockSpec((H2, K, TS), lambda i: (0, 0, i))],
        out_specs=[hspec, hspec, pl.BlockSpec((PEER_HEADS, TS), lambda i: (0, i))],
        out_shape=[jax.ShapeDtypeStruct((PEER_HEADS, K, M), F32),
                   jax.ShapeDtypeStruct((PEER_HEADS, K, M), F32),
                   jax.ShapeDtypeStruct((PEER_HEADS, M), F32)],
        compiler_params=_cparams(("parallel",)),
        name="peer_sel",
    )(st)


def _peer_exp_kernel(h_ref, u_ref, vta_ref, vtb_ref, e1_ref, e2_ref, th_ref, x_ref, g_ref, o_ref,
                     acc_scr, sc0, sc1, zt0, zt1, *, mc, nc):
    s = pl.program_id(1)
    slabs = mc // N_KEYS

    @pl.when(s == 0)
    def _():
        acc_scr[...] = jnp.zeros(acc_scr.shape, F32)
        sc1[...] = jnp.zeros(sc1.shape, F32)
        zt0[...] = jnp.zeros(zt0.shape, BF16)
        zt1[...] = jnp.zeros(zt1.shape, BF16)

    nq = u_ref.shape[0]
    spp = slabs // nq
    assert spp * nq == slabs

    def substep(half, sc_a, sc_b, zt_b, zt_c, chunk_b):
        c = jnp.clip(chunk_b, 0, nc - 1)
        cols = slice(half * mc, (half + 1) * mc)
        sc_a[...] = jnp.zeros(sc_a.shape, F32)

        def piece(j, carry):
            acc_scr[j] += _dot((vta_ref, vtb_ref)[half][j], zt_c[...])
            for sl in range(spp):
                rs = pl.ds(pl.multiple_of((j * spp + sl) * N_KEYS, N_KEYS), N_KEYS)
                i1 = c * slabs + j * spp + sl
                for tc in range(sc_b.shape[1] // LANES):
                    tl = slice(tc * LANES, (tc + 1) * LANES)
                    w = jnp.zeros((N_KEYS, LANES), F32)
                    for hd in range(PEER_HEADS):
                        p = e1_ref[hd, pl.ds(i1, 1), :][:, tl] * e2_ref[hd, :, tl]
                        w = jnp.where(p >= th_ref[pl.ds(hd, 1), tl], w + p, w)
                    zt_b[rs, tl] = (jax.nn.gelu(sc_b[rs, tl]) * w).astype(BF16)
            sc_a[...] += _dot_nt(u_ref[j, cols, :], h_ref[j])
            return carry

        lax.fori_loop(0, nq, piece, 0)

    substep(0, sc0, sc1, zt1, zt0, 2 * s - 1)
    substep(1, sc1, sc0, zt0, zt1, 2 * s)

    @pl.when(s == pl.num_programs(1) - 1)
    def _():
        o_ref[...] = x_ref[...] + g_ref[...] * acc_scr[...].reshape(-1, acc_scr.shape[-1]).T.reshape(o_ref.shape)


def _peer_exp(hb, u, vt, e1, e2, th, x, mod, l, moff):
    B, T, D = x.shape
    bb, tt = _tile_cfg(B, T, PEER_TB)
    TB, nT = bb * tt, T // tt
    MC = PEER_MC
    NP = u.shape[1]
    NC = vt.shape[1]
    assert NC == 2 * NP and u.shape[3] == 2 * MC and vt.shape[4] == MC
    xspec = pl.BlockSpec((bb, tt, D), lambda i, s: (i // nT, i % nT, 0))
    hspec = pl.BlockSpec((PEER_HEADS, N_KEYS, TB), lambda i, s: (0, 0, i))
    return pl.pallas_call(
        functools.partial(_peer_exp_kernel, mc=MC, nc=NC),
        grid=((B // bb) * nT, NP + 1),
        in_specs=[pl.BlockSpec((PEER_KQ, TB, D // PEER_KQ), lambda i, s: (0, i, 0)),
                  pl.BlockSpec((None, None, PEER_KQ, 2 * MC, D // PEER_KQ),
                               lambda i, s: (l, jnp.minimum(s, NP - 1), 0, 0, 0)),
                  pl.BlockSpec((None, None, PEER_KQ, D // PEER_KQ, MC),
                               lambda i, s: (l, 2 * jnp.maximum(s - 1, 0), 0, 0, 0)),
                  pl.BlockSpec((None, None, PEER_KQ, D // PEER_KQ, MC),
                               lambda i, s: (l, 2 * jnp.maximum(s - 1, 0) + 1, 0, 0, 0)),
                  hspec, hspec,
                  pl.BlockSpec((PEER_HEADS, TB), lambda i, s: (0, i)),
                  xspec,
                  pl.BlockSpec((None, bb, 1, D), lambda i, s: (l, moff // bb + i // nT, 0, 5))],
        out_specs=xspec,
        out_shape=jax.ShapeDtypeStruct((B, T, D), F32),
        scratch_shapes=[pltpu.VMEM((PEER_KQ, D // PEER_KQ, TB), F32),
                        pltpu.VMEM((MC, TB), F32), pltpu.VMEM((MC, TB), F32),
                        pltpu.VMEM((MC, TB), BF16), pltpu.VMEM((MC, TB), BF16)],
        compiler_params=_cparams(("parallel", "arbitrary")),
        name="peer_exp",
    )(hb, u, vt, vt, e1, e2, th, x, mod)


def _final_kernel(x_ref, w_ref, o_ref):
    x = x_ref[...]
    ms = jnp.mean(x * x, axis=-1, keepdims=True)
    o_ref[...] = x * lax.rsqrt(ms + EPS) * w_ref[...]


def _final_norm(x, w):
    B, T, D = x.shape
    bb, tt = _tile_cfg(B, T, 512)
    nT = T // tt
    xspec = pl.BlockSpec((bb, tt, D), lambda i: (i // nT, i % nT, 0))
    return pl.pallas_call(
        _final_kernel,
        grid=((B // bb) * nT,),
        in_specs=[xspec, pl.BlockSpec((1, D), lambda i: (0, 0))],
        out_specs=xspec,
        out_shape=jax.ShapeDtypeStruct((B, T, D), F32),
        compiler_params=_cparams(("parallel",)),
        name="final_norm",
    )(x, w)


def _trunk(x, mod, moff, pos0, big_states, states, lw, final_w):
    B, T, D = x.shape
    depth = len(lw)
    rtabs = _ret_tables(T, pos0)
    ret_in, c_in = big_states
    n_ret = n_c = None
    new = []
    for l, (st, w) in enumerate(zip(states, lw)):
        s_n, s_m, s_conv, s_hr, s_hi = st
        z, zg = _in_proj(x, mod, l, moff, w['n1'], w['w_a'], w['w_b'], w['w_gate'])
        o_r, n_ret = _retention(z, B, T, ret_in, min(l, ret_in.shape[0] - 1), l, depth, n_ret, rtabs, w['ret_gn'])
        o_m, n_c, n_n, n_m, n_conv = _mlstm(z, zg, B, T, c_in, min(l, c_in.shape[0] - 1), l, depth, n_c,
                                            (s_n, s_m, s_conv), w['conv_w'], w['conv_b'],
                                            w['wqk'], w['gate_b'], w['m_gn'], w['m_skip'])
        o_s, n_hr, n_hi = _s5(z, B, T, s_hr, s_hi, w['s5_tabs'], w['s5_d'], w['s5_wglu'], w['s5_bglu'])
        merged = _branch(o_r, o_m, o_s, z, w['w_ret_out'], w['w_mlstm_out'], w['w_s5_out'], l)
        x = _wo(merged, w['w_o'], x, mod, l, moff)
        hb, sc = _peer_q(x, mod, l, moff, w['n2'], w['peer_wq'], w['peer_k1'], w['peer_k2'])
        e1, e2, th = _peer_sel(sc)
        x = _peer_exp(hb, w['peer_u'], w['peer_vt'], e1, e2, th, x, mod, l, moff)
        new.append((n_n, n_m[..., 0], n_conv, n_hr, n_hi))
    y = _final_norm(x, final_w)
    return y, [n_ret, n_c] + [jnp.stack([s[i] for s in new]) for i in range(5)]


def kernel(x_prompt, x_sample, state_ret, state_mlstm_c, state_mlstm_n, state_mlstm_m, state_mlstm_conv,
           state_s5_re, state_s5_im, c_prompt, c_sample, ada_w, ada_b, norm1_w, norm2_w, final_norm_w,
           w_in, ret_gn_w, w_ret_out, mlstm_conv_w, mlstm_conv_b, mlstm_wq, mlstm_wk, mlstm_b_i, mlstm_b_f,
           mlstm_gn_w, mlstm_skip, w_mlstm_out, s5_a_re, s5_a_im, s5_log_dt, s5_b_re, s5_b_im, s5_c_re,
           s5_c_im, s5_d, s5_w_glu, s5_b_glu, w_s5_out, w_o, peer_wq, peer_k1, peer_k2, peer_u, peer_v):
    depth = w_in.shape[0]
    Bp, Tp, D = x_prompt.shape
    Bs, Ts, _ = x_sample.shape
    W = HEADS * DH
    G, P = s5_a_re.shape[1:]
    NS = G * P
    past_len = 16384

    pad = -(Bs + Bp) % SUBLANES
    c_all = jnp.concatenate([c_sample, c_prompt, jnp.zeros((pad, D), F32)], axis=0)
    mod = _ada(c_all, ada_w, ada_b)
    mod = mod.reshape(depth, c_all.shape[0], 1, ada_w.shape[2])

    a_end = 7 * W
    g_end = a_end + 2 * HEADS
    w_ret_out_b, w_mlstm_out_b, w_s5_out_b = w_ret_out.astype(BF16), w_mlstm_out.astype(BF16), w_s5_out.astype(BF16)
    w_o_b, peer_wq_b = w_o.astype(BF16), peer_wq.astype(BF16)
    u_tiles = _prep_u(peer_u)
    vt_tiles = _prep_v(peer_v)
    lw = []
    for l in range(depth):
        w_gate = jnp.pad(w_in[l, :, a_end:g_end], ((0, 0), (0, LANES - 2 * HEADS))).astype(BF16)
        gate_b = jnp.pad(jnp.concatenate([mlstm_b_i[l], mlstm_b_f[l]]), (0, LANES - 2 * HEADS)).reshape(1, LANES)
        lw.append(dict(
            n1=norm1_w[l].reshape(1, D), n2=norm2_w[l].reshape(1, D),
            w_a=w_in[l, :, :a_end].astype(BF16), w_b=w_in[l, :, g_end:].astype(BF16), w_gate=w_gate, gate_b=gate_b,
            ret_gn=ret_gn_w[l].reshape(1, W),
            conv_w=mlstm_conv_w[l], conv_b=mlstm_conv_b[l].reshape(1, W),
            wqk=jnp.concatenate([mlstm_wq[l], mlstm_wk[l]], axis=-1).astype(BF16),
            m_gn=mlstm_gn_w[l].reshape(1, W), m_skip=mlstm_skip[l].reshape(1, W),
            s5_tabs=_s5_tables(s5_a_re[l], s5_a_im[l], s5_log_dt[l], s5_b_re[l], s5_b_im[l], s5_c_re[l], s5_c_im[l]),
            s5_d=s5_d[l].reshape(1, -1), s5_wglu=s5_w_glu[l].astype(BF16), s5_bglu=s5_b_glu[l].reshape(1, -1),
            w_ret_out=w_ret_out_b, w_mlstm_out=w_mlstm_out_b, w_s5_out=w_s5_out_b, w_o=w_o_b,
            peer_wq=peer_wq_b, peer_k1=peer_k1[l], peer_k2=peer_k2[l],
            peer_u=u_tiles, peer_vt=vt_tiles))

    def zero_states(B):
        return (jnp.zeros((B, HEADS, DH), F32), jnp.zeros((B, HEADS, LANES), F32),
                jnp.zeros((B, CONV_W - 1, W), F32), jnp.zeros((B, 1, NS), F32), jnp.zeros((B, 1, NS), F32))

    zero_big = jnp.zeros((1, Bp, HEADS, DH, DH), F32)
    prompt_states = [zero_states(Bp) for _ in range(depth)]
    sample_states = [(state_mlstm_n[l],
                      jnp.broadcast_to(state_mlstm_m[l][..., None], (Bs, HEADS, LANES)),
                      state_mlstm_conv[l], state_s5_re[l].reshape(Bs, 1, NS), state_s5_im[l].reshape(Bs, 1, NS))
                     for l in range(depth)]

    y_p, ps = _trunk(x_prompt, mod, Bs, 0, (zero_big, zero_big), prompt_states, lw, final_norm_w.reshape(1, D))
    y_s, ss = _trunk(x_sample, mod, 0, past_len, (state_ret, state_mlstm_c), sample_states, lw,
                     final_norm_w.reshape(1, D))

    def unpack(st, B):
        r, c, n, m, conv, hr, hi = st
        return (r, c, n, m, conv, hr.reshape(depth, B, G, P), hi.reshape(depth, B, G, P))

    return (y_p, y_s) + unpack(ps, Bp) + unpack(ss, Bs)
```

```python
import functools
import math

import jax
import jax.numpy as jnp
import numpy as np
from jax import lax
from jax.experimental import pallas as pl
from jax.experimental.pallas import tpu as pltpu

F32 = jnp.float32
BF16 = jnp.bfloat16

EPS = 1e-6
ROPE_BASE = 10000.0
HEADS = 8
DH = 128
CONV_W = 4
S5_GROUP = 16
S5_STATE = 64
PEER_HEADS = 8
PEER_TOPK = 16
N_KEYS = 128
CHUNK = 128
LANES = 128
SUBLANES = 8
VMEM_LIMIT = 56 * 1024 * 1024
RET_GROUP = 8
MLSTM_GROUP = 1
MLSTM_HEADS_PER_PASS = 4
PEER_TB = 512
PEER_MC = 512
PEER_KQ = 4


def _cparams(sem):
    return pltpu.CompilerParams(dimension_semantics=sem, vmem_limit_bytes=VMEM_LIMIT)


def _tile_cfg(B, T, target):
    if T >= target:
        bb, tt = 1, target
    else:
        bb, tt = min(B, target // T), T
    assert T % tt == 0 and B % bb == 0
    return bb, tt


def _act_dtype(T):
    return BF16 if T % CHUNK == 0 else F32


def _silu(x):
    return x * jax.nn.sigmoid(x)


def _norm_mod(x, w, sc, sh):
    ms = jnp.mean(x * x, axis=-1, keepdims=True)
    y = x * lax.rsqrt(ms + EPS) * w
    return y * (1.0 + sc) + sh


def _head_norm(x):
    mu = jnp.mean(x, axis=-1, keepdims=True)
    xc = x - mu
    var = jnp.mean(xc * xc, axis=-1, keepdims=True)
    return xc * lax.rsqrt(var + EPS)


def _dot(a, b):
    return jnp.dot(a, b, preferred_element_type=F32)


def _dot_nt(a, b, **kw):
    return lax.dot_general(a, b, (((1,), (1,)), ((), ())), preferred_element_type=F32, **kw)


def _dot_nt_split(a, b):
    ah, bh = a.astype(BF16), b.astype(BF16)
    al = (a - ah.astype(F32)).astype(BF16)
    bl = (b - bh.astype(F32)).astype(BF16)
    return _dot_nt(ah, bh) + (_dot_nt(ah, bl) + _dot_nt(al, bh))


def _dot_tn(a, b):
    return lax.dot_general(a, b, (((0,), (0,)), ((), ())), preferred_element_type=F32)


def _rows(start, n):
    if isinstance(start, int):
        return pl.ds(start, n)
    return pl.ds(pl.multiple_of(start, SUBLANES), n)


def _for_each_group(bb, gs, body):
    gs = min(gs, bb)
    assert bb % gs == 0
    if bb == gs:
        body(list(range(bb)))
    else:
        def f(gi, c):
            body([gi * gs + i for i in range(gs)])
            return c
        lax.fori_loop(0, bb // gs, f, 0)


def _for_each_seq(bb, body, unroll=1):
    if bb == 1:
        body(0)
    else:
        def f(bi, c):
            body(bi)
            return c
        lax.fori_loop(0, bb, f, 0, unroll=unroll)


def _ada_kernel(c_ref, w_ref, b_ref, o_ref):
    a = _silu(c_ref[...]).astype(BF16)
    o_ref[...] = _dot(a, w_ref[...].astype(BF16)) + b_ref[...]


def _ada(c_all, ada_w, ada_b):
    L, D, N = ada_w.shape
    Bc = c_all.shape[0]
    TN = 1024
    return pl.pallas_call(
        _ada_kernel,
        grid=(L, N // TN),
        in_specs=[pl.BlockSpec((Bc, D), lambda l, j: (0, 0)),
                  pl.BlockSpec((None, D, TN), lambda l, j: (l, 0, j)),
                  pl.BlockSpec((None, 1, TN), lambda l, j: (l, 0, j))],
        out_specs=pl.BlockSpec((None, Bc, TN), lambda l, j: (l, 0, j)),
        out_shape=jax.ShapeDtypeStruct((L, Bc, N), F32),
        compiler_params=_cparams(("parallel", "parallel")),
        name="ada",
    )(c_all, ada_w, ada_b.reshape(L, 1, N))


def _in_kernel(x_ref, nw_ref, sh_ref, sc_ref, wa_ref, wb_ref, wg_ref, z_ref, zg_ref, h_scr, *, na):
    j = pl.program_id(1)

    @pl.when(j == 0)
    def _():
        h = _norm_mod(x_ref[...], nw_ref[...], sc_ref[...], sh_ref[...])
        hb = h.reshape(h_scr.shape).astype(BF16)
        h_scr[...] = hb
        zg_ref[...] = _dot(hb, wg_ref[...])

    @pl.when(j < na)
    def _():
        z_ref[...] = _dot(h_scr[...], wa_ref[...]).astype(z_ref.dtype)

    @pl.when(j >= na)
    def _():
        z_ref[...] = _dot(h_scr[...], wb_ref[...]).astype(z_ref.dtype)


def _in_proj(x, mod, l, moff, nw, wa, wb, wg):
    B, T, D = x.shape
    bb, tt = _tile_cfg(B, T, 1024)
    TM, nT = bb * tt, T // tt
    nI = (B // bb) * nT
    TN = 1024
    na = wa.shape[1] // TN
    N = wa.shape[1] + wb.shape[1]
    xmap = lambda i, j: (i // nT, i % nT, 0)
    mmap = lambda k: (lambda i, j: (l, moff // bb + i // nT, 0, k))
    return pl.pallas_call(
        functools.partial(_in_kernel, na=na),
        grid=(nI, N // TN),
        in_specs=[pl.BlockSpec((bb, tt, D), xmap),
                  pl.BlockSpec((1, D), lambda i, j: (0, 0)),
                  pl.BlockSpec((None, bb, 1, D), mmap(0)),
                  pl.BlockSpec((None, bb, 1, D), mmap(1)),
                  pl.BlockSpec((D, TN), lambda i, j: (0, jnp.minimum(j, na - 1))),
                  pl.BlockSpec((D, TN), lambda i, j: (0, jnp.maximum(j - na, 0))),
                  pl.BlockSpec((D, LANES), lambda i, j: (0, 0))],
        out_specs=[pl.BlockSpec((TM, TN), lambda i, j: (i, j)),
                   pl.BlockSpec((TM, LANES), lambda i, j: (i, 0))],
        out_shape=[jax.ShapeDtypeStruct((B * T, N), _act_dtype(T)),
                   jax.ShapeDtypeStruct((B * T, LANES), F32)],
        scratch_shapes=[pltpu.VMEM((TM, D), BF16)],
        compiler_params=_cparams(("parallel", "arbitrary")),
        name="in_proj",
    )(x, nw, mod, mod, wa, wb, wg)


def _ret_kernel(*refs, bb, L):
    (zq_ref, zk_ref, zv_ref, zg_ref, cq_ref, sq_ref, ck_ref, sk_ref, intra_ref, cross_ref,
     kdec_ref, cdec_ref, gn_ref, s0_ref) = refs[:14]
    o_ref, s_ref = refs[-2:]

    @pl.when(pl.program_id(1) == 0)
    def _():
        s_ref[...] = s0_ref[...]

    cq, sq, ck, sk = cq_ref[...], sq_ref[...], ck_ref[...], sk_ref[...]

    def group(bis):
        chains = [(i, h) for i in range(len(bis)) for h in range(HEADS)]
        rows = [_rows(bi * L, L) for bi in bis]
        cols = [slice(h * DH, (h + 1) * DH) for h in range(HEADS)]
        q, k, v, s = {}, {}, {}, {}
        for c in chains:
            i, h = c
            qf = zq_ref[rows[i], cols[h]].astype(F32)
            kf = zk_ref[rows[i], cols[h]].astype(F32)
            q[c] = (qf * cq + pltpu.roll(qf, DH // 2, 1) * sq).astype(BF16)
            k[c] = kf * ck + pltpu.roll(kf, DH // 2, 1) * sk
            v[c] = zv_ref[rows[i], cols[h]].astype(BF16)
            s[c] = s_ref[bis[i], h]
        att = {c: _dot_nt(q[c], k[c].astype(BF16)) for c in chains}
        inter = {c: _dot(q[c], s[c].astype(BF16)) for c in chains}
        upd = {c: _dot_tn((k[c] * kdec_ref[c[1]]).astype(BF16), v[c]) for c in chains}
        o = {c: _dot((att[c] * intra_ref[c[1]]).astype(BF16), v[c]) + inter[c] * cross_ref[c[1]] for c in chains}
        for c in chains:
            i, h = c
            s_ref[bis[i], h] = cdec_ref[h] * s[c] + upd[c]
            g = zg_ref[rows[i], cols[h]].astype(F32)
            o_ref[rows[i], cols[h]] = (_head_norm(o[c]) * gn_ref[:, cols[h]] * _silu(g)).astype(o_ref.dtype)

    _for_each_group(bb, RET_GROUP, group)


def _layer_state_io(s_in, lin, l, depth, prev, bb, n_inputs, out_index):
    tail = s_in.shape[2:]
    zeros = (0,) * len(tail)
    in_spec = pl.BlockSpec((None, bb) + tail, lambda i, c: (lin, i) + zeros)
    out_spec = pl.BlockSpec((None, bb) + tail, lambda i, c: (l, i) + zeros)
    out_shape = jax.ShapeDtypeStruct((depth,) + s_in.shape[1:], s_in.dtype)
    extra_specs, extra_args, aliases = [], [], {}
    if prev is not None:
        extra_specs = [pl.BlockSpec(memory_space=pl.ANY)]
        extra_args = [prev]
        aliases = {n_inputs: out_index}
    return in_spec, out_spec, out_shape, extra_specs, extra_args, aliases


def _retention(z, B, T, s_in, lin, l, depth, s_prev, tabs, gn_w):
    L = CHUNK if T % CHUNK == 0 else T
    nC = T // L
    bb = 1 if nC > 1 else min(B, 8)
    R = bb * L
    W = HEADS * DH
    cq, sq, ck, sk, intra, cross, kdec, cdec = tabs
    zspec = lambda k: pl.BlockSpec((R, W), lambda i, c: (i * nC + c, k))
    tspec = pl.BlockSpec((L, DH), lambda i, c: (c, 0))
    full = lambda a: pl.BlockSpec(a.shape, lambda i, c: (0,) * a.ndim)
    s_ispec, s_ospec, s_shape, xspecs, xargs, aliases = _layer_state_io(s_in, lin, l, depth, s_prev, bb, 14, 1)
    return pl.pallas_call(
        functools.partial(_ret_kernel, bb=bb, L=L),
        grid=(B // bb, nC),
        in_specs=[zspec(0), zspec(1), zspec(2), zspec(3), tspec, tspec, tspec, tspec,
                  full(intra), full(cross), full(kdec), full(cdec), full(gn_w), s_ispec] + xspecs,
        out_specs=[pl.BlockSpec((R, W), lambda i, c: (i * nC + c, 0)), s_ospec],
        out_shape=[jax.ShapeDtypeStruct((B * T, W), _act_dtype(T)), s_shape],
        input_output_aliases=aliases,
        compiler_params=_cparams(("parallel", "arbitrary")),
        name="ret",
    )(z, z, z, z, cq, sq, ck, sk, intra, cross, kdec, cdec, gn_w, s_in, *xargs)


def _ret_tables(T, pos0):
    L = CHUNK if T % CHUNK == 0 else T
    half = DH // 2
    inv = jnp.exp(-math.log(ROPE_BASE) * jnp.arange(half, dtype=F32) / half)
    pos = jnp.arange(T, dtype=F32) + pos0
    ang = pos[:, None] * inv[None, :]
    cos, sin = jnp.cos(ang), jnp.sin(ang)
    c = jnp.concatenate([cos, cos], axis=-1)
    s = jnp.concatenate([-sin, sin], axis=-1)
    kscale = DH ** -0.5
    lg = jnp.log1p(-jnp.exp2(-5.0 - jnp.arange(HEADS, dtype=F32)))
    j = jnp.arange(L, dtype=F32)
    diff = j[:, None] - j[None, :]
    intra = jnp.where(diff >= 0, jnp.exp(lg[:, None, None] * jnp.maximum(diff, 0.0)), 0.0)
    cross = jnp.exp(lg[:, None] * (j + 1.0))
    kdec = jnp.exp(lg[:, None] * (L - 1.0 - j))
    cdec = jnp.exp(lg * L)
    bl = lambda a: jnp.broadcast_to(a[..., None], a.shape + (DH,))
    return (c, s, c * kscale, s * kscale, intra, bl(cross), bl(kdec), bl(cdec[:, None]))


def _mlstm_kernel(*refs, bb, L, last, hp):
    (zu_ref, zv_ref, zo_ref, zg_ref, cw_ref, cb_ref, wqk_ref, gb_ref, gn_ref, sk_ref,
     c0_ref, n0_ref, m0_ref, conv0_ref) = refs[:14]
    o_ref, c_ref, n_ref, m_ref, conv_ref, xp_scr, tail_scr, ca_scr = refs[-8:]
    ci = pl.program_id(1)

    @pl.when(ci == 0)
    def _():
        c_ref[...] = c0_ref[...]
        n_ref[...] = n0_ref[...]
        m_ref[...] = m0_ref[...]
        tail_scr[...] = jnp.zeros(tail_scr.shape, F32)
        tail_scr[:, SUBLANES - (CONV_W - 1):, :] = conv0_ref[...]

    ri = lax.broadcasted_iota(jnp.int32, (L, L), 0)
    rj = lax.broadcasted_iota(jnp.int32, (L, L), 1)
    causal = rj <= ri
    tri = causal.astype(F32)
    lane = lax.broadcasted_iota(jnp.int32, (L, LANES), 1)
    kscale = DH ** -0.5

    def group(bis):
        ns = len(bis)
        rows = [_rows(bi * L, L) for bi in bis]
        cols = [slice(h * DH, (h + 1) * DH) for h in range(HEADS)]
        gates, csum, gates_t, csum_t, m_all, n_all = [], [], [], [], [], []
        for i, bi in enumerate(bis):
            xp_scr[bi, 0:SUBLANES, :] = tail_scr[bi]
            xp_scr[bi, SUBLANES:, :] = zu_ref[rows[i], :].astype(F32)
            cu = cb_ref[...]
            for t in range(CONV_W):
                cu = cu + xp_scr[bi, pl.ds(SUBLANES - (CONV_W - 1) + t, L), :] * cw_ref[pl.ds(t, 1), :]
            tail_scr[bi] = xp_scr[bi, L:L + SUBLANES, :]
            ca_scr[bi] = _silu(cu)
            gz = zg_ref[rows[i], :] + gb_ref[...]
            g = jnp.where(lane < HEADS, gz, jax.nn.log_sigmoid(gz))
            cs = jnp.dot(tri, g, preferred_element_type=F32, precision=lax.Precision.HIGHEST)
            gates.append(g)
            csum.append(cs)
            gates_t.append(g.T)
            csum_t.append(cs.T)
            m_all.append(m_ref[bi])
            n_all.append(n_ref[bi])
        n_rows = [[] for _ in range(ns)]
        m_rows = [[] for _ in range(ns)]

        def run(chains):
            ca = {c: ca_scr[bis[c[0]], :, cols[c[1]]] for c in chains}
            qk = {c: _dot(ca[c].astype(BF16), wqk_ref[c[1]]) for c in chains}
            q = {c: qk[c][:, :DH] for c in chains}
            k = {c: qk[c][:, DH:] * kscale for c in chains}
            qb = {c: q[c].astype(BF16) for c in chains}
            v = {c: zv_ref[rows[c[0]], cols[c[1]]].astype(BF16) for c in chains}
            c_prev = {c: c_ref[bis[c[0]], c[1]] for c in chains}
            sraw = {c: _dot_nt(qb[c], k[c].astype(BF16)) for c in chains}
            qc = {c: _dot(qb[c], c_prev[c].astype(BF16)) for c in chains}
            s, mt, w_int, m_prev, n_prev, b_col, i_col = {}, {}, {}, {}, {}, {}, {}
            for c in chains:
                i, h = c
                i_col[c] = gates[i][:, h:h + 1]
                b_col[c] = csum[i][:, HEADS + h:HEADS + h + 1]
                i_row = gates_t[i][h:h + 1, :]
                b_row = csum_t[i][HEADS + h:HEADS + h + 1, :]
                m_prev[c] = m_all[i][h:h + 1, :1]
                n_prev[c] = n_all[i][h:h + 1, :]
                dlog = jnp.where(causal, b_col[c] - b_row + i_row, -jnp.inf)
                inter = b_col[c] + m_prev[c]
                mt[c] = jnp.maximum(inter, jnp.max(dlog, axis=-1, keepdims=True))
                s[c] = sraw[c] * jnp.exp(dlog - mt[c])
                w_int[c] = jnp.exp(inter - mt[c])
            sv = {c: _dot(s[c].astype(BF16), v[c]) for c in chains}
            kt, dec, m_new = {}, {}, {}
            for c in chains:
                m_new[c] = mt[c][L - 1:L, :]
                b_last = b_col[c][L - 1:L, :]
                tail = jnp.exp(b_last - b_col[c] + i_col[c] - m_new[c])
                dec[c] = jnp.exp(b_last + m_prev[c] - m_new[c])
                kt[c] = k[c] * tail
            upd = {c: _dot_tn(kt[c].astype(BF16), v[c]) for c in chains}
            for c in chains:
                i, h = c
                num = sv[c] + w_int[c] * qc[c]
                den = (jnp.sum(s[c], axis=-1, keepdims=True)
                       + w_int[c] * jnp.sum(q[c] * n_prev[c], axis=-1, keepdims=True))
                hh = num / jnp.maximum(jnp.abs(den), jnp.exp(-mt[c]))
                c_ref[bis[i], h] = dec[c] * c_prev[c] + upd[c]
                n_rows[i].append(dec[c] * n_prev[c] + jnp.sum(kt[c], axis=0, keepdims=True))
                m_rows[i].append(jnp.broadcast_to(m_new[c], (1, LANES)))
                hm = jax.nn.sigmoid(zo_ref[rows[i], cols[h]].astype(F32)) * hh
                o_ref[rows[i], cols[h]] = (_head_norm(hm) * gn_ref[:, cols[h]]
                                           + sk_ref[:, cols[h]] * ca[c]).astype(o_ref.dtype)

        for h0 in range(0, HEADS, hp):
            run([(i, h) for i in range(ns) for h in range(h0, h0 + hp)])
        for i, bi in enumerate(bis):
            n_ref[bi] = jnp.concatenate(n_rows[i], axis=0)
            m_ref[bi] = jnp.concatenate(m_rows[i], axis=0)

    _for_each_group(bb, MLSTM_GROUP, group)

    @pl.when(ci == last)
    def _():
        conv_ref[...] = tail_scr[:, SUBLANES - (CONV_W - 1):, :]


def _mlstm(z, zg, B, T, c_in, lin, l, depth, c_prev, states, cw, cb, wqk, gb, gn_w, skip):
    L = CHUNK if T % CHUNK == 0 else T
    nC = T // L
    bb = 1 if nC > 1 else min(B, 8)
    R = bb * L
    W = HEADS * DH
    n0, m0, conv0 = states
    zspec = lambda k: pl.BlockSpec((R, W), lambda i, c: (i * nC + c, k))
    full = lambda a: pl.BlockSpec(a.shape, lambda i, c: (0,) * a.ndim)
    nspec = pl.BlockSpec((bb, HEADS, DH), lambda i, c: (i, 0, 0))
    vspec = pl.BlockSpec((bb, CONV_W - 1, W), lambda i, c: (i, 0, 0))
    c_ispec, c_ospec, c_shape, xspecs, xargs, aliases = _layer_state_io(c_in, lin, l, depth, c_prev, bb, 14, 1)
    return pl.pallas_call(
        functools.partial(_mlstm_kernel, bb=bb, L=L, last=nC - 1,
                          hp=MLSTM_HEADS_PER_PASS if nC > 1 else HEADS),
        grid=(B // bb, nC),
        in_specs=[zspec(4), zspec(5), zspec(6), pl.BlockSpec((R, LANES), lambda i, c: (i * nC + c, 0)),
                  full(cw), full(cb), full(wqk), full(gb), full(gn_w), full(skip),
                  c_ispec, nspec, nspec, vspec] + xspecs,
        out_specs=[pl.BlockSpec((R, W), lambda i, c: (i * nC + c, 0)), c_ospec, nspec, nspec, vspec],
        input_output_aliases=aliases,
        out_shape=[jax.ShapeDtypeStruct((B * T, W), _act_dtype(T)),
                   c_shape,
                   jax.ShapeDtypeStruct((B, HEADS, DH), F32),
                   jax.ShapeDtypeStruct((B, HEADS, LANES), F32),
                   jax.ShapeDtypeStruct((B, CONV_W - 1, W), F32)],
        scratch_shapes=[pltpu.VMEM((bb, L + SUBLANES, W), F32),
                        pltpu.VMEM((bb, SUBLANES, W), F32),
                        pltpu.VMEM((bb, L, W), F32)],
        compiler_params=_cparams(("parallel", "arbitrary")),
        name="mlstm",
    )(z, z, z, zg, cw, cb, wqk, gb, gn_w, skip, c_in, n0, m0, conv0, *xargs)


def _s5_kernel(zu_ref, bw_ref, ar_ref, ai_ref, pr_ref, pi_ref, cwr_ref, cwi_ref, d_ref, wglu_ref, bglu_ref,
               h0r_ref, h0i_ref, o_ref, hr_ref, hi_ref, xr_scr, xi_scr, *, bb, Lc):
    @pl.when(pl.program_id(1) == 0)
    def _():
        hr_ref[...] = h0r_ref[...]
        hi_ref[...] = h0i_ref[...]

    NB = xr_scr.shape[0]
    KB = bw_ref.shape[0]
    BPK = NB // KB
    u = zu_ref[...].astype(F32)
    ub = u.astype(BF16)
    for kb in range(KB):
        r = _dot(ub[:, kb * LANES:(kb + 1) * LANES], bw_ref[kb])
        for t in range(BPK):
            xr_scr[kb * BPK + t] = r[:, t * LANES:(t + 1) * LANES]
            xi_scr[kb * BPK + t] = r[:, (BPK + t) * LANES:(BPK + t + 1) * LANES]

    def cmul_add(a_r, a_i, x_r, x_i, b_r, b_i):
        return a_r * x_r - a_i * x_i + b_r, a_r * x_i + a_i * x_r + b_i

    def scan8(x_r, x_i, cb):
        cl = slice(cb * LANES, (cb + 1) * LANES)
        for lv in range(3):
            x_r, x_i = cmul_add(ar_ref[lv, :, cl], ai_ref[lv, :, cl], pltpu.roll(x_r, 1 << lv, 0),
                                pltpu.roll(x_i, 1 << lv, 0), x_r, x_i)
        return x_r, x_i

    def bcast(row):
        return jnp.broadcast_to(row, (SUBLANES, LANES))

    def seq(bi):
        def grp(gi, carry):
            rows = _rows(bi * Lc + gi * SUBLANES, SUBLANES)
            for cb in range(NB):
                cl = slice(cb * LANES, (cb + 1) * LANES)
                x_r, x_i = scan8(xr_scr[cb, rows, :], xi_scr[cb, rows, :], cb)
                h_r, h_i = cmul_add(pr_ref[:, cl], pi_ref[:, cl], bcast(hr_ref[bi, :, cl]),
                                    bcast(hi_ref[bi, :, cl]), x_r, x_i)
                xr_scr[cb, rows, :] = h_r
                xi_scr[cb, rows, :] = h_i
                hr_ref[bi, :, cl] = h_r[SUBLANES - 1:, :]
                hi_ref[bi, :, cl] = h_i[SUBLANES - 1:, :]
            return carry
        lax.fori_loop(0, Lc // SUBLANES, grp, 0)

    _for_each_seq(bb, seq)

    ys = []
    for kb in range(KB):
        hrb = jnp.concatenate([xr_scr[kb * BPK + t] for t in range(BPK)], axis=1).astype(BF16)
        hib = jnp.concatenate([xi_scr[kb * BPK + t] for t in range(BPK)], axis=1).astype(BF16)
        ys.append(_dot(hrb, cwr_ref[kb]) + _dot(hib, cwi_ref[kb]))
    y = jnp.concatenate(ys, axis=1) + d_ref[...] * u
    g = jax.nn.gelu(y)
    o_ref[...] = (g * jax.nn.sigmoid(_dot(g.astype(BF16), wglu_ref[...]) + bglu_ref[...])).astype(o_ref.dtype)


def _s5(z, B, T, h0r, h0i, tabs, d, wglu, bglu):
    Lc = min(T, 256)
    nC = T // Lc
    bb = 1 if nC > 1 else min(B, max(1, 128 // T))
    bw, ar, ai, pr, pi, cwr, cwi = tabs
    R = bb * Lc
    W = d.shape[1]
    NS = h0r.shape[-1]
    full = lambda a: pl.BlockSpec(a.shape, lambda i, c: (0,) * a.ndim)
    hspec = pl.BlockSpec((bb, 1, NS), lambda i, c: (i, 0, 0))
    return pl.pallas_call(
        functools.partial(_s5_kernel, bb=bb, Lc=Lc),
        grid=(B // bb, nC),
        in_specs=[pl.BlockSpec((R, W), lambda i, c: (i * nC + c, 7)),
                  full(bw), full(ar), full(ai), full(pr), full(pi), full(cwr), full(cwi),
                  full(d), full(wglu), full(bglu), hspec, hspec],
        out_specs=[pl.BlockSpec((R, W), lambda i, c: (i * nC + c, 0)), hspec, hspec],
        out_shape=[jax.ShapeDtypeStruct((B * T, W), _act_dtype(T)),
                   jax.ShapeDtypeStruct((B, 1, NS), F32),
                   jax.ShapeDtypeStruct((B, 1, NS), F32)],
        scratch_shapes=[pltpu.VMEM((NS // LANES, R, LANES), F32), pltpu.VMEM((NS // LANES, R, LANES), F32)],
        compiler_params=_cparams(("parallel", "arbitrary")),
        name="s5",
    )(z, bw, ar, ai, pr, pi, cwr, cwi, d, wglu, bglu, h0r, h0i)


def _s5_tables(a_re, a_im, log_dt, b_re, b_im, c_re, c_im):
    G, P = a_re.shape
    dt = jnp.exp(log_dt.astype(F32))[:, None]
    lam_re = -jnp.abs(a_re.astype(F32))
    lam_im = a_im.astype(F32)

    def power(n):
        n = jnp.asarray(n, F32)[:, None, None]
        mag = jnp.exp(lam_re * dt * n)
        return ((mag * jnp.cos(lam_im * dt * n)).reshape(-1, G * P),
                (mag * jnp.sin(lam_im * dt * n)).reshape(-1, G * P))

    mag = jnp.exp(lam_re * dt)
    ab_re = mag * jnp.cos(lam_im * dt)
    ab_im = mag * jnp.sin(lam_im * dt)
    den = lam_re * lam_re + lam_im * lam_im
    co_re = ((ab_re - 1.0) * lam_re + ab_im * lam_im) / den
    co_im = (ab_im * lam_re - (ab_re - 1.0) * lam_im) / den
    br, bi = b_re.astype(F32), b_im.astype(F32)
    bb_re = co_re[..., None] * br - co_im[..., None] * bi
    bb_im = co_re[..., None] * bi + co_im[..., None] * br
    gpb = LANES // S5_GROUP
    KB = G // gpb
    eye = jnp.eye(gpb, dtype=F32)
    blk = lambda w: jnp.einsum('kgpc,gh->kgchp', w.reshape(KB, gpb, P, S5_GROUP), eye).reshape(KB, LANES, gpb * P)
    bw = jnp.concatenate([blk(bb_re), blk(bb_im)], axis=-1).astype(BF16)
    cblk = lambda w: jnp.einsum('kgcp,gh->kgphc', w.reshape(KB, gpb, S5_GROUP, P), eye).reshape(KB, gpb * P, LANES)
    cwr = cblk(c_re.astype(F32)).astype(BF16)
    cwi = cblk(-c_im.astype(F32)).astype(BF16)
    rows = jnp.arange(SUBLANES)
    ars, ais = [], []
    for lv in range(3):
        r, i = power([1 << lv])
        keep = (rows >= (1 << lv))[:, None]
        ars.append(jnp.where(keep, r, 0.0))
        ais.append(jnp.where(keep, i, 0.0))
    pr, pi = power(np.arange(1, SUBLANES + 1))
    return bw, jnp.stack(ars), jnp.stack(ais), pr, pi, cwr, cwi


def _branch_kernel(or_ref, om_ref, os_ref, gr_ref, gm_ref, gs_ref, wr_ref, wm_ref, ws_ref, o_ref):
    br = _dot(or_ref[...].astype(BF16), wr_ref[...])
    bm = _dot(om_ref[...].astype(BF16), wm_ref[...])
    bs = _dot(os_ref[...].astype(BF16), ws_ref[...])
    merged = (jax.nn.sigmoid(gr_ref[...].astype(F32)) * br + jax.nn.sigmoid(gm_ref[...].astype(F32)) * bm
              + jax.nn.sigmoid(gs_ref[...].astype(F32)) * bs)
    o_ref[...] = merged.astype(BF16)


def _branch(o_r, o_m, o_s, z, wr, wm, ws, l):
    M, W = o_r.shape
    D = wr.shape[2]
    TM, TN = min(M, 512), 1024
    nJ = D // TN
    ospec = pl.BlockSpec((TM, W), lambda j, i: (i, 0))
    gspec = lambda k: pl.BlockSpec((TM, TN), lambda j, i: (i, 8 + k * nJ + j))
    wspec = pl.BlockSpec((None, W, TN), lambda j, i: (l, 0, j))
    return pl.pallas_call(
        _branch_kernel,
        grid=(nJ, M // TM),
        in_specs=[ospec, ospec, ospec, gspec(0), gspec(1), gspec(2), wspec, wspec, wspec],
        out_specs=pl.BlockSpec((TM, TN), lambda j, i: (i, j)),
        out_shape=jax.ShapeDtypeStruct((M, D), BF16),
        compiler_params=_cparams(("parallel", "parallel")),
        name="branch",
    )(o_r, o_m, o_s, z, z, z, wr, wm, ws)


def _wo_kernel(m_ref, w_ref, x_ref, g_ref, o_ref):
    y = _dot(m_ref[...], w_ref[...])
    o_ref[...] = x_ref[...] + g_ref[...] * y.reshape(o_ref.shape)


def _wo(merged, w_o, x, mod, l, moff):
    B, T, D = x.shape
    bb, tt = _tile_cfg(B, T, 512)
    TM, nT = bb * tt, T // tt
    TN = 1024
    xspec = pl.BlockSpec((bb, tt, TN), lambda j, i: (i // nT, i % nT, j))
    return pl.pallas_call(
        _wo_kernel,
        grid=(D // TN, (B // bb) * nT),
        in_specs=[pl.BlockSpec((TM, D), lambda j, i: (i, 0)),
                  pl.BlockSpec((None, D, TN), lambda j, i: (l, 0, j)),
                  xspec,
                  pl.BlockSpec((None, bb, 1, TN), lambda j, i: (l, moff // bb + i // nT, 0, 2 * (D // TN) + j))],
        out_specs=xspec,
        out_shape=jax.ShapeDtypeStruct((B, T, D), F32),
        compiler_params=_cparams(("parallel", "parallel")),
        name="wo",
    )(merged, w_o, x, mod)


def _prep_u_kernel(u_ref, o_ref):
    kq = o_ref.shape[-1]
    for j in range(o_ref.shape[0]):
        o_ref[j] = u_ref[:, j * kq:(j + 1) * kq].astype(BF16)


def _prep_u(peer_u):
    L, NE, D = peer_u.shape
    R = 2 * PEER_MC
    kq = D // PEER_KQ
    return pl.pallas_call(
        _prep_u_kernel,
        grid=(L, NE // R),
        in_specs=[pl.BlockSpec((None, R, D), lambda l, p: (l, p, 0))],
        out_specs=pl.BlockSpec((None, None, PEER_KQ, R, kq), lambda l, p: (l, p, 0, 0, 0)),
        out_shape=jax.ShapeDtypeStruct((L, NE // R, PEER_KQ, R, kq), BF16),
        compiler_params=_cparams(("parallel", "parallel")),
        name="prep_u",
    )(peer_u)


def _prep_v_kernel(v_ref, o_ref):
    kq = o_ref.shape[1]
    for j in range(o_ref.shape[0]):
        o_ref[j] = v_ref[:, j * kq:(j + 1) * kq].T.astype(BF16)


def _prep_v(peer_v):
    L, NE, D = peer_v.shape
    kq = D // PEER_KQ
    return pl.pallas_call(
        _prep_v_kernel,
        grid=(L, NE // PEER_MC),
        in_specs=[pl.BlockSpec((None, PEER_MC, D), lambda l, c: (l, c, 0))],
        out_specs=pl.BlockSpec((None, None, PEER_KQ, kq, PEER_MC), lambda l, c: (l, c, 0, 0, 0)),
        out_shape=jax.ShapeDtypeStruct((L, NE // PEER_MC, PEER_KQ, kq, PEER_MC), BF16),
        compiler_params=_cparams(("parallel", "parallel")),
        name="prep_v",
    )(peer_v)


def _peer_q_kernel(x_ref, nw_ref, sh_ref, sc_ref, wq_ref, k1_ref, k2_ref, h_ref, st_ref):
    h = _norm_mod(x_ref[...], nw_ref[...], sc_ref[...], sh_ref[...])
    hb = h.reshape(-1, h.shape[-1]).astype(BF16)
    kq = h_ref.shape[-1]
    for j in range(h_ref.shape[0]):
        h_ref[j] = hb[:, j * kq:(j + 1) * kq]
    q = _dot(hb, wq_ref[...])
    dq = k1_ref.shape[-1]
    for hd in range(PEER_HEADS):
        q1 = q[:, (2 * hd) * dq:(2 * hd + 1) * dq]
        q2 = q[:, (2 * hd + 1) * dq:(2 * hd + 2) * dq]
        st_ref[2 * hd] = _dot_nt_split(k1_ref[hd], q1)
        st_ref[2 * hd + 1] = _dot_nt_split(k2_ref[hd], q2)


def _peer_q(x, mod, l, moff, nw, wq, k1, k2):
    B, T, D = x.shape
    bb, tt = _tile_cfg(B, T, 256)
    TM, nT = bb * tt, T // tt
    full = lambda a: pl.BlockSpec(a.shape, lambda i: (0,) * a.ndim)
    mmap = lambda k: (lambda i: (l, moff // bb + i // nT, 0, k))
    return pl.pallas_call(
        _peer_q_kernel,
        grid=((B // bb) * nT,),
        in_specs=[pl.BlockSpec((bb, tt, D), lambda i: (i // nT, i % nT, 0)),
                  pl.BlockSpec((1, D), lambda i: (0, 0)),
                  pl.BlockSpec((None, bb, 1, D), mmap(3)),
                  pl.BlockSpec((None, bb, 1, D), mmap(4)),
                  pl.BlockSpec((None,) + wq.shape[1:], lambda i: (l, 0, 0)), full(k1), full(k2)],
        out_specs=[pl.BlockSpec((PEER_KQ, TM, D // PEER_KQ), lambda i: (0, i, 0)),
                   pl.BlockSpec((2 * PEER_HEADS, N_KEYS, TM), lambda i: (0, 0, i))],
        out_shape=[jax.ShapeDtypeStruct((PEER_KQ, B * T, D // PEER_KQ), BF16),
                   jax.ShapeDtypeStruct((2 * PEER_HEADS, N_KEYS, B * T), F32)],
        compiler_params=_cparams(("parallel",)),
        name="peer_q",
    )(x, nw, mod, mod, wq, k1, k2)


def _top_values(cur, n):
    vals = []
    for _ in range(n):
        m = jnp.max(cur, axis=0, keepdims=True)
        vals.append(m)
        cur = jnp.where(cur == m, -jnp.inf, cur)
    return vals


def _peer_sel_kernel(st_ref, e1_ref, e2_ref, th_ref):
    ts = st_ref.shape[-1]
    neg = jnp.full((1, ts), -jnp.inf, F32)
    n = PEER_TOPK + 1
    ths = []
    for hd in range(PEER_HEADS):
        s1 = st_ref[2 * hd]
        s2 = st_ref[2 * hd + 1]
        v1 = _top_values(s1, n)
        v2 = _top_values(s2, n)
        rows = [v1[r] + v2[c] for r in range(n) for c in range(n // (r + 1))]
        rows += [neg] * (-len(rows) % SUBLANES)
        top = _top_values(jnp.concatenate(rows, axis=0), n)
        den = jnp.exp(top[0] - top[0])
        for t in top[1:PEER_TOPK]:
            den = den + jnp.exp(t - top[0])
        e1_ref[hd] = jnp.exp(s1 - v1[0]) / den
        e2_ref[hd] = jnp.exp(s2 - v2[0])
        ths.append(jnp.exp(0.5 * (top[PEER_TOPK - 1] + top[PEER_TOPK]) - top[0]) / den)
    th_ref[...] = jnp.concatenate(ths, axis=0)


def _peer_sel(st):
    H2, K, M = st.shape
    TS = min(M, 256)
    hspec = pl.BlockSpec((PEER_HEADS, K, TS), lambda i: (0, 0, i))
    return pl.pallas_call(
        _peer_sel_kernel,
        grid=(M // TS,),
        in_specs=[pl.BlockSpec((H2, K, TS), lambda i: (0, 0, i))],
        out_specs=[hspec, hspec, pl.BlockSpec((PEER_HEADS, TS), lambda i: (0, i))],
        out_shape=[jax.ShapeDtypeStruct((PEER_HEADS, K, M), F32),
                   jax.ShapeDtypeStruct((PEER_HEADS, K, M), F32),
                   jax.ShapeDtypeStruct((PEER_HEADS, M), F32)],
        compiler_params=_cparams(("parallel",)),
        name="peer_sel",
    )(st)


def _peer_exp_kernel(h_ref, u_ref, vta_ref, vtb_ref, e1_ref, e2_ref, th_ref, x_ref, g_ref, o_ref,
                     acc_scr, sc0, sc1, zt0, zt1, *, mc, nc):
    s = pl.program_id(1)
    slabs = mc // N_KEYS

    @pl.when(s == 0)
    def _():
        acc_scr[...] = jnp.zeros(acc_scr.shape, F32)
        sc1[...] = jnp.zeros(sc1.shape, F32)
        zt0[...] = jnp.zeros(zt0.shape, BF16)
        zt1[...] = jnp.zeros(zt1.shape, BF16)

    nq = u_ref.shape[0]
    spp = slabs // nq
    assert spp * nq == slabs

    def substep(half, sc_a, sc_b, zt_b, zt_c, chunk_b):
        c = jnp.clip(chunk_b, 0, nc - 1)
        cols = slice(half * mc, (half + 1) * mc)
        sc_a[...] = jnp.zeros(sc_a.shape, F32)

        def piece(j, carry):
            acc_scr[j] += _dot((vta_ref, vtb_ref)[half][j], zt_c[...])
            for sl in range(spp):
                rs = pl.ds(pl.multiple_of((j * spp + sl) * N_KEYS, N_KEYS), N_KEYS)
                i1 = c * slabs + j * spp + sl
                for tc in range(sc_b.shape[1] // LANES):
                    tl = slice(tc * LANES, (tc + 1) * LANES)
                    w = jnp.zeros((N_KEYS, LANES), F32)
                    for hd in range(PEER_HEADS):
                        p = e1_ref[hd, pl.ds(i1, 1), :][:, tl] * e2_ref[hd, :, tl]
                        w = jnp.where(p >= th_ref[pl.ds(hd, 1), tl], w + p, w)
                    zt_b[rs, tl] = (jax.nn.gelu(sc_b[rs, tl]) * w).astype(BF16)
            sc_a[...] += _dot_nt(u_ref[j, cols, :], h_ref[j])
            return carry

        lax.fori_loop(0, nq, piece, 0)

    substep(0, sc0, sc1, zt1, zt0, 2 * s - 1)
    substep(1, sc1, sc0, zt0, zt1, 2 * s)

    @pl.when(s == pl.num_programs(1) - 1)
    def _():
        o_ref[...] = x_ref[...] + g_ref[...] * acc_scr[...].reshape(-1, acc_scr.shape[-1]).T.reshape(o_ref.shape)


def _peer_exp(hb, u, vt, e1, e2, th, x, mod, l, moff):
    B, T, D = x.shape
    bb, tt = _tile_cfg(B, T, PEER_TB)
    TB, nT = bb * tt, T // tt
    MC = PEER_MC
    NP = u.shape[1]
    NC = vt.shape[1]
    assert NC == 2 * NP and u.shape[3] == 2 * MC and vt.shape[4] == MC
    xspec = pl.BlockSpec((bb, tt, D), lambda i, s: (i // nT, i % nT, 0))
    hspec = pl.BlockSpec((PEER_HEADS, N_KEYS, TB), lambda i, s: (0, 0, i))
    return pl.pallas_call(
        functools.partial(_peer_exp_kernel, mc=MC, nc=NC),
        grid=((B // bb) * nT, NP + 1),
        in_specs=[pl.BlockSpec((PEER_KQ, TB, D // PEER_KQ), lambda i, s: (0, i, 0)),
                  pl.BlockSpec((None, None, PEER_KQ, 2 * MC, D // PEER_KQ),
                               lambda i, s: (l, jnp.minimum(s, NP - 1), 0, 0, 0)),
                  pl.BlockSpec((None, None, PEER_KQ, D // PEER_KQ, MC),
                               lambda i, s: (l, 2 * jnp.maximum(s - 1, 0), 0, 0, 0)),
                  pl.BlockSpec((None, None, PEER_KQ, D // PEER_KQ, MC),
                               lambda i, s: (l, 2 * jnp.maximum(s - 1, 0) + 1, 0, 0, 0)),
                  hspec, hspec,
                  pl.BlockSpec((PEER_HEADS, TB), lambda i, s: (0, i)),
                  xspec,
                  pl.BlockSpec((None, bb, 1, D), lambda i, s: (l, moff // bb + i // nT, 0, 5))],
        out_specs=xspec,
        out_shape=jax.ShapeDtypeStruct((B, T, D), F32),
        scratch_shapes=[pltpu.VMEM((PEER_KQ, D // PEER_KQ, TB), F32),
                        pltpu.VMEM((MC, TB), F32), pltpu.VMEM((MC, TB), F32),
                        pltpu.VMEM((MC, TB), BF16), pltpu.VMEM((MC, TB), BF16)],
        compiler_params=_cparams(("parallel", "arbitrary")),
        name="peer_exp",
    )(hb, u, vt, vt, e1, e2, th, x, mod)


def _final_kernel(x_ref, w_ref, o_ref):
    x = x_ref[...]
    ms = jnp.mean(x * x, axis=-1, keepdims=True)
    o_ref[...] = x * lax.rsqrt(ms + EPS) * w_ref[...]


def _final_norm(x, w):
    B, T, D = x.shape
    bb, tt = _tile_cfg(B, T, 512)
    nT = T // tt
    xspec = pl.BlockSpec((bb, tt, D), lambda i: (i // nT, i % nT, 0))
    return pl.pallas_call(
        _final_kernel,
        grid=((B // bb) * nT,),
        in_specs=[xspec, pl.BlockSpec((1, D), lambda i: (0, 0))],
        out_specs=xspec,
        out_shape=jax.ShapeDtypeStruct((B, T, D), F32),
        compiler_params=_cparams(("parallel",)),
        name="final_norm",
    )(x, w)


def _trunk(x, mod, moff, pos0, big_states, states, lw, final_w):
    B, T, D = x.shape
    depth = len(lw)
    rtabs = _ret_tables(T, pos0)
    ret_in, c_in = big_states
    n_ret = n_c = None
    new = []
    for l, (st, w) in enumerate(zip(states, lw)):
        s_n, s_m, s_conv, s_hr, s_hi = st
        z, zg = _in_proj(x, mod, l, moff, w['n1'], w['w_a'], w['w_b'], w['w_gate'])
        o_r, n_ret = _retention(z, B, T, ret_in, min(l, ret_in.shape[0] - 1), l, depth, n_ret, rtabs, w['ret_gn'])
        o_m, n_c, n_n, n_m, n_conv = _mlstm(z, zg, B, T, c_in, min(l, c_in.shape[0] - 1), l, depth, n_c,
                                            (s_n, s_m, s_conv), w['conv_w'], w['conv_b'],
                                            w['wqk'], w['gate_b'], w['m_gn'], w['m_skip'])
        o_s, n_hr, n_hi = _s5(z, B, T, s_hr, s_hi, w['s5_tabs'], w['s5_d'], w['s5_wglu'], w['s5_bglu'])
        merged = _branch(o_r, o_m, o_s, z, w['w_ret_out'], w['w_mlstm_out'], w['w_s5_out'], l)
        x = _wo(merged, w['w_o'], x, mod, l, moff)
        hb, sc = _peer_q(x, mod, l, moff, w['n2'], w['peer_wq'], w['peer_k1'], w['peer_k2'])
        e1, e2, th = _peer_sel(sc)
        x = _peer_exp(hb, w['peer_u'], w['peer_vt'], e1, e2, th, x, mod, l, moff)
        new.append((n_n, n_m[..., 0], n_conv, n_hr, n_hi))
    y = _final_norm(x, final_w)
    return y, [n_ret, n_c] + [jnp.stack([s[i] for s in new]) for i in range(5)]


def kernel(x_prompt, x_sample, state_ret, state_mlstm_c, state_mlstm_n, state_mlstm_m, state_mlstm_conv,
           state_s5_re, state_s5_im, c_prompt, c_sample, ada_w, ada_b, norm1_w, norm2_w, final_norm_w,
           w_in, ret_gn_w, w_ret_out, mlstm_conv_w, mlstm_conv_b, mlstm_wq, mlstm_wk, mlstm_b_i, mlstm_b_f,
           mlstm_gn_w, mlstm_skip, w_mlstm_out, s5_a_re, s5_a_im, s5_log_dt, s5_b_re, s5_b_im, s5_c_re,
           s5_c_im, s5_d, s5_w_glu, s5_b_glu, w_s5_out, w_o, peer_wq, peer_k1, peer_k2, peer_u, peer_v):
    depth = w_in.shape[0]
    Bp, Tp, D = x_prompt.shape
    Bs, Ts, _ = x_sample.shape
    W = HEADS * DH
    G, P = s5_a_re.shape[1:]
    NS = G * P
    past_len = 16384

    pad = -(Bs + Bp) % SUBLANES
    c_all = jnp.concatenate([c_sample, c_prompt, jnp.zeros((pad, D), F32)], axis=0)
    mod = _ada(c_all, ada_w, ada_b)
    mod = mod.reshape(depth, c_all.shape[0], 1, ada_w.shape[2])

    a_end = 7 * W
    g_end = a_end + 2 * HEADS
    w_ret_out_b, w_mlstm_out_b, w_s5_out_b = w_ret_out.astype(BF16), w_mlstm_out.astype(BF16), w_s5_out.astype(BF16)
    w_o_b, peer_wq_b = w_o.astype(BF16), peer_wq.astype(BF16)
    u_tiles = _prep_u(peer_u)
    vt_tiles = _prep_v(peer_v)
    lw = []
    for l in range(depth):
        w_gate = jnp.pad(w_in[l, :, a_end:g_end], ((0, 0), (0, LANES - 2 * HEADS))).astype(BF16)
        gate_b = jnp.pad(jnp.concatenate([mlstm_b_i[l], mlstm_b_f[l]]), (0, LANES - 2 * HEADS)).reshape(1, LANES)
        lw.append(dict(
            n1=norm1_w[l].reshape(1, D), n2=norm2_w[l].reshape(1, D),
            w_a=w_in[l, :, :a_end].astype(BF16), w_b=w_in[l, :, g_end:].astype(BF16), w_gate=w_gate, gate_b=gate_b,
            ret_gn=ret_gn_w[l].reshape(1, W),
            conv_w=mlstm_conv_w[l], conv_b=mlstm_conv_b[l].reshape(1, W),
            wqk=jnp.concatenate([mlstm_wq[l], mlstm_wk[l]], axis=-1).astype(BF16),
            m_gn=mlstm_gn_w[l].reshape(1, W), m_skip=mlstm_skip[l].reshape(1, W),
            s5_tabs=_s5_tables(s5_a_re[l], s5_a_im[l], s5_log_dt[l], s5_b_re[l], s5_b_im[l], s5_c_re[l], s5_c_im[l]),
            s5_d=s5_d[l].reshape(1, -1), s5_wglu=s5_w_glu[l].astype(BF16), s5_bglu=s5_b_glu[l].reshape(1, -1),
            w_ret_out=w_ret_out_b, w_mlstm_out=w_mlstm_out_b, w_s5_out=w_s5_out_b, w_o=w_o_b,
            peer_wq=peer_wq_b, peer_k1=peer_k1[l], peer_k2=peer_k2[l],
            peer_u=u_tiles, peer_vt=vt_tiles))

    def zero_states(B):
        return (jnp.zeros((B, HEADS, DH), F32), jnp.zeros((B, HEADS, LANES), F32),
                jnp.zeros((B, CONV_W - 1, W), F32), jnp.zeros((B, 1, NS), F32), jnp.zeros((B, 1, NS), F32))

    zero_big = jnp.zeros((1, Bp, HEADS, DH, DH), F32)
    prompt_states = [zero_states(Bp) for _ in range(depth)]
    sample_states = [(state_mlstm_n[l],
                      jnp.broadcast_to(state_mlstm_m[l][..., None], (Bs, HEADS, LANES)),
                      state_mlstm_conv[l], state_s5_re[l].reshape(Bs, 1, NS), state_s5_im[l].reshape(Bs, 1, NS))
                     for l in range(depth)]

    y_p, ps = _trunk(x_prompt, mod, Bs, 0, (zero_big, zero_big), prompt_states, lw, final_norm_w.reshape(1, D))
    y_s, ss = _trunk(x_sample, mod, 0, past_len, (state_ret, state_mlstm_c), sample_states, lw,
                     final_norm_w.reshape(1, D))

    def unpack(st, B):
        r, c, n, m, conv, hr, hi = st
        return (r, c, n, m, conv, hr.reshape(depth, B, G, P), hi.reshape(depth, B, G, P))

    return (y_p, y_s) + unpack(ps, Bp) + unpack(ss, Bs)
```
